```python
import math, functools
import jax, jax.numpy as jnp
from jax import lax
import numpy as np

D_MODEL = 2048
BATCH = 4
SEQ = 4096
DEPTH = 2
DEC_BATCH = 16
DEC_SEQ = 16
PAST_LEN = 2048

CHUNK = 64
N_BRANCH = 3
BRANCH_WIDTH = 1024
POOL_WIDTH = BRANCH_WIDTH
POOL_WINDOWS = (2, 4, 8, 16)
POOL_GROUPS = len(POOL_WINDOWS)
POOL_GROUP_DIM = POOL_WIDTH // POOL_GROUPS
POOL_HIST = max(POOL_WINDOWS) - 1
RWKV_WIDTH = BRANCH_WIDTH
RWKV_HEAD = 64
RWKV_HEADS = RWKV_WIDTH // RWKV_HEAD
DECAY_RANK = 64
ICLR_RANK = 64
SHIFT_WIDTH = 3 * RWKV_WIDTH + DECAY_RANK + ICLR_RANK
GN_EPS = 64e-5
ATT_HEADS = 16
ATT_KV_HEADS = 4
ATT_GROUP = ATT_HEADS // ATT_KV_HEADS
ATT_HEAD_DIM = 64
ATT_WIDTH = ATT_HEADS * ATT_HEAD_DIM
KV_WIDTH = ATT_KV_HEADS * ATT_HEAD_DIM
WINDOW = 128
BAND_CHUNKS = WINDOW // CHUNK
BAND_KEYS = (BAND_CHUNKS + 1) * CHUNK
ATT_SCALE = ATT_HEAD_DIM ** -0.5
N_BUCKETS = 32
MAX_DISTANCE = 128
NEG_INF = -1e30
RMS_EPS = 1e-6
OFF_POOL = 0
OFF_SHIFT = OFF_POOL + POOL_WIDTH
OFF_Q = OFF_SHIFT + SHIFT_WIDTH
OFF_K = OFF_Q + ATT_WIDTH
OFF_V = OFF_K + KV_WIDTH
OFF_GATE = OFF_V + KV_WIDTH
OFF_MERGE = OFF_GATE + N_BRANCH * BRANCH_WIDTH
IN_WIDTH = OFF_MERGE + N_BRANCH * D_MODEL

kernel_name = 'hybrid_pool_rwkv7_swa_stream_step'


def rms_norm(x, g):
    xf = x.astype(jnp.float32)
    y = xf * lax.rsqrt(jnp.mean(xf * xf, axis=-1, keepdims=True) + RMS_EPS)
    return (y * g.astype(jnp.float32)).astype(x.dtype)


def t5_bucket(rel):
    half = N_BUCKETS // 2
    n = -rel
    ret = jnp.where(n < 0, half, 0)
    n = jnp.abs(n)
    max_exact = half // 2
    large = max_exact + (jnp.log(jnp.maximum(n, 1).astype(jnp.float32) / max_exact)
                         / math.log(MAX_DISTANCE / max_exact) * (half - max_exact)).astype(jnp.int32)
    large = jnp.minimum(large, half - 1)
    return ret + jnp.where(n < max_exact, n, large)


def rel_bias(table, rel):
    return jnp.transpose(table[t5_bucket(rel)], (2, 0, 1)).astype(jnp.float32)


def sink_softmax(logits, sink):
    sink = jnp.broadcast_to(sink.astype(jnp.float32), logits.shape[:-1] + (1,))
    return jax.nn.softmax(jnp.concatenate([logits, sink], axis=-1), axis=-1)[..., :-1]


def pool_mix(u, hist, pos0):
    T = u.shape[1]
    ext = jnp.concatenate([hist.astype(u.dtype), u], axis=1).astype(jnp.float32)
    cs = jnp.concatenate([jnp.zeros_like(ext[:, :1]), jnp.cumsum(ext, axis=1)], axis=1)
    end = cs[:, POOL_HIST + 1:]
    pos = pos0 + jnp.arange(T)
    outs = []
    for g, w in enumerate(POOL_WINDOWS):
        sl = slice(g * POOL_GROUP_DIM, (g + 1) * POOL_GROUP_DIM)
        s = end[..., sl] - cs[:, POOL_HIST + 1 - w:POOL_HIST + 1 - w + T, sl]
        cnt = jnp.minimum(w, pos + 1).astype(jnp.float32)[None, :, None]
        outs.append(s / cnt)
    return (jnp.concatenate(outs, axis=-1) - u.astype(jnp.float32)).astype(u.dtype)


def rwkv_scan(r, w, k, v, a, b, s0):
    def step(s, inp):
        r_t, w_t, k_t, v_t, a_t, b_t = inp
        sa = jnp.einsum('bhij,bhj->bhi', s, a_t)
        s = s * w_t[:, :, None, :] + sa[..., None] * b_t[:, :, None, :] + v_t[..., None] * k_t[:, :, None, :]
        return s, jnp.einsum('bhij,bhj->bhi', s, r_t)
    xs = tuple(jnp.moveaxis(z, 1, 0) for z in (r, w, k, v, a, b))
    s, ys = lax.scan(step, s0, xs)
    return jnp.moveaxis(ys, 0, 1), s


def rwkv_branch(p, prev_row, s0, mu, w0, w_up, a0, a_up, k_k, k_a, r_k, gn_w, gn_b):
    B, T, _ = p.shape
    W = RWKV_WIDTH
    prev = jnp.concatenate([prev_row[:, None, :].astype(p.dtype), p[:, :-1]], axis=1)
    xs = p + (prev - p) * mu
    r, k, v = xs[..., :W], xs[..., W:2 * W], xs[..., 2 * W:3 * W]
    wd, ad = xs[..., 3 * W:3 * W + DECAY_RANK], xs[..., 3 * W + DECAY_RANK:]
    w_log = -jax.nn.softplus(-(w0 + jnp.tanh(wd) @ w_up)) - 0.5
    decay = jnp.exp(-jnp.exp(w_log.astype(jnp.float32)))
    a = jax.nn.sigmoid(a0 + ad @ a_up)
    heads = lambda z: z.astype(jnp.float32).reshape(B, T, RWKV_HEADS, RWKV_HEAD)
    kk = heads(k * k_k)
    kk = kk / jnp.maximum(jnp.sqrt(jnp.sum(kk * kk, axis=-1, keepdims=True)), 1e-12)
    k = k * (1 + (a - 1) * k_a)
    rh, kh, vh, ah, wh = heads(r), heads(k), heads(v), heads(a), heads(decay)
    y, s = rwkv_scan(rh, wh, kh, vh, -kk, kk * ah, s0.astype(jnp.float32))
    mean = jnp.mean(y, axis=-1, keepdims=True)
    var = jnp.mean(jnp.square(y - mean), axis=-1, keepdims=True)
    y = ((y - mean) * lax.rsqrt(var + GN_EPS)).reshape(B, T, W) * gn_w.astype(jnp.float32) + gn_b.astype(jnp.float32)
    bonus = jnp.sum(rh * kh * r_k.astype(jnp.float32).reshape(RWKV_HEADS, RWKV_HEAD), axis=-1, keepdims=True) * vh
    y = y + bonus.reshape(B, T, W)
    return y.astype(p.dtype), s


def swa_prompt(q, k, v, sink, bias):
    B, T = q.shape[:2]
    nc = T // CHUNK
    qb = q.reshape(B, nc, CHUNK, ATT_KV_HEADS, ATT_GROUP, ATT_HEAD_DIM)

    def band(z):
        zb = z.reshape(B, nc, CHUNK, ATT_KV_HEADS, ATT_HEAD_DIM)
        zp = jnp.pad(zb, ((0, 0), (BAND_CHUNKS, 0), (0, 0), (0, 0), (0, 0)))
        return jnp.concatenate([zp[:, i:i + nc] for i in range(BAND_CHUNKS + 1)], axis=2)

    kb, vb = band(k), band(v)
    logits = jnp.einsum('bnqkgd,bnskd->bnkgqs', qb, kb).astype(jnp.float32) * ATT_SCALE
    logits = logits + bias.reshape(ATT_KV_HEADS, ATT_GROUP, CHUNK, BAND_KEYS)
    key_chunk = jnp.arange(nc)[:, None] - BAND_CHUNKS + jnp.arange(BAND_KEYS)[None, :] // CHUNK
    logits = jnp.where((key_chunk >= 0)[None, :, None, None, None, :], logits, NEG_INF)
    probs = sink_softmax(logits, sink.reshape(ATT_KV_HEADS, ATT_GROUP)[:, :, None, None])
    out = jnp.einsum('bnkgqs,bnskd->bnqkgd', probs.astype(vb.dtype), vb)
    return out.reshape(B, T, ATT_WIDTH)


def swa_sample(q, k, v, k_cache, v_cache, sink, bias):
    B, T = q.shape[:2]
    kk = jnp.concatenate([k_cache.astype(k.dtype), k], axis=1)
    vv = jnp.concatenate([v_cache.astype(v.dtype), v], axis=1)
    S = kk.shape[1]
    qh = q.reshape(B, T, ATT_KV_HEADS, ATT_GROUP, ATT_HEAD_DIM)
    logits = jnp.einsum('btkgd,bskd->bkgts', qh, kk).astype(jnp.float32) * ATT_SCALE
    logits = logits + bias.reshape(ATT_KV_HEADS, ATT_GROUP, T, S)
    probs = sink_softmax(logits, sink.reshape(ATT_KV_HEADS, ATT_GROUP)[:, :, None, None])
    out = jnp.einsum('bkgts,bskd->btkgd', probs.astype(vv.dtype), vv)
    return out.reshape(B, T, ATT_WIDTH)


def mixer_layer(x, pos0, pool_hist, shift_prev, s0, attn_fn, norm_g, w_in, pool_w, pool_scale, mu, w0, w_up,
                a0, a_up, k_k, k_a, r_k, gn_w, gn_b, w_branch, w_out):
    B, T, _ = x.shape
    h = rms_norm(x, norm_g)
    proj = h @ w_in
    u_pool = proj[..., OFF_POOL:OFF_SHIFT]
    p_shift = proj[..., OFF_SHIFT:OFF_Q]
    q = proj[..., OFF_Q:OFF_K].reshape(B, T, ATT_HEADS, ATT_HEAD_DIM)
    k = proj[..., OFF_K:OFF_V].reshape(B, T, ATT_KV_HEADS, ATT_HEAD_DIM)
    v = proj[..., OFF_V:OFF_GATE].reshape(B, T, ATT_KV_HEADS, ATT_HEAD_DIM)
    zg = jax.nn.silu(proj[..., OFF_GATE:OFF_MERGE]).reshape(B, T, N_BRANCH, BRANCH_WIDTH)
    mg = jax.nn.sigmoid(proj[..., OFF_MERGE:]).reshape(B, T, N_BRANCH, D_MODEL)
    pooled = pool_mix(u_pool, pool_hist, pos0).reshape(B, T, POOL_GROUPS, POOL_GROUP_DIM)
    o_a = jnp.einsum('btgc,gcd->btgd', pooled, pool_w).reshape(B, T, POOL_WIDTH) * pool_scale
    o_b, s_new = rwkv_branch(p_shift, shift_prev, s0, mu, w0, w_up, a0, a_up, k_k, k_a, r_k, gn_w, gn_b)
    o_c = attn_fn(q, k, v).astype(x.dtype)
    merged = (mg[:, :, 0] * ((o_a * zg[:, :, 0]) @ w_branch[0])
              + mg[:, :, 1] * ((o_b * zg[:, :, 1]) @ w_branch[1])
              + mg[:, :, 2] * ((o_c * zg[:, :, 2]) @ w_branch[2]))
    y = x + merged @ w_out
    new_pool = jnp.concatenate([pool_hist.astype(u_pool.dtype), u_pool], axis=1)[:, -POOL_HIST:]
    return y, k, v, s_new, p_shift[:, -1], new_pool


def setup_inputs(seed: int = 0) -> dict:
    key = jax.random.key(seed)
    ks = jax.random.split(key, 26)
    f32 = jnp.float32
    nrm = lambda kk, shape, s: s * jax.random.normal(kk, shape, f32)
    n_cache = min(WINDOW, PAST_LEN)
    return {
        'x_prompt': nrm(ks[0], (BATCH, SEQ, D_MODEL), 1.0),
        'x_sample': nrm(ks[1], (DEC_BATCH, DEC_SEQ, D_MODEL), 1.0),
        'cache_attn_k': nrm(ks[2], (DEPTH, DEC_BATCH, n_cache, ATT_KV_HEADS, ATT_HEAD_DIM), 1.0),
        'cache_attn_v': nrm(ks[3], (DEPTH, DEC_BATCH, n_cache, ATT_KV_HEADS, ATT_HEAD_DIM), 1.0),
        'state_rwkv': nrm(ks[4], (DEPTH, DEC_BATCH, RWKV_HEADS, RWKV_HEAD, RWKV_HEAD), 0.3),
        'state_rwkv_shift': nrm(ks[5], (DEPTH, DEC_BATCH, SHIFT_WIDTH), 1.0),
        'state_pool': nrm(ks[6], (DEPTH, DEC_BATCH, POOL_HIST, POOL_WIDTH), 1.0),
        'norm_g': 1.0 + nrm(ks[7], (DEPTH, D_MODEL), 0.05),
        'w_in': nrm(ks[8], (DEPTH, D_MODEL, IN_WIDTH), D_MODEL ** -0.5),
        'pool_w': nrm(ks[9], (DEPTH, POOL_GROUPS, POOL_GROUP_DIM, POOL_GROUP_DIM), POOL_GROUP_DIM ** -0.5),
        'pool_scale': 1.0 + nrm(ks[10], (DEPTH, POOL_WIDTH), 0.1),
        'rwkv_mu': jax.random.uniform(ks[11], (DEPTH, SHIFT_WIDTH), f32),
        'rwkv_w0': jax.random.uniform(ks[12], (DEPTH, RWKV_WIDTH), f32, -6.0, 1.0),
        'rwkv_w_up': nrm(ks[13], (DEPTH, DECAY_RANK, RWKV_WIDTH), 0.5 * DECAY_RANK ** -0.5),
        'rwkv_a0': nrm(ks[14], (DEPTH, RWKV_WIDTH), 0.1),
        'rwkv_a_up': nrm(ks[15], (DEPTH, ICLR_RANK, RWKV_WIDTH), 0.5 * ICLR_RANK ** -0.5),
        'rwkv_k_k': 0.85 + nrm(ks[16], (DEPTH, RWKV_WIDTH), 0.05),
        'rwkv_k_a': 1.0 + nrm(ks[17], (DEPTH, RWKV_WIDTH), 0.05),
        'rwkv_r_k': nrm(ks[18], (DEPTH, RWKV_WIDTH), 0.1),
        'rwkv_gn_w': 1.0 + nrm(ks[19], (DEPTH, RWKV_WIDTH), 0.05),
        'rwkv_gn_b': nrm(ks[20], (DEPTH, RWKV_WIDTH), 0.02),
        'attn_sink': nrm(ks[21], (DEPTH, ATT_HEADS), 0.5),
        'rel_pos_table': nrm(ks[22], (N_BUCKETS, ATT_HEADS), 0.5),
        'w_branch': nrm(ks[23], (DEPTH, N_BRANCH, BRANCH_WIDTH, D_MODEL), BRANCH_WIDTH ** -0.5),
        'w_out': nrm(ks[24], (DEPTH, D_MODEL, D_MODEL), 0.5 * D_MODEL ** -0.5),
        'final_norm_g': 1.0 + nrm(ks[25], (D_MODEL,), 0.05),
    }


def reference(x_prompt, x_sample, cache_attn_k, cache_attn_v, state_rwkv, state_rwkv_shift, state_pool,
              norm_g, w_in, pool_w, pool_scale, rwkv_mu, rwkv_w0, rwkv_w_up, rwkv_a0, rwkv_a_up, rwkv_k_k,
              rwkv_k_a, rwkv_r_k, rwkv_gn_w, rwkv_gn_b, attn_sink, rel_pos_table, w_branch, w_out, final_norm_g):
    bp = x_prompt.shape[0]
    rel_p = (jnp.arange(BAND_KEYS) - BAND_CHUNKS * CHUNK)[None, :] - jnp.arange(CHUNK)[:, None]
    bias_p = rel_bias(rel_pos_table, rel_p)
    n_cache = cache_attn_k.shape[2]
    t_s = x_sample.shape[1]
    key_s = jnp.concatenate([jnp.arange(n_cache) - n_cache, jnp.arange(t_s)])
    bias_s = rel_bias(rel_pos_table, key_s[None, :] - jnp.arange(t_s)[:, None])

    xp, xs = x_prompt, x_sample
    kp_l, vp_l, sp_l, shp_l, plp_l = [], [], [], [], []
    ks_l, vs_l, ss_l, shs_l, pls_l = [], [], [], [], []
    for l in range(DEPTH):
        lw = (norm_g[l], w_in[l], pool_w[l], pool_scale[l], rwkv_mu[l], rwkv_w0[l], rwkv_w_up[l], rwkv_a0[l],
              rwkv_a_up[l], rwkv_k_k[l], rwkv_k_a[l], rwkv_r_k[l], rwkv_gn_w[l], rwkv_gn_b[l], w_branch[l], w_out[l])
        attn_p = functools.partial(swa_prompt, sink=attn_sink[l], bias=bias_p)
        xp, kp, vp, sp, shp, plp = mixer_layer(
            xp, 0, jnp.zeros((bp, POOL_HIST, POOL_WIDTH), xp.dtype), jnp.zeros((bp, SHIFT_WIDTH), xp.dtype),
            jnp.zeros((bp, RWKV_HEADS, RWKV_HEAD, RWKV_HEAD), jnp.float32), attn_p, *lw)
        kp_l.append(kp[:, -WINDOW:]); vp_l.append(vp[:, -WINDOW:])
        sp_l.append(sp); shp_l.append(shp); plp_l.append(plp)
        attn_s = functools.partial(swa_sample, k_cache=cache_attn_k[l], v_cache=cache_attn_v[l],
                                   sink=attn_sink[l], bias=bias_s)
        xs, k_s, v_s, s_s, sh_s, pl_s = mixer_layer(
            xs, PAST_LEN, state_pool[l], state_rwkv_shift[l], state_rwkv[l], attn_s, *lw)
        ks_l.append(k_s); vs_l.append(v_s); ss_l.append(s_s); shs_l.append(sh_s); pls_l.append(pl_s)

    y_prompt = rms_norm(xp, final_norm_g)
    y_sample = rms_norm(xs, final_norm_g)
    return (y_prompt, y_sample,
            jnp.stack(kp_l), jnp.stack(vp_l), jnp.stack(ks_l), jnp.stack(vs_l),
            jnp.stack(sp_l), jnp.stack(ss_l), jnp.stack(shp_l), jnp.stack(shs_l),
            jnp.stack(plp_l), jnp.stack(pls_l))
```

```python
import functools
import math

import jax
import jax.numpy as jnp
import numpy as np
from jax import lax
from jax.experimental import pallas as pl
from jax.experimental.pallas import tpu as pltpu

F32 = jnp.float32
BF16 = jnp.bfloat16

D_MODEL = 2048
BRANCH_WIDTH = 1024
POOL_WINDOWS = (2, 4, 8, 16)
POOL_GROUP_DIM = BRANCH_WIDTH // len(POOL_WINDOWS)
POOL_HIST = max(POOL_WINDOWS) - 1
HEAD_DIM = 64
RWKV_HEADS = BRANCH_WIDTH // HEAD_DIM
LORA_RANK = 64
SHIFT_WIDTH = 3 * BRANCH_WIDTH + 2 * LORA_RANK
GN_EPS = 64e-5
ATT_HEADS = 16
ATT_KV_HEADS = 4
ATT_GROUP = ATT_HEADS // ATT_KV_HEADS
KV_WIDTH = ATT_KV_HEADS * HEAD_DIM
CHUNK = 64
WINDOW = 128
BAND_CHUNKS = WINDOW // CHUNK
N_BUCKETS = 32
MAX_DISTANCE = 128
NEG_INF = -1e30
RMS_EPS = 1e-6
PAST_LEN = 2048
N_BRANCH = 3
OFF_POOL = 0
OFF_SHIFT = OFF_POOL + BRANCH_WIDTH
OFF_Q = OFF_SHIFT + SHIFT_WIDTH
OFF_K = OFF_Q + BRANCH_WIDTH
OFF_V = OFF_K + KV_WIDTH
OFF_GATE = OFF_V + KV_WIDTH
OFF_MERGE = OFF_GATE + N_BRANCH * BRANCH_WIDTH

LANES = 128
PAIRS = BRANCH_WIDTH // LANES
KEY_PAD = 256
VMEM_LIMIT = 56 * 1024 * 1024


def _params(*sem):
    return pltpu.CompilerParams(dimension_semantics=sem, vmem_limit_bytes=VMEM_LIMIT)


def _sigmoid(x):
    return 1.0 / (1.0 + jnp.exp(-x))


def _dot(a, b):
    return jnp.dot(a, b, preferred_element_type=F32)


def _dot_nt(a, b):
    return lax.dot_general(a, b, (((1,), (1,)), ((), ())), preferred_element_type=F32)


def _split3(x):
    hi = x.astype(BF16)
    r1 = x - hi.astype(F32)
    mid = r1.astype(BF16)
    lo = (r1 - mid.astype(F32)).astype(BF16)
    return hi, mid, lo


def _dot3_right_exact(x, b):
    hi, mid, lo = _split3(x)
    return _dot(hi, b) + _dot(mid, b) + _dot(lo, b)


def _dot3_left_exact(a, x):
    hi, mid, lo = _split3(x)
    return _dot(a, hi) + _dot(a, mid) + _dot(a, lo)


def _rmsnorm_kernel(x_ref, g_ref, o_ref):
    x = x_ref[...]
    ms = jnp.mean(x * x, axis=-1, keepdims=True)
    o_ref[...] = (x * lax.rsqrt(ms + RMS_EPS) * g_ref[...]).astype(o_ref.dtype)


def _rmsnorm(x2d, g, out_dtype, tm):
    m, d = x2d.shape
    tm = min(tm, m)
    return pl.pallas_call(
        _rmsnorm_kernel,
        grid=(m // tm,),
        in_specs=[pl.BlockSpec((tm, d), lambda i: (i, 0)), pl.BlockSpec((1, d), lambda i: (0, 0))],
        out_specs=pl.BlockSpec((tm, d), lambda i: (i, 0)),
        out_shape=jax.ShapeDtypeStruct((m, d), out_dtype),
        compiler_params=_params("parallel"),
        name="rmsnorm",
    )(x2d, g.reshape(1, d))


def _mm_kernel(a_ref, w_ref, o_ref, *, act):
    acc = _dot(a_ref[...], w_ref[...])
    if act == "silu":
        acc = acc * _sigmoid(acc)
    elif act == "sigmoid":
        acc = _sigmoid(acc)
    o_ref[...] = acc.astype(o_ref.dtype)


def _matmul(a, w, *, act, out_dtype, tm, tn, name):
    m, k = a.shape
    n = w.shape[1]
    tm = min(tm, m)
    return pl.pallas_call(
        functools.partial(_mm_kernel, act=act),
        grid=(m // tm, n // tn),
        in_specs=[pl.BlockSpec((tm, k), lambda i, j: (i, 0)), pl.BlockSpec((k, tn), lambda i, j: (0, j))],
        out_specs=pl.BlockSpec((tm, tn), lambda i, j: (i, j)),
        out_shape=jax.ShapeDtypeStruct((m, n), out_dtype),
        compiler_params=_params("parallel", "arbitrary"),
        name=name,
    )(a, w)


def _pool_kernel(u_ref, prev_ref, hist_ref, zg_ref, w_ref, scale_ref, o_ref, *, tt, pos0):
    i = pl.program_id(1)
    u = u_ref[0]
    prev = jnp.where(i == 0, hist_ref[0], prev_ref[0])
    pos = pos0 + i * tt + lax.broadcasted_iota(jnp.int32, (tt, POOL_GROUP_DIM), 0)
    for g, w in enumerate(POOL_WINDOWS):
        sl = slice(g * POOL_GROUP_DIM, (g + 1) * POOL_GROUP_DIM)
        e = jnp.concatenate([prev[:, sl], u[:, sl]], axis=0)
        step = 1
        while step < w:
            e = e + pltpu.roll(e, step, 0)
            step *= 2
        s = e[16:, :]
        cnt = jnp.minimum(w, pos + 1).astype(F32)
        pooled = s / cnt - u[:, sl]
        o = _dot(pooled.astype(BF16), w_ref[g]) * scale_ref[:, sl]
        o_ref[0, :, sl] = (o * zg_ref[0, :, sl].astype(F32)).astype(o_ref.dtype)


def _pool_branch(u, hist16, gate, pool_w, pool_scale, pos0):
    b, t, c = u.shape
    tt = min(t, 512)
    sub = tt // 16
    return pl.pallas_call(
        functools.partial(_pool_kernel, tt=tt, pos0=pos0),
        grid=(b, t // tt),
        in_specs=[
            pl.BlockSpec((1, tt, c), lambda bi, i: (bi, i, 0)),
            pl.BlockSpec((1, 16, c), lambda bi, i: (bi, jnp.maximum(i * sub - 1, 0), 0)),
            pl.BlockSpec((1, 16, c), lambda bi, i: (bi, 0, 0)),
            pl.BlockSpec((1, tt, c), lambda bi, i: (bi, i, 0)),
            pl.BlockSpec(pool_w.shape, lambda bi, i: (0, 0, 0)),
            pl.BlockSpec((1, c), lambda bi, i: (0, 0)),
        ],
        out_specs=pl.BlockSpec((1, tt, c), lambda bi, i: (bi, i, 0)),
        out_shape=jax.ShapeDtypeStruct((b, t, c), BF16),
        compiler_params=_params("parallel", "parallel"),
        name="pool_branch",
    )(u, u, hist16, gate, pool_w, pool_scale)


def _rwkv_kernel(p_ref, sprev_ref, s0_ref, zg_ref, mu_ref, vec_ref, wl_ref, o_ref, s_ref, carry_ref,
                 *, n_valid):
    c = pl.program_id(1)
    C = CHUNK

    @pl.when(c == 0)
    def _():
        s_ref[...] = s0_ref[...]
        carry_ref[...] = sprev_ref[0]

    p = p_ref[0]
    row_w = lax.broadcasted_iota(jnp.int32, (C, SHIFT_WIDTH), 0)
    row = lax.broadcasted_iota(jnp.int32, (C, BRANCH_WIDTH), 0)
    prev = jnp.where(row_w == 0, carry_ref[...], pltpu.roll(p, 1, 0))
    carry_ref[...] = p[C - 1:C, :]
    xs = p + (prev - p) * mu_ref[...]

    w0, a0, k_k, k_a, r_k, gn_w, gn_b = (vec_ref[i:i + 1, :] for i in range(7))
    lane = lax.broadcasted_iota(jnp.int32, (C, LANES), 1)
    head0 = lane < HEAD_DIM
    lin = xs[:, 3 * BRANCH_WIDTH:]
    lin = jnp.where(head0, jnp.tanh(lin), lin)
    lora = _dot(lin.astype(BF16), wl_ref[...])
    log_decay = -math.exp(-0.5) * _sigmoid(w0 + lora[:, :BRANCH_WIDTH])
    iclr = _sigmoid(a0 + lora[:, BRANCH_WIDTH:])
    r = xs[:, :BRANCH_WIDTH]
    k = xs[:, BRANCH_WIDTH:2 * BRANCH_WIDTH]
    v = xs[:, 2 * BRANCH_WIDTH:3 * BRANCH_WIDTH]
    kk_raw = k * k_k
    k2 = k * (1.0 + (iclr - 1.0) * k_a)
    if n_valid < C:
        ok = row < n_valid
        log_decay = jnp.where(ok, log_decay, 0.0)
        kk_raw = jnp.where(ok, kk_raw, 0.0)
        k2 = jnp.where(ok, k2, 0.0)
        r = jnp.where(ok, r, 0.0)
        v = jnp.where(ok, v, 0.0)

    ti = lax.broadcasted_iota(jnp.int32, (C, C), 0)
    si = lax.broadcasted_iota(jnp.int32, (C, C), 1)
    tri = (si <= ti).astype(BF16)
    cum = _dot3_left_exact(tri, log_decay)
    cum_x = cum - log_decay
    cum_c = cum[C - 1:C, :]

    l0 = lax.broadcasted_iota(jnp.int32, (LANES, LANES), 0)
    l1 = lax.broadcasted_iota(jnp.int32, (LANES, LANES), 1)
    same_head = (l0 >= HEAD_DIM) == (l1 >= HEAD_DIM)
    ones_head = same_head.astype(BF16)
    eye = (l0 == l1).astype(F32)
    ri = lax.broadcasted_iota(jnp.int32, (2 * C, 2 * C), 0)
    ci = lax.broadcasted_iota(jnp.int32, (2 * C, 2 * C), 1)
    same_blk = (ri >= C) == (ci >= C)
    strict = same_blk & ((ci & (C - 1)) < (ri & (C - 1)))
    incl = same_blk & ((ci & (C - 1)) <= (ri & (C - 1)))
    eye2 = (ri == ci).astype(F32)

    def stack(x):
        return jnp.concatenate([jnp.where(head0, x, 0.0), jnp.where(head0, 0.0, x)], axis=0)

    def dup(x):
        return jnp.concatenate([x, x], axis=0)

    for pr in range(PAIRS):
        sl = slice(pr * LANES, (pr + 1) * LANES)
        r_p, k_p, v_p, kkr, ic = r[:, sl], k2[:, sl], v[:, sl], kk_raw[:, sl], iclr[:, sl]
        cm, cx, cc = cum[:, sl], cum_x[:, sl], cum_c[:, sl]
        n2 = _dot3_right_exact(kkr * kkr, ones_head)
        kk = kkr / jnp.maximum(jnp.sqrt(n2), 1e-12)
        a_p = -kk
        b_p = kk * ic
        e_neg = jnp.exp(-cm)
        e_end = jnp.exp(cc - cm)
        w_c = jnp.exp(cc)
        at_s = stack(a_p * jnp.exp(cx))
        rt_s = stack(r_p * jnp.exp(cm))
        v_s = stack(v_p)
        bb_s = stack(b_p * e_end)
        kb_s = stack(k_p * e_end)
        at_b, rt_b, v_b = at_s.astype(BF16), rt_s.astype(BF16), v_s.astype(BF16)
        bt_d = dup(b_p * e_neg).astype(BF16)
        kt_d = dup(k_p * e_neg).astype(BF16)
        a_ab = jnp.where(strict, _dot_nt(at_b, bt_d), 0.0)
        a_ak = jnp.where(strict, _dot_nt(at_b, kt_d), 0.0)
        a_rb = jnp.where(incl, _dot_nt(rt_b, bt_d), 0.0)
        a_rk = jnp.where(incl, _dot_nt(rt_b, kt_d), 0.0)
        tinv = eye2 + a_ab
        pw = a_ab
        for _ in range(int(math.log2(C)) - 1):
            pw_b = pw.astype(BF16)
            pw = _dot(pw_b, pw_b)
            tinv = tinv + _dot(tinv.astype(BF16), pw.astype(BF16))
        t_b = tinv.astype(BF16)
        a_rb_b = a_rb.astype(BF16)
        akv = _dot(a_ak.astype(BF16), v_b)
        ah = _dot(t_b, at_b)
        u0 = _dot(t_b, akv.astype(BF16))
        ah_b, u0_b = ah.astype(BF16), u0.astype(BF16)
        rh = rt_s + _dot(a_rb_b, ah_b)
        y0 = _dot(a_rb_b, u0_b) + _dot(a_rk.astype(BF16), v_b)
        bb_b, kb_b = bb_s.astype(BF16), kb_s.astype(BF16)
        m_lr = _dot(ah.T.astype(BF16), bb_b)
        g_bd = _dot(u0.T.astype(BF16), bb_b) + _dot(v_s.T.astype(BF16), kb_b)
        s_bd = s_ref[0, pr]
        s_b = s_bd.astype(BF16)
        y = _dot_nt((rh[:C] + rh[C:]).astype(BF16), s_b) + (y0[:C] + y0[C:])
        s_ref[0, pr] = s_bd * w_c + _dot(s_b, m_lr.astype(BF16)) + g_bd
        mean = _dot3_right_exact(y, ones_head) * (1.0 / HEAD_DIM)
        yc = y - mean
        var = _dot3_right_exact(yc * yc, ones_head) * (1.0 / HEAD_DIM)
        yn = yc * lax.rsqrt(var + GN_EPS) * gn_w[:, sl] + gn_b[:, sl]
        bonus = _dot3_right_exact(r_p * k_p * r_k[:, sl], ones_head) * v_p
        o_ref[0, :, sl] = ((yn + bonus) * zg_ref[0, :, sl].astype(F32)).astype(o_ref.dtype)


def _rwkv_branch(p_shift, shift_prev, s0_bd, gate, mu, vecs, w_lora, n_valid):
    b, t, _ = p_shift.shape
    return pl.pallas_call(
        functools.partial(_rwkv_kernel, n_valid=n_valid),
        grid=(b, t // CHUNK),
        in_specs=[
            pl.BlockSpec((1, CHUNK, SHIFT_WIDTH), lambda bi, c: (bi, c, 0)),
            pl.BlockSpec((1, 1, SHIFT_WIDTH), lambda bi, c: (bi, 0, 0)),
            pl.BlockSpec((1, PAIRS, LANES, LANES), lambda bi, c: (bi, 0, 0, 0)),
            pl.BlockSpec((1, CHUNK, BRANCH_WIDTH), lambda bi, c: (bi, c, 1)),
            pl.BlockSpec((1, SHIFT_WIDTH), lambda bi, c: (0, 0)),
            pl.BlockSpec((8, BRANCH_WIDTH), lambda bi, c: (0, 0)),
            pl.BlockSpec((LANES, 2 * BRANCH_WIDTH), lambda bi, c: (0, 0)),
        ],
        out_specs=[
            pl.BlockSpec((1, CHUNK, BRANCH_WIDTH), lambda bi, c: (bi, c, 0)),
            pl.BlockSpec((1, PAIRS, LANES, LANES), lambda bi, c: (bi, 0, 0, 0)),
        ],
        out_shape=[
            jax.ShapeDtypeStruct((b, t, BRANCH_WIDTH), BF16),
            jax.ShapeDtypeStruct((b, PAIRS, LANES, LANES), F32),
        ],
        scratch_shapes=[pltpu.VMEM((1, SHIFT_WIDTH), F32)],
        compiler_params=_params("parallel", "arbitrary"),
        name="rwkv_branch",
    )(p_shift, shift_prev, s0_bd, gate, mu, vecs, w_lora)


def _state_to_blockdiag(s):
    b = s.shape[0]
    sp = s.reshape(b, PAIRS, 2, HEAD_DIM, HEAD_DIM)
    z = jnp.zeros_like(sp[:, :, 0])
    top = jnp.concatenate([sp[:, :, 0], z], axis=-1)
    bot = jnp.concatenate([z, sp[:, :, 1]], axis=-1)
    return jnp.concatenate([top, bot], axis=-2)


def _state_from_blockdiag(s_bd):
    b = s_bd.shape[0]
    h0 = s_bd[:, :, :HEAD_DIM, :HEAD_DIM]
    h1 = s_bd[:, :, HEAD_DIM:, HEAD_DIM:]
    return jnp.stack([h0, h1], axis=2).reshape(b, RWKV_HEADS, HEAD_DIM, HEAD_DIM)


def _attn_kernel(*refs, n_kv_parts, tq, n_keys, masked):
    q_ref = refs[0]
    k_refs = refs[1:1 + n_kv_parts]
    v_refs = refs[1 + n_kv_parts:1 + 2 * n_kv_parts]
    bias_ref, sink_ref, zg_ref, o_ref = refs[1 + 2 * n_kv_parts:]
    n = pl.program_id(1)
    pad = KEY_PAD - n_keys
    k_parts = [kr[0] for kr in k_refs]
    v_parts = [vr[0] for vr in v_refs]
    if pad:
        k_parts.append(jnp.zeros((pad, KV_WIDTH), F32))
        v_parts.append(jnp.zeros((pad, KV_WIDTH), F32))
    k_all = jnp.concatenate(k_parts, axis=0).astype(BF16)
    v_all = jnp.concatenate(v_parts, axis=0).astype(BF16)
    lane = lax.broadcasted_iota(jnp.int32, (tq, LANES), 1)
    head0 = lane < HEAD_DIM
    col = lax.broadcasted_iota(jnp.int32, (1, KEY_PAD), 1)
    valid = col < n_keys
    if masked:
        valid = valid & (col // CHUNK + n - BAND_CHUNKS >= 0)
    q = q_ref[0]
    zero = jnp.zeros((), q.dtype)
    for j in range(ATT_KV_HEADS // 2):
        slabs = [q[:, (j * ATT_GROUP + g) * LANES:(j * ATT_GROUP + g + 1) * LANES] for g in range(ATT_GROUP)]
        lhs = jnp.concatenate([jnp.where(head0, s, zero) for s in slabs]
                              + [jnp.where(head0, zero, s) for s in slabs], axis=0)
        ksl = k_all[:, j * LANES:(j + 1) * LANES]
        vsl = v_all[:, j * LANES:(j + 1) * LANES]
        logits = _dot_nt(lhs, ksl) + bias_ref[j]
        logits = jnp.where(valid, logits, NEG_INF)
        sink = sink_ref[j]
        m = jnp.maximum(jnp.max(logits, axis=-1, keepdims=True), sink)
        e = jnp.exp(logits - m)
        denom = jnp.sum(e, axis=-1, keepdims=True) + jnp.exp(sink - m)
        probs = (e / denom).astype(BF16)
        out = _dot(probs, vsl)
        for g in range(ATT_GROUP):
            lo = out[g * tq:(g + 1) * tq]
            hi = out[(ATT_GROUP + g) * tq:(ATT_GROUP + g + 1) * tq]
            sl = slice((j * ATT_GROUP + g) * LANES, (j * ATT_GROUP + g + 1) * LANES)
            o_ref[0, :, sl] = (jnp.where(head0, lo, hi) * zg_ref[0, :, sl].astype(F32)).astype(o_ref.dtype)


def _attn_prompt(q, k, v, bias, sink_rows, gate):
    b, t, _ = q.shape
    nc = t // CHUNK
    kv_specs = [pl.BlockSpec((1, CHUNK, KV_WIDTH), lambda bi, n, d=d: (bi, jnp.maximum(n - d, 0), 0))
                for d in range(BAND_CHUNKS, -1, -1)]
    return pl.pallas_call(
        functools.partial(_attn_kernel, n_kv_parts=BAND_CHUNKS + 1, tq=CHUNK,
                          n_keys=(BAND_CHUNKS + 1) * CHUNK, masked=True),
        grid=(b, nc),
        in_specs=[pl.BlockSpec((1, CHUNK, BRANCH_WIDTH), lambda bi, n: (bi, n, 0))]
        + kv_specs + kv_specs
        + [pl.BlockSpec(bias.shape, lambda bi, n: (0, 0, 0)),
           pl.BlockSpec(sink_rows.shape, lambda bi, n: (0, 0, 0)),
           pl.BlockSpec((1, CHUNK, BRANCH_WIDTH), lambda bi, n: (bi, n, 2))],
        out_specs=pl.BlockSpec((1, CHUNK, BRANCH_WIDTH), lambda bi, n: (bi, n, 0)),
        out_shape=jax.ShapeDtypeStruct((b, t, BRANCH_WIDTH), BF16),
        compiler_params=_params("parallel", "parallel"),
        name="attn_prompt",
    )(q, k, k, k, v, v, v, bias, sink_rows, gate)


def _attn_sample(q, k, v, k_cache, v_cache, bias, sink_rows, gate):
    b, t, _ = q.shape
    n_cache = k_cache.shape[1]
    new_spec = pl.BlockSpec((1, t, KV_WIDTH), lambda bi, n: (bi, 0, 0))
    cache_spec = pl.BlockSpec((1, n_cache, KV_WIDTH), lambda bi, n: (bi, 0, 0))
    return pl.pallas_call(
        functools.partial(_attn_kernel, n_kv_parts=2, tq=t, n_keys=n_cache + t, masked=False),
        grid=(b, 1),
        in_specs=[pl.BlockSpec((1, t, BRANCH_WIDTH), lambda bi, n: (bi, 0, 0)),
                  cache_spec, new_spec, cache_spec, new_spec,
                  pl.BlockSpec(bias.shape, lambda bi, n: (0, 0, 0)),
                  pl.BlockSpec(sink_rows.shape, lambda bi, n: (0, 0, 0)),
                  pl.BlockSpec((1, t, BRANCH_WIDTH), lambda bi, n: (bi, 0, 2))],
        out_specs=pl.BlockSpec((1, t, BRANCH_WIDTH), lambda bi, n: (bi, 0, 0)),
        out_shape=jax.ShapeDtypeStruct((b, t, BRANCH_WIDTH), BF16),
        compiler_params=_params("parallel", "parallel"),
        name="attn_sample",
    )(q, k_cache, k, v_cache, v, bias, sink_rows, gate)


def _t5_bucket(rel):
    half = N_BUCKETS // 2
    n = -rel
    ret = jnp.where(n < 0, half, 0)
    n = jnp.abs(n)
    max_exact = half // 2
    large = max_exact + (jnp.log(jnp.maximum(n, 1).astype(jnp.float32) / max_exact)
                         / math.log(MAX_DISTANCE / max_exact) * (half - max_exact)).astype(jnp.int32)
    large = jnp.minimum(large, half - 1)
    return ret + jnp.where(n < max_exact, n, large)


def _bias_kernel(tab_ref, onehot_ref, o_ref):
    o_ref[...] = _dot3_right_exact(tab_ref[...], onehot_ref[...])


def _rel_bias(table, key_pos, n_q):
    n_keys = key_pos.shape[0]
    bucket = _t5_bucket(key_pos[None, :] - jnp.arange(n_q)[:, None])
    bucket = jnp.pad(bucket, ((0, 0), (0, KEY_PAD - n_keys)), constant_values=-1)
    onehot = (bucket.reshape(1, -1) == jnp.arange(N_BUCKETS)[:, None]).astype(BF16)
    out = pl.pallas_call(
        _bias_kernel,
        out_shape=jax.ShapeDtypeStruct((ATT_HEADS, n_q * KEY_PAD), F32),
        name="rel_bias",
    )(table.T, onehot)
    return out.reshape(ATT_KV_HEADS // 2, 2 * ATT_GROUP * n_q, KEY_PAD)


def _merge_kernel(ga_ref, gb_ref, gc_ref, w_ref, m0_ref, m1_ref, m2_ref, o_ref):
    acc = m0_ref[...].astype(F32) * _dot(ga_ref[...], w_ref[0])
    acc = acc + m1_ref[...].astype(F32) * _dot(gb_ref[...], w_ref[1])
    acc = acc + m2_ref[...].astype(F32) * _dot(gc_ref[...], w_ref[2])
    o_ref[...] = acc.astype(o_ref.dtype)


def _merge(ga, gb, gc, w_branch, mg, tm, tn):
    m, kdim = ga.shape
    tm = min(tm, m)
    nj = D_MODEL // tn
    g_spec = pl.BlockSpec((tm, kdim), lambda i, j: (i, 0))
    m_specs = [pl.BlockSpec((tm, tn), lambda i, j, br=br: (i, br * nj + j)) for br in range(N_BRANCH)]
    return pl.pallas_call(
        _merge_kernel,
        grid=(m // tm, nj),
        in_specs=[g_spec, g_spec, g_spec, pl.BlockSpec((N_BRANCH, kdim, tn), lambda i, j: (0, 0, j))] + m_specs,
        out_specs=pl.BlockSpec((tm, tn), lambda i, j: (i, j)),
        out_shape=jax.ShapeDtypeStruct((m, D_MODEL), BF16),
        compiler_params=_params("parallel", "arbitrary"),
        name="merge",
    )(ga, gb, gc, w_branch, mg, mg, mg)


def _out_kernel(a_ref, w_ref, x_ref, g_ref, o_ref, *, final_norm):
    y = x_ref[...] + _dot(a_ref[...], w_ref[...])
    if final_norm:
        ms = jnp.mean(y * y, axis=-1, keepdims=True)
        y = y * lax.rsqrt(ms + RMS_EPS) * g_ref[...]
    o_ref[...] = y


def _out_proj(merged, w_out, x2d, final_g, final_norm, tm):
    m, d = x2d.shape
    tm = min(tm, m)
    return pl.pallas_call(
        functools.partial(_out_kernel, final_norm=final_norm),
        grid=(m // tm,),
        in_specs=[pl.BlockSpec((tm, d), lambda i: (i, 0)), pl.BlockSpec((d, d), lambda i: (0, 0)),
                  pl.BlockSpec((tm, d), lambda i: (i, 0)), pl.BlockSpec((1, d), lambda i: (0, 0))],
        out_specs=pl.BlockSpec((tm, d), lambda i: (i, 0)),
        out_shape=jax.ShapeDtypeStruct((m, d), F32),
        compiler_params=_params("parallel"),
        name="out_proj",
    )(merged, w_out, x2d, final_g.reshape(1, d))


def _head_perm():
    perm = np.zeros(BRANCH_WIDTH, np.int32)
    d = np.arange(HEAD_DIM)
    for j in range(ATT_KV_HEADS // 2):
        for g in range(ATT_GROUP):
            for half in range(2):
                h = ATT_GROUP * (2 * j + half) + g
                new0 = (j * ATT_GROUP + g) * LANES + half * HEAD_DIM
                perm[new0 + d] = h * HEAD_DIM + d
    return perm


def _layer_weights(l, w_in, pool_w, pool_scale, rwkv_mu, rwkv_w0, rwkv_w_up, rwkv_a0, rwkv_a_up, rwkv_k_k,
                   rwkv_k_a, rwkv_r_k, rwkv_gn_w, rwkv_gn_b, attn_sink, w_branch, w_out):
    perm = _head_perm()
    wi = w_in[l]
    gate_cols = np.concatenate([np.arange(2 * BRANCH_WIDTH), 2 * BRANCH_WIDTH + perm]) + OFF_GATE
    w_lora = jnp.zeros((LANES, 2 * BRANCH_WIDTH), F32)
    w_lora = w_lora.at[:LORA_RANK, :BRANCH_WIDTH].set(rwkv_w_up[l]).at[LORA_RANK:, BRANCH_WIDTH:].set(rwkv_a_up[l])
    vecs = jnp.stack([rwkv_w0[l], rwkv_a0[l], rwkv_k_k[l], rwkv_k_a[l], rwkv_r_k[l], rwkv_gn_w[l], rwkv_gn_b[l],
                      jnp.zeros_like(rwkv_w0[l])])
    wb = w_branch[l]
    return dict(
        w_pool=wi[:, OFF_POOL:OFF_SHIFT].astype(BF16),
        w_shift=wi[:, OFF_SHIFT:OFF_Q].astype(BF16),
        w_q=(wi[:, OFF_Q + perm] * (HEAD_DIM ** -0.5)).astype(BF16),
        w_kv=wi[:, OFF_K:OFF_GATE].astype(BF16),
        w_gate=wi[:, gate_cols].astype(BF16),
        w_merge=wi[:, OFF_MERGE:].astype(BF16),
        pool_w=pool_w[l].astype(BF16),
        pool_scale=pool_scale[l].reshape(1, -1),
        mu=rwkv_mu[l].reshape(1, -1),
        vecs=vecs,
        w_lora=w_lora.astype(BF16),
        sink=attn_sink[l],
        w_branch=jnp.stack([wb[0], wb[1], wb[2][perm]]).astype(BF16),
        w_out=w_out[l].astype(BF16),
    )


def _sink_rows(sink, n_q):
    return jnp.repeat(sink.reshape(ATT_KV_HEADS // 2, 2 * ATT_GROUP), n_q, axis=1)[..., None]


def _mixer_layer(x, lw, norm_g, final_g, final_norm, *, pos0, pool_hist16, shift_prev, s0_bd, bias, caches, tm):
    b, t, d = x.shape
    x2 = x.reshape(b * t, d)
    h = _rmsnorm(x2, norm_g, BF16, tm)
    mm = functools.partial(_matmul, h, tm=tm)
    u_pool = mm(lw["w_pool"], act=None, out_dtype=F32, tn=512, name="proj_pool").reshape(b, t, -1)
    p_shift = mm(lw["w_shift"], act=None, out_dtype=F32, tn=640, name="proj_shift").reshape(b, t, -1)
    q = mm(lw["w_q"], act=None, out_dtype=BF16, tn=512, name="proj_q").reshape(b, t, -1)
    kv = mm(lw["w_kv"], act=None, out_dtype=F32, tn=512, name="proj_kv").reshape(b, t, -1)
    gate = mm(lw["w_gate"], act="silu", out_dtype=BF16, tn=512, name="proj_gate").reshape(b, t, -1)
    mg = mm(lw["w_merge"], act="sigmoid", out_dtype=BF16, tn=512, name="proj_merge")
    k, v = kv[..., :KV_WIDTH], kv[..., KV_WIDTH:]

    ga = _pool_branch(u_pool, pool_hist16, gate, lw["pool_w"], lw["pool_scale"], pos0)

    t_pad = -(-t // CHUNK) * CHUNK
    if t_pad != t:
        p_in = jnp.pad(p_shift, ((0, 0), (0, t_pad - t), (0, 0)))
        gate_in = jnp.pad(gate, ((0, 0), (0, t_pad - t), (0, 0)))
    else:
        p_in, gate_in = p_shift, gate
    gb, s_bd = _rwkv_branch(p_in, shift_prev, s0_bd, gate_in, lw["mu"], lw["vecs"], lw["w_lora"],
                            n_valid=CHUNK - (t_pad - t))
    gb = gb[:, :t]

    if caches is None:
        gc = _attn_prompt(q, k, v, bias, _sink_rows(lw["sink"], CHUNK), gate)
    else:
        gc = _attn_sample(q, k, v, caches[0], caches[1], bias, _sink_rows(lw["sink"], t), gate)

    merged = _merge(ga.reshape(b * t, -1), gb.reshape(b * t, -1), gc.reshape(b * t, -1), lw["w_branch"], mg,
                    tm=tm, tn=512)
    y = _out_proj(merged, lw["w_out"], x2, final_g, final_norm, tm=min(tm, 512)).reshape(b, t, d)
    return y, k, v, _state_from_blockdiag(s_bd), p_shift[:, -1], u_pool[:, t - POOL_HIST:]


def kernel(x_prompt, x_sample, cache_attn_k, cache_attn_v, state_rwkv, state_rwkv_shift, state_pool, norm_g, w_in,
           pool_w, pool_scale, rwkv_mu, rwkv_w0, rwkv_w_up, rwkv_a0, rwkv_a_up, rwkv_k_k, rwkv_k_a, rwkv_r_k,
           rwkv_gn_w, rwkv_gn_b, attn_sink, rel_pos_table, w_branch, w_out, final_norm_g):
    depth = norm_g.shape[0]
    bp, tp, _ = x_prompt.shape
    bs, ts, _ = x_sample.shape
    n_cache = cache_attn_k.shape[2]

    key_p = jnp.arange((BAND_CHUNKS + 1) * CHUNK) - BAND_CHUNKS * CHUNK
    bias_p = _rel_bias(rel_pos_table, key_p, CHUNK)
    key_s = jnp.concatenate([jnp.arange(n_cache) - n_cache, jnp.arange(ts)])
    bias_s = _rel_bias(rel_pos_table, key_s, ts)

    xp, xs = x_prompt, x_sample
    outs_p, outs_s = [], []
    for l in range(depth):
        lw = _layer_weights(l, w_in, pool_w, pool_scale, rwkv_mu, rwkv_w0, rwkv_w_up, rwkv_a0, rwkv_a_up, rwkv_k_k,
                            rwkv_k_a, rwkv_r_k, rwkv_gn_w, rwkv_gn_b, attn_sink, w_branch, w_out)
        last = l == depth - 1
        xp, kp, vp, sp, shp, plp = _mixer_layer(
            xp, lw, norm_g[l], final_norm_g, last, pos0=0,
            pool_hist16=jnp.zeros((bp, 16, BRANCH_WIDTH), F32),
            shift_prev=jnp.zeros((bp, 1, SHIFT_WIDTH), F32),
            s0_bd=jnp.zeros((bp, PAIRS, LANES, LANES), F32),
            bias=bias_p, caches=None, tm=1024)
        outs_p.append((kp[:, -WINDOW:].reshape(bp, WINDOW, ATT_KV_HEADS, HEAD_DIM),
                       vp[:, -WINDOW:].reshape(bp, WINDOW, ATT_KV_HEADS, HEAD_DIM), sp, shp, plp))
        xs, k_s, v_s, s_s, sh_s, pl_s = _mixer_layer(
            xs, lw, norm_g[l], final_norm_g, last, pos0=PAST_LEN,
            pool_hist16=jnp.pad(state_pool[l], ((0, 0), (1, 0), (0, 0))),
            shift_prev=state_rwkv_shift[l][:, None, :],
            s0_bd=_state_to_blockdiag(state_rwkv[l]),
            bias=bias_s,
            caches=(cache_attn_k[l].reshape(bs, n_cache, KV_WIDTH), cache_attn_v[l].reshape(bs, n_cache, KV_WIDTH)),
            tm=bs * ts)
        outs_s.append((k_s.reshape(bs, ts, ATT_KV_HEADS, HEAD_DIM), v_s.reshape(bs, ts, ATT_KV_HEADS, HEAD_DIM),
                       s_s, sh_s, pl_s))

    st = lambda outs, i: jnp.stack([o[i] for o in outs])
    return (xp, xs, st(outs_p, 0), st(outs_p, 1), st(outs_s, 0), st(outs_s, 1),
            st(outs_p, 2), st(outs_s, 2), st(outs_p, 3), st(outs_s, 3), st(outs_p, 4), st(outs_s, 4))
```

```python
import functools
import math

import jax
import jax.numpy as jnp
import numpy as np
from jax import lax
from jax.experimental import pallas as pl
from jax.experimental.pallas import tpu as pltpu

F32 = jnp.float32
BF16 = jnp.bfloat16

D_MODEL = 2048
BRANCH_WIDTH = 1024
POOL_WINDOWS = (2, 4, 8, 16)
POOL_GROUP_DIM = BRANCH_WIDTH // len(POOL_WINDOWS)
POOL_HIST = max(POOL_WINDOWS) - 1
HEAD_DIM = 64
RWKV_HEADS = BRANCH_WIDTH // HEAD_DIM
LORA_RANK = 64
SHIFT_WIDTH = 3 * BRANCH_WIDTH + 2 * LORA_RANK
GN_EPS = 64e-5
ATT_HEADS = 16
ATT_KV_HEADS = 4
ATT_GROUP = ATT_HEADS // ATT_KV_HEADS
KV_WIDTH = ATT_KV_HEADS * HEAD_DIM
CHUNK = 64
WINDOW = 128
BAND_CHUNKS = WINDOW // CHUNK
N_BUCKETS = 32
MAX_DISTANCE = 128
NEG_INF = -1e30
RMS_EPS = 1e-6
PAST_LEN = 2048
N_BRANCH = 3
OFF_POOL = 0
OFF_SHIFT = OFF_POOL + BRANCH_WIDTH
OFF_Q = OFF_SHIFT + SHIFT_WIDTH
OFF_K = OFF_Q + BRANCH_WIDTH
OFF_V = OFF_K + KV_WIDTH
OFF_GATE = OFF_V + KV_WIDTH
OFF_MERGE = OFF_GATE + N_BRANCH * BRANCH_WIDTH

LANES = 128
PAIRS = BRANCH_WIDTH // LANES
RWKV_GROUP = 8
KEY_PAD = 256
VMEM_LIMIT = 56 * 1024 * 1024


def _params(*sem):
    return pltpu.CompilerParams(dimension_semantics=sem, vmem_limit_bytes=VMEM_LIMIT)


def _sigmoid(x):
    return 1.0 / (1.0 + jnp.exp(-x))


def _dot(a, b):
    return jnp.dot(a, b, preferred_element_type=F32)


def _dot_nt(a, b):
    return lax.dot_general(a, b, (((1,), (1,)), ((), ())), preferred_element_type=F32)


def _split3(x):
    hi = x.astype(BF16)
    r1 = x - hi.astype(F32)
    mid = r1.astype(BF16)
    lo = (r1 - mid.astype(F32)).astype(BF16)
    return hi, mid, lo


def _dot3_right_exact(x, b):
    hi, mid, lo = _split3(x)
    return _dot(hi, b) + _dot(mid, b) + _dot(lo, b)


def _dot3_left_exact(a, x):
    hi, mid, lo = _split3(x)
    return _dot(a, hi) + _dot(a, mid) + _dot(a, lo)


def _rmsnorm_kernel(x_ref, g_ref, o_ref):
    x = x_ref[...]
    ms = jnp.mean(x * x, axis=-1, keepdims=True)
    o_ref[...] = (x * lax.rsqrt(ms + RMS_EPS) * g_ref[...]).astype(o_ref.dtype)


def _rmsnorm(x2d, g, out_dtype, tm):
    m, d = x2d.shape
    tm = min(tm, m)
    return pl.pallas_call(
        _rmsnorm_kernel,
        grid=(m // tm,),
        in_specs=[pl.BlockSpec((tm, d), lambda i: (i, 0)), pl.BlockSpec((1, d), lambda i: (0, 0))],
        out_specs=pl.BlockSpec((tm, d), lambda i: (i, 0)),
        out_shape=jax.ShapeDtypeStruct((m, d), out_dtype),
        compiler_params=_params("parallel"),
        name="rmsnorm",
    )(x2d, g.reshape(1, d))


def _mm_kernel(a_ref, w_ref, o_ref, *, act):
    acc = _dot(a_ref[...], w_ref[...])
    if act == "silu":
        acc = acc * _sigmoid(acc)
    elif act == "sigmoid":
        acc = _sigmoid(acc)
    o_ref[...] = acc.astype(o_ref.dtype)


def _matmul(a, w, *, act, out_dtype, tm, tn, name):
    m, k = a.shape
    n = w.shape[1]
    tm = min(tm, m)
    return pl.pallas_call(
        functools.partial(_mm_kernel, act=act),
        grid=(m // tm, n // tn),
        in_specs=[pl.BlockSpec((tm, k), lambda i, j: (i, 0)), pl.BlockSpec((k, tn), lambda i, j: (0, j))],
        out_specs=pl.BlockSpec((tm, tn), lambda i, j: (i, j)),
        out_shape=jax.ShapeDtypeStruct((m, n), out_dtype),
        compiler_params=_params("parallel", "arbitrary"),
        name=name,
    )(a, w)


def _pool_kernel(u_ref, prev_ref, hist_ref, zg_ref, w_ref, scale_ref, o_ref, *, tt, pos0):
    i = pl.program_id(1)
    u = u_ref[0]
    prev = jnp.where(i == 0, hist_ref[0], prev_ref[0])
    pos = pos0 + i * tt + lax.broadcasted_iota(jnp.int32, (tt, POOL_GROUP_DIM), 0)
    for g, w in enumerate(POOL_WINDOWS):
        sl = slice(g * POOL_GROUP_DIM, (g + 1) * POOL_GROUP_DIM)
        e = jnp.concatenate([prev[:, sl], u[:, sl]], axis=0)
        step = 1
        while step < w:
            e = e + pltpu.roll(e, step, 0)
            step *= 2
        s = e[16:, :]
        cnt = jnp.minimum(w, pos + 1).astype(F32)
        pooled = s / cnt - u[:, sl]
        o = _dot(pooled.astype(BF16), w_ref[g]) * scale_ref[:, sl]
        o_ref[0, :, sl] = (o * zg_ref[0, :, sl].astype(F32)).astype(o_ref.dtype)


def _pool_branch(u, hist16, gate, pool_w, pool_scale, pos0):
    b, t, c = u.shape
    tt = min(t, 512)
    sub = tt // 16
    return pl.pallas_call(
        functools.partial(_pool_kernel, tt=tt, pos0=pos0),
        grid=(b, t // tt),
        in_specs=[
            pl.BlockSpec((1, tt, c), lambda bi, i: (bi, i, 0)),
            pl.BlockSpec((1, 16, c), lambda bi, i: (bi, jnp.maximum(i * sub - 1, 0), 0)),
            pl.BlockSpec((1, 16, c), lambda bi, i: (bi, 0, 0)),
            pl.BlockSpec((1, tt, c), lambda bi, i: (bi, i, 0)),
            pl.BlockSpec(pool_w.shape, lambda bi, i: (0, 0, 0)),
            pl.BlockSpec((1, c), lambda bi, i: (0, 0)),
        ],
        out_specs=pl.BlockSpec((1, tt, c), lambda bi, i: (bi, i, 0)),
        out_shape=jax.ShapeDtypeStruct((b, t, c), BF16),
        compiler_params=_params("parallel", "parallel"),
        name="pool_branch",
    )(u, u, hist16, gate, pool_w, pool_scale)


def _rwkv_kernel(p_ref, sprev_ref, s0_ref, zg_ref, mu_ref, vec_ref, wl_ref, o_ref, s_ref, carry_ref,
                 *, n_valid):
    c = pl.program_id(1)
    C = CHUNK

    @pl.when(c == 0)
    def _():
        s_ref[...] = s0_ref[...]
        carry_ref[...] = sprev_ref[0]

    p = p_ref[0]
    row_w = lax.broadcasted_iota(jnp.int32, (C, SHIFT_WIDTH), 0)
    row = lax.broadcasted_iota(jnp.int32, (C, BRANCH_WIDTH), 0)
    prev = jnp.where(row_w == 0, carry_ref[...], pltpu.roll(p, 1, 0))
    carry_ref[...] = p[C - 1:C, :]
    xs = p + (prev - p) * mu_ref[...]

    w0, a0, k_k, k_a, r_k, gn_w, gn_b = (vec_ref[i:i + 1, :] for i in range(7))
    lane = lax.broadcasted_iota(jnp.int32, (C, LANES), 1)
    head0 = lane < HEAD_DIM
    lin = xs[:, 3 * BRANCH_WIDTH:]
    lin = jnp.where(head0, jnp.tanh(lin), lin)
    lora = _dot(lin.astype(BF16), wl_ref[...])
    log_decay = -math.exp(-0.5) * _sigmoid(w0 + lora[:, :BRANCH_WIDTH])
    iclr = _sigmoid(a0 + lora[:, BRANCH_WIDTH:])
    r = xs[:, :BRANCH_WIDTH]
    k = xs[:, BRANCH_WIDTH:2 * BRANCH_WIDTH]
    v = xs[:, 2 * BRANCH_WIDTH:3 * BRANCH_WIDTH]
    kk_raw = k * k_k
    k2 = k * (1.0 + (iclr - 1.0) * k_a)
    if n_valid < C:
        ok = row < n_valid
        log_decay = jnp.where(ok, log_decay, 0.0)
        kk_raw = jnp.where(ok, kk_raw, 0.0)
        k2 = jnp.where(ok, k2, 0.0)
        r = jnp.where(ok, r, 0.0)
        v = jnp.where(ok, v, 0.0)

    ti = lax.broadcasted_iota(jnp.int32, (C, C), 0)
    si = lax.broadcasted_iota(jnp.int32, (C, C), 1)
    tri = (si <= ti).astype(BF16)
    cum = _dot3_left_exact(tri, log_decay)
    cum_x = cum - log_decay
    cum_c = cum[C - 1:C, :]

    l0 = lax.broadcasted_iota(jnp.int32, (LANES, LANES), 0)
    l1 = lax.broadcasted_iota(jnp.int32, (LANES, LANES), 1)
    same_head = (l0 >= HEAD_DIM) == (l1 >= HEAD_DIM)
    ones_head = same_head.astype(BF16)
    eye = (l0 == l1).astype(F32)
    ri = lax.broadcasted_iota(jnp.int32, (2 * C, 2 * C), 0)
    ci = lax.broadcasted_iota(jnp.int32, (2 * C, 2 * C), 1)
    same_blk = (ri >= C) == (ci >= C)
    strict = same_blk & ((ci & (C - 1)) < (ri & (C - 1)))
    incl = same_blk & ((ci & (C - 1)) <= (ri & (C - 1)))
    eye2 = (ri == ci).astype(F32)

    def stack(x):
        return jnp.concatenate([jnp.where(head0, x, 0.0), jnp.where(head0, 0.0, x)], axis=0)

    def each(f, *cols):
        return [f(*xs) for xs in zip(*cols)]

    def bf(xs):
        return [x.astype(BF16) for x in xs]

    def head_sums(xs):
        tot = _dot3_right_exact(jnp.concatenate(xs, axis=0), ones_head)
        return [tot[i * C:(i + 1) * C] for i in range(len(xs))]

    def unfold(x):
        xr = pltpu.roll(x, C, 1)
        return jnp.where(ci < C, x, xr), jnp.where(ci < C, xr, x)

    for g0 in range(0, PAIRS, RWKV_GROUP):
        prs = list(range(g0, g0 + RWKV_GROUP))
        sls = [slice(pr * LANES, (pr + 1) * LANES) for pr in prs]
        r_p = [r[:, sl] for sl in sls]
        k_p = [k2[:, sl] for sl in sls]
        v_p = [v[:, sl] for sl in sls]
        kkr = [kk_raw[:, sl] for sl in sls]
        ic = [iclr[:, sl] for sl in sls]
        cm = [cum[:, sl] for sl in sls]
        cx = [cum_x[:, sl] for sl in sls]
        cc = [cum_c[:, sl] for sl in sls]
        n2 = head_sums(each(lambda x: x * x, kkr))
        kk = each(lambda x, n: x / jnp.maximum(jnp.sqrt(n), 1e-12), kkr, n2)
        b_p = each(lambda x, i: x * i, kk, ic)
        e_neg = each(lambda c_: jnp.exp(-c_), cm)
        e_end = each(lambda c_, e_: jnp.exp(e_ - c_), cm, cc)
        w_c = each(jnp.exp, cc)
        at_s = each(lambda x, c_: stack(-x * jnp.exp(c_)), kk, cx)
        rt_s = each(lambda x, c_: stack(x * jnp.exp(c_)), r_p, cm)
        v_s = each(stack, v_p)
        bb_b = bf(each(lambda x, e_: stack(x * e_), b_p, e_end))
        kb_b = bf(each(lambda x, e_: stack(x * e_), k_p, e_end))
        at_b, rt_b, v_b = bf(at_s), bf(rt_s), bf(v_s)
        bk_b = bf(each(lambda b_, k_, e_: jnp.concatenate([b_ * e_, k_ * e_], axis=0), b_p, k_p, e_neg))
        ar_b = each(lambda a_, r_: jnp.concatenate([a_, r_], axis=0), at_b, rt_b)
        prod = each(_dot_nt, ar_b, bk_b)
        a_pair = each(lambda x: unfold(x[:2 * C]), prod)
        r_pair = each(lambda x: unfold(x[2 * C:]), prod)
        a_ab = [jnp.where(strict, x[0], 0.0) for x in a_pair]
        a_ak = [jnp.where(strict, x[1], 0.0) for x in a_pair]
        a_rb = [jnp.where(incl, x[0], 0.0) for x in r_pair]
        a_rk = [jnp.where(incl, x[1], 0.0) for x in r_pair]
        tinv = each(lambda x: eye2 + x, a_ab)
        pw_b = bf(a_ab)
        pw_b = bf(each(_dot, pw_b, pw_b))
        n_sq = int(math.log2(C)) - 1
        for step in range(n_sq):
            if step < n_sq - 1:
                both = each(lambda t_, p_: _dot(jnp.concatenate([t_.astype(BF16), p_], axis=0), p_), tinv, pw_b)
                tinv = each(lambda t_, x: t_ + x[:2 * C], tinv, both)
                pw_b = bf([x[2 * C:] for x in both])
            else:
                tinv = each(lambda t_, p_: t_ + _dot(t_.astype(BF16), p_), tinv, pw_b)
        t_b = bf(tinv)
        akv = each(_dot, bf(a_ak), v_b)
        au = each(lambda t_, a_, k_: _dot(t_, jnp.concatenate([a_, k_.astype(BF16)], axis=1)), t_b, at_b, akv)
        ry = each(_dot, bf(a_rb), bf(au))
        rh = each(lambda x, y_: x + y_[:, :LANES], rt_s, ry)
        y0 = each(lambda y_, k_, v_: y_[:, LANES:] + _dot(k_, v_), ry, bf(a_rk), v_b)
        mg = each(lambda x, b_: _dot(x.T.astype(BF16), b_), au, bb_b)
        m_lr = [x[:LANES] for x in mg]
        g_bd = each(lambda x, v_, k_: x[LANES:] + _dot(v_.T.astype(BF16), k_), mg, v_s, kb_b)
        s_bd = [s_ref[0, pr] for pr in prs]
        s_b = bf(s_bd)
        y = each(lambda x, s_, y_: _dot_nt((x[:C] + x[C:]).astype(BF16), s_) + (y_[:C] + y_[C:]), rh, s_b, y0)
        s_new = each(lambda s_, w_, sb_, m_, g_: s_ * w_ + _dot(sb_, m_.astype(BF16)) + g_,
                     s_bd, w_c, s_b, m_lr, g_bd)
        for pr, s_ in zip(prs, s_new):
            s_ref[0, pr] = s_
        yc = each(lambda x, m_: x - m_ * (1.0 / HEAD_DIM), y, head_sums(y))
        var = head_sums(each(lambda x: x * x, yc))
        rk = head_sums(each(lambda r_, k_, sl: r_ * k_ * r_k[:, sl], r_p, k_p, sls))
        for sl, yc_, var_, rk_, v_ in zip(sls, yc, var, rk, v_p):
            yn = yc_ * lax.rsqrt(var_ * (1.0 / HEAD_DIM) + GN_EPS) * gn_w[:, sl] + gn_b[:, sl]
            o_ref[0, :, sl] = ((yn + rk_ * v_) * zg_ref[0, :, sl].astype(F32)).astype(o_ref.dtype)


def _rwkv_branch(p_shift, shift_prev, s0_bd, gate, mu, vecs, w_lora, n_valid):
    b, t, _ = p_shift.shape
    return pl.pallas_call(
        functools.partial(_rwkv_kernel, n_valid=n_valid),
        grid=(b, t // CHUNK),
        in_specs=[
            pl.BlockSpec((1, CHUNK, SHIFT_WIDTH), lambda bi, c: (bi, c, 0)),
            pl.BlockSpec((1, 1, SHIFT_WIDTH), lambda bi, c: (bi, 0, 0)),
            pl.BlockSpec((1, PAIRS, LANES, LANES), lambda bi, c: (bi, 0, 0, 0)),
            pl.BlockSpec((1, CHUNK, BRANCH_WIDTH), lambda bi, c: (bi, c, 1)),
            pl.BlockSpec((1, SHIFT_WIDTH), lambda bi, c: (0, 0)),
            pl.BlockSpec((8, BRANCH_WIDTH), lambda bi, c: (0, 0)),
            pl.BlockSpec((LANES, 2 * BRANCH_WIDTH), lambda bi, c: (0, 0)),
        ],
        out_specs=[
            pl.BlockSpec((1, CHUNK, BRANCH_WIDTH), lambda bi, c: (bi, c, 0)),
            pl.BlockSpec((1, PAIRS, LANES, LANES), lambda bi, c: (bi, 0, 0, 0)),
        ],
        out_shape=[
            jax.ShapeDtypeStruct((b, t, BRANCH_WIDTH), BF16),
            jax.ShapeDtypeStruct((b, PAIRS, LANES, LANES), F32),
        ],
        scratch_shapes=[pltpu.VMEM((1, SHIFT_WIDTH), F32)],
        compiler_params=_params("parallel", "arbitrary"),
        name="rwkv_branch",
    )(p_shift, shift_prev, s0_bd, gate, mu, vecs, w_lora)


def _state_to_blockdiag(s):
    b = s.shape[0]
    sp = s.reshape(b, PAIRS, 2, HEAD_DIM, HEAD_DIM)
    z = jnp.zeros_like(sp[:, :, 0])
    top = jnp.concatenate([sp[:, :, 0], z], axis=-1)
    bot = jnp.concatenate([z, sp[:, :, 1]], axis=-1)
    return jnp.concatenate([top, bot], axis=-2)


def _state_from_blockdiag(s_bd):
    b = s_bd.shape[0]
    h0 = s_bd[:, :, :HEAD_DIM, :HEAD_DIM]
    h1 = s_bd[:, :, HEAD_DIM:, HEAD_DIM:]
    return jnp.stack([h0, h1], axis=2).reshape(b, RWKV_HEADS, HEAD_DIM, HEAD_DIM)


def _attn_kernel(*refs, n_kv_parts, tq, n_keys, masked):
    q_ref = refs[0]
    k_refs = refs[1:1 + n_kv_parts]
    v_refs = refs[1 + n_kv_parts:1 + 2 * n_kv_parts]
    bias_ref, sink_ref, zg_ref, o_ref = refs[1 + 2 * n_kv_parts:]
    n = pl.program_id(1)
    pad = KEY_PAD - n_keys
    k_parts = [kr[0] for kr in k_refs]
    v_parts = [vr[0] for vr in v_refs]
    if pad:
        k_parts.append(jnp.zeros((pad, KV_WIDTH), F32))
        v_parts.append(jnp.zeros((pad, KV_WIDTH), F32))
    k_all = jnp.concatenate(k_parts, axis=0).astype(BF16)
    v_all = jnp.concatenate(v_parts, axis=0).astype(BF16)
    lane = lax.broadcasted_iota(jnp.int32, (tq, LANES), 1)
    head0 = lane < HEAD_DIM
    col = lax.broadcasted_iota(jnp.int32, (1, KEY_PAD), 1)
    valid = col < n_keys
    if masked:
        valid = valid & (col // CHUNK + n - BAND_CHUNKS >= 0)
    q = q_ref[0]
    zero = jnp.zeros((), q.dtype)
    for j in range(ATT_KV_HEADS // 2):
        slabs = [q[:, (j * ATT_GROUP + g) * LANES:(j * ATT_GROUP + g + 1) * LANES] for g in range(ATT_GROUP)]
        lhs = jnp.concatenate([jnp.where(head0, s, zero) for s in slabs]
                              + [jnp.where(head0, zero, s) for s in slabs], axis=0)
        ksl = k_all[:, j * LANES:(j + 1) * LANES]
        vsl = v_all[:, j * LANES:(j + 1) * LANES]
        logits = _dot_nt(lhs, ksl) + bias_ref[j]
        logits = jnp.where(valid, logits, NEG_INF)
        sink = sink_ref[j]
        m = jnp.maximum(jnp.max(logits, axis=-1, keepdims=True), sink)
        e = jnp.exp(logits - m)
        denom = jnp.sum(e, axis=-1, keepdims=True) + jnp.exp(sink - m)
        probs = (e / denom).astype(BF16)
        out = _dot(probs, vsl)
        for g in range(ATT_GROUP):
            lo = out[g * tq:(g + 1) * tq]
            hi = out[(ATT_GROUP + g) * tq:(ATT_GROUP + g + 1) * tq]
            sl = slice((j * ATT_GROUP + g) * LANES, (j * ATT_GROUP + g + 1) * LANES)
            o_ref[0, :, sl] = (jnp.where(head0, lo, hi) * zg_ref[0, :, sl].astype(F32)).astype(o_ref.dtype)


def _attn_prompt(q, k, v, bias, sink_rows, gate):
    b, t, _ = q.shape
    nc = t // CHUNK
    kv_specs = [pl.BlockSpec((1, CHUNK, KV_WIDTH), lambda bi, n, d=d: (bi, jnp.maximum(n - d, 0), 0))
                for d in range(BAND_CHUNKS, -1, -1)]
    return pl.pallas_call(
        functools.partial(_attn_kernel, n_kv_parts=BAND_CHUNKS + 1, tq=CHUNK,
                          n_keys=(BAND_CHUNKS + 1) * CHUNK, masked=True),
        grid=(b, nc),
        in_specs=[pl.BlockSpec((1, CHUNK, BRANCH_WIDTH), lambda bi, n: (bi, n, 0))]
        + kv_specs + kv_specs
        + [pl.BlockSpec(bias.shape, lambda bi, n: (0, 0, 0)),
           pl.BlockSpec(sink_rows.shape, lambda bi, n: (0, 0, 0)),
           pl.BlockSpec((1, CHUNK, BRANCH_WIDTH), lambda bi, n: (bi, n, 2))],
        out_specs=pl.BlockSpec((1, CHUNK, BRANCH_WIDTH), lambda bi, n: (bi, n, 0)),
        out_shape=jax.ShapeDtypeStruct((b, t, BRANCH_WIDTH), BF16),
        compiler_params=_params("parallel", "parallel"),
        name="attn_prompt",
    )(q, k, k, k, v, v, v, bias, sink_rows, gate)


def _attn_sample(q, k, v, k_cache, v_cache, bias, sink_rows, gate):
    b, t, _ = q.shape
    n_cache = k_cache.shape[1]
    new_spec = pl.BlockSpec((1, t, KV_WIDTH), lambda bi, n: (bi, 0, 0))
    cache_spec = pl.BlockSpec((1, n_cache, KV_WIDTH), lambda bi, n: (bi, 0, 0))
    return pl.pallas_call(
        functools.partial(_attn_kernel, n_kv_parts=2, tq=t, n_keys=n_cache + t, masked=False),
        grid=(b, 1),
        in_specs=[pl.BlockSpec((1, t, BRANCH_WIDTH), lambda bi, n: (bi, 0, 0)),
                  cache_spec, new_spec, cache_spec, new_spec,
                  pl.BlockSpec(bias.shape, lambda bi, n: (0, 0, 0)),
                  pl.BlockSpec(sink_rows.shape, lambda bi, n: (0, 0, 0)),
                  pl.BlockSpec((1, t, BRANCH_WIDTH), lambda bi, n: (bi, 0, 2))],
        out_specs=pl.BlockSpec((1, t, BRANCH_WIDTH), lambda bi, n: (bi, 0, 0)),
        out_shape=jax.ShapeDtypeStruct((b, t, BRANCH_WIDTH), BF16),
        compiler_params=_params("parallel", "parallel"),
        name="attn_sample",
    )(q, k_cache, k, v_cache, v, bias, sink_rows, gate)


def _t5_bucket(rel):
    half = N_BUCKETS // 2
    n = -rel
    ret = jnp.where(n < 0, half, 0)
    n = jnp.abs(n)
    max_exact = half // 2
    large = max_exact + (jnp.log(jnp.maximum(n, 1).astype(jnp.float32) / max_exact)
                         / math.log(MAX_DISTANCE / max_exact) * (half - max_exact)).astype(jnp.int32)
    large = jnp.minimum(large, half - 1)
    return ret + jnp.where(n < max_exact, n, large)


def _bias_kernel(tab_ref, onehot_ref, o_ref):
    o_ref[...] = _dot3_right_exact(tab_ref[...], onehot_ref[...])


def _rel_bias(table, key_pos, n_q):
    n_keys = key_pos.shape[0]
    bucket = _t5_bucket(key_pos[None, :] - jnp.arange(n_q)[:, None])
    bucket = jnp.pad(bucket, ((0, 0), (0, KEY_PAD - n_keys)), constant_values=-1)
    onehot = (bucket.reshape(1, -1) == jnp.arange(N_BUCKETS)[:, None]).astype(BF16)
    out = pl.pallas_call(
        _bias_kernel,
        out_shape=jax.ShapeDtypeStruct((ATT_HEADS, n_q * KEY_PAD), F32),
        name="rel_bias",
    )(table.T, onehot)
    return out.reshape(ATT_KV_HEADS // 2, 2 * ATT_GROUP * n_q, KEY_PAD)


def _merge_kernel(ga_ref, gb_ref, gc_ref, w_ref, m0_ref, m1_ref, m2_ref, o_ref):
    acc = m0_ref[...].astype(F32) * _dot(ga_ref[...], w_ref[0])
    acc = acc + m1_ref[...].astype(F32) * _dot(gb_ref[...], w_ref[1])
    acc = acc + m2_ref[...].astype(F32) * _dot(gc_ref[...], w_ref[2])
    o_ref[...] = acc.astype(o_ref.dtype)


def _merge(ga, gb, gc, w_branch, mg, tm, tn):
    m, kdim = ga.shape
    tm = min(tm, m)
    nj = D_MODEL // tn
    g_spec = pl.BlockSpec((tm, kdim), lambda i, j: (i, 0))
    m_specs = [pl.BlockSpec((tm, tn), lambda i, j, br=br: (i, br * nj + j)) for br in range(N_BRANCH)]
    return pl.pallas_call(
        _merge_kernel,
        grid=(m // tm, nj),
        in_specs=[g_spec, g_spec, g_spec, pl.BlockSpec((N_BRANCH, kdim, tn), lambda i, j: (0, 0, j))] + m_specs,
        out_specs=pl.BlockSpec((tm, tn), lambda i, j: (i, j)),
        out_shape=jax.ShapeDtypeStruct((m, D_MODEL), BF16),
        compiler_params=_params("parallel", "arbitrary"),
        name="merge",
    )(ga, gb, gc, w_branch, mg, mg, mg)


def _out_kernel(a_ref, w_ref, x_ref, g_ref, o_ref, *, final_norm):
    y = x_ref[...] + _dot(a_ref[...], w_ref[...])
    if final_norm:
        ms = jnp.mean(y * y, axis=-1, keepdims=True)
        y = y * lax.rsqrt(ms + RMS_EPS) * g_ref[...]
    o_ref[...] = y


def _out_proj(merged, w_out, x2d, final_g, final_norm, tm):
    m, d = x2d.shape
    tm = min(tm, m)
    return pl.pallas_call(
        functools.partial(_out_kernel, final_norm=final_norm),
        grid=(m // tm,),
        in_specs=[pl.BlockSpec((tm, d), lambda i: (i, 0)), pl.BlockSpec((d, d), lambda i: (0, 0)),
                  pl.BlockSpec((tm, d), lambda i: (i, 0)), pl.BlockSpec((1, d), lambda i: (0, 0))],
        out_specs=pl.BlockSpec((tm, d), lambda i: (i, 0)),
        out_shape=jax.ShapeDtypeStruct((m, d), F32),
        compiler_params=_params("parallel"),
        name="out_proj",
    )(merged, w_out, x2d, final_g.reshape(1, d))


def _head_perm():
    perm = np.zeros(BRANCH_WIDTH, np.int32)
    d = np.arange(HEAD_DIM)
    for j in range(ATT_KV_HEADS // 2):
        for g in range(ATT_GROUP):
            for half in range(2):
                h = ATT_GROUP * (2 * j + half) + g
                new0 = (j * ATT_GROUP + g) * LANES + half * HEAD_DIM
                perm[new0 + d] = h * HEAD_DIM + d
    return perm


def _layer_weights(l, w_in, pool_w, pool_scale, rwkv_mu, rwkv_w0, rwkv_w_up, rwkv_a0, rwkv_a_up, rwkv_k_k,
                   rwkv_k_a, rwkv_r_k, rwkv_gn_w, rwkv_gn_b, attn_sink, w_branch, w_out):
    perm = _head_perm()
    wi = w_in[l]
    gate_cols = np.concatenate([np.arange(2 * BRANCH_WIDTH), 2 * BRANCH_WIDTH + perm]) + OFF_GATE
    w_lora = jnp.zeros((LANES, 2 * BRANCH_WIDTH), F32)
    w_lora = w_lora.at[:LORA_RANK, :BRANCH_WIDTH].set(rwkv_w_up[l]).at[LORA_RANK:, BRANCH_WIDTH:].set(rwkv_a_up[l])
    vecs = jnp.stack([rwkv_w0[l], rwkv_a0[l], rwkv_k_k[l], rwkv_k_a[l], rwkv_r_k[l], rwkv_gn_w[l], rwkv_gn_b[l],
                      jnp.zeros_like(rwkv_w0[l])])
    wb = w_branch[l]
    return dict(
        w_pool=wi[:, OFF_POOL:OFF_SHIFT].astype(BF16),
        w_shift=wi[:, OFF_SHIFT:OFF_Q].astype(BF16),
        w_q=(wi[:, OFF_Q + perm] * (HEAD_DIM ** -0.5)).astype(BF16),
        w_kv=wi[:, OFF_K:OFF_GATE].astype(BF16),
        w_gate=wi[:, gate_cols].astype(BF16),
        w_merge=wi[:, OFF_MERGE:].astype(BF16),
        pool_w=pool_w[l].astype(BF16),
        pool_scale=pool_scale[l].reshape(1, -1),
        mu=rwkv_mu[l].reshape(1, -1),
        vecs=vecs,
        w_lora=w_lora.astype(BF16),
        sink=attn_sink[l],
        w_branch=jnp.stack([wb[0], wb[1], wb[2][perm]]).astype(BF16),
        w_out=w_out[l].astype(BF16),
    )


def _sink_rows(sink, n_q):
    return jnp.repeat(sink.reshape(ATT_KV_HEADS // 2, 2 * ATT_GROUP), n_q, axis=1)[..., None]


def _mixer_layer(x, lw, norm_g, final_g, final_norm, *, pos0, pool_hist16, shift_prev, s0_bd, bias, caches, tm):
    b, t, d = x.shape
    x2 = x.reshape(b * t, d)
    h = _rmsnorm(x2, norm_g, BF16, tm)
    mm = functools.partial(_matmul, h, tm=tm)
    u_pool = mm(lw["w_pool"], act=None, out_dtype=F32, tn=512, name="proj_pool").reshape(b, t, -1)
    p_shift = mm(lw["w_shift"], act=None, out_dtype=F32, tn=640, name="proj_shift").reshape(b, t, -1)
    q = mm(lw["w_q"], act=None, out_dtype=BF16, tn=512, name="proj_q").reshape(b, t, -1)
    kv = mm(lw["w_kv"], act=None, out_dtype=F32, tn=512, name="proj_kv").reshape(b, t, -1)
    gate = mm(lw["w_gate"], act="silu", out_dtype=BF16, tn=512, name="proj_gate").reshape(b, t, -1)
    mg = mm(lw["w_merge"], act="sigmoid", out_dtype=BF16, tn=512, name="proj_merge")
    k, v = kv[..., :KV_WIDTH], kv[..., KV_WIDTH:]

    ga = _pool_branch(u_pool, pool_hist16, gate, lw["pool_w"], lw["pool_scale"], pos0)

    t_pad = -(-t // CHUNK) * CHUNK
    if t_pad != t:
        p_in = jnp.pad(p_shift, ((0, 0), (0, t_pad - t), (0, 0)))
        gate_in = jnp.pad(gate, ((0, 0), (0, t_pad - t), (0, 0)))
    else:
        p_in, gate_in = p_shift, gate
    gb, s_bd = _rwkv_branch(p_in, shift_prev, s0_bd, gate_in, lw["mu"], lw["vecs"], lw["w_lora"],
                            n_valid=CHUNK - (t_pad - t))
    gb = gb[:, :t]

    if caches is None:
        gc = _attn_prompt(q, k, v, bias, _sink_rows(lw["sink"], CHUNK), gate)
    else:
        gc = _attn_sample(q, k, v, caches[0], caches[1], bias, _sink_rows(lw["sink"], t), gate)

    merged = _merge(ga.reshape(b * t, -1), gb.reshape(b * t, -1), gc.reshape(b * t, -1), lw["w_branch"], mg,
                    tm=tm, tn=512)
    y = _out_proj(merged, lw["w_out"], x2, final_g, final_norm, tm=min(tm, 512)).reshape(b, t, d)
    return y, k, v, _state_from_blockdiag(s_bd), p_shift[:, -1], u_pool[:, t - POOL_HIST:]


def kernel(x_prompt, x_sample, cache_attn_k, cache_attn_v, state_rwkv, state_rwkv_shift, state_pool, norm_g, w_in,
           pool_w, pool_scale, rwkv_mu, rwkv_w0, rwkv_w_up, rwkv_a0, rwkv_a_up, rwkv_k_k, rwkv_k_a, rwkv_r_k,
           rwkv_gn_w, rwkv_gn_b, attn_sink, rel_pos_table, w_branch, w_out, final_norm_g):
    depth = norm_g.shape[0]
    bp, tp, _ = x_prompt.shape
    bs, ts, _ = x_sample.shape
    n_cache = cache_attn_k.shape[2]

    key_p = jnp.arange((BAND_CHUNKS + 1) * CHUNK) - BAND_CHUNKS * CHUNK
    bias_p = _rel_bias(rel_pos_table, key_p, CHUNK)
    key_s = jnp.concatenate([jnp.arange(n_cache) - n_cache, jnp.arange(ts)])
    bias_s = _rel_bias(rel_pos_table, key_s, ts)

    xp, xs = x_prompt, x_sample
    outs_p, outs_s = [], []
    for l in range(depth):
        lw = _layer_weights(l, w_in, pool_w, pool_scale, rwkv_mu, rwkv_w0, rwkv_w_up, rwkv_a0, rwkv_a_up, rwkv_k_k,
                            rwkv_k_a, rwkv_r_k, rwkv_gn_w, rwkv_gn_b, attn_sink, w_branch, w_out)
        last = l == depth - 1
        xp, kp, vp, sp, shp, plp = _mixer_layer(
            xp, lw, norm_g[l], final_norm_g, last, pos0=0,
            pool_hist16=jnp.zeros((bp, 16, BRANCH_WIDTH), F32),
            shift_prev=jnp.zeros((bp, 1, SHIFT_WIDTH), F32),
            s0_bd=jnp.zeros((bp, PAIRS, LANES, LANES), F32),
            bias=bias_p, caches=None, tm=1024)
        outs_p.append((kp[:, -WINDOW:].reshape(bp, WINDOW, ATT_KV_HEADS, HEAD_DIM),
                       vp[:, -WINDOW:].reshape(bp, WINDOW, ATT_KV_HEADS, HEAD_DIM), sp, shp, plp))
        xs, k_s, v_s, s_s, sh_s, pl_s = _mixer_layer(
            xs, lw, norm_g[l], final_norm_g, last, pos0=PAST_LEN,
            pool_hist16=jnp.pad(state_pool[l], ((0, 0), (1, 0), (0, 0))),
            shift_prev=state_rwkv_shift[l][:, None, :],
            s0_bd=_state_to_blockdiag(state_rwkv[l]),
            bias=bias_s,
            caches=(cache_attn_k[l].reshape(bs, n_cache, KV_WIDTH), cache_attn_v[l].reshape(bs, n_cache, KV_WIDTH)),
            tm=bs * ts)
        outs_s.append((k_s.reshape(bs, ts, ATT_KV_HEADS, HEAD_DIM), v_s.reshape(bs, ts, ATT_KV_HEADS, HEAD_DIM),
                       s_s, sh_s, pl_s))

    st = lambda outs, i: jnp.stack([o[i] for o in outs])
    return (xp, xs, st(outs_p, 0), st(outs_p, 1), st(outs_s, 0), st(outs_s, 1),
            st(outs_p, 2), st(outs_s, 2), st(outs_p, 3), st(outs_s, 3), st(outs_p, 4), st(outs_s, 4))
```

```python
import functools
import math

import jax
import jax.numpy as jnp
import numpy as np
from jax import lax
from jax.experimental import pallas as pl
from jax.experimental.pallas import tpu as pltpu

F32 = jnp.float32
BF16 = jnp.bfloat16

D_MODEL = 2048
BRANCH_WIDTH = 1024
POOL_WINDOWS = (2, 4, 8, 16)
POOL_GROUP_DIM = BRANCH_WIDTH // len(POOL_WINDOWS)
POOL_HIST = max(POOL_WINDOWS) - 1
HEAD_DIM = 64
RWKV_HEADS = BRANCH_WIDTH // HEAD_DIM
LORA_RANK = 64
SHIFT_WIDTH = 3 * BRANCH_WIDTH + 2 * LORA_RANK
GN_EPS = 64e-5
ATT_HEADS = 16
ATT_KV_HEADS = 4
ATT_GROUP = ATT_HEADS // ATT_KV_HEADS
KV_WIDTH = ATT_KV_HEADS * HEAD_DIM
CHUNK = 64
WINDOW = 128
BAND_CHUNKS = WINDOW // CHUNK
N_BUCKETS = 32
MAX_DISTANCE = 128
NEG_INF = -1e30
RMS_EPS = 1e-6
PAST_LEN = 2048
N_BRANCH = 3
OFF_POOL = 0
OFF_SHIFT = OFF_POOL + BRANCH_WIDTH
OFF_Q = OFF_SHIFT + SHIFT_WIDTH
OFF_K = OFF_Q + BRANCH_WIDTH
OFF_V = OFF_K + KV_WIDTH
OFF_GATE = OFF_V + KV_WIDTH
OFF_MERGE = OFF_GATE + N_BRANCH * BRANCH_WIDTH

LANES = 128
PAIRS = BRANCH_WIDTH // LANES
RWKV_GROUP = 8
KEY_PAD = 256
ATT_HEAD_ORDER = tuple(8 * m + o for m in range(ATT_KV_HEADS // 2) for o in (0, 2, 5, 7, 1, 3, 4, 6))
PROJ_TN = 512
W_POOL, W_Q, W_KV, W_GATE, W_MERGE, W_SHIFT = 0, 1024, 2048, 2560, 5632, 11776
SHIFT_PAD = 3584
W_TOTAL = W_SHIFT + SHIFT_PAD
VMEM_LIMIT = 56 * 1024 * 1024


def _params(*sem):
    return pltpu.CompilerParams(dimension_semantics=sem, vmem_limit_bytes=VMEM_LIMIT)


def _sigmoid(x):
    return 1.0 / (1.0 + jnp.exp(-x))


def _dot(a, b):
    return jnp.dot(a, b, preferred_element_type=F32)


def _dot_nt(a, b):
    return lax.dot_general(a, b, (((1,), (1,)), ((), ())), preferred_element_type=F32)


def _split3(x):
    hi = x.astype(BF16)
    r1 = x - hi.astype(F32)
    mid = r1.astype(BF16)
    lo = (r1 - mid.astype(F32)).astype(BF16)
    return hi, mid, lo


def _dot3_right_exact(x, b):
    hi, mid, lo = _split3(x)
    return _dot(hi, b) + _dot(mid, b) + _dot(lo, b)


def _dot3_left_exact(a, x):
    hi, mid, lo = _split3(x)
    return _dot(a, hi) + _dot(a, mid) + _dot(a, lo)


def _rmsnorm_kernel(x_ref, g_ref, o_ref):
    x = x_ref[...]
    ms = jnp.mean(x * x, axis=-1, keepdims=True)
    o_ref[...] = (x * lax.rsqrt(ms + RMS_EPS) * g_ref[...]).astype(o_ref.dtype)


def _rmsnorm(x2d, g, out_dtype, tm):
    m, d = x2d.shape
    tm = min(tm, m)
    return pl.pallas_call(
        _rmsnorm_kernel,
        grid=(m // tm,),
        in_specs=[pl.BlockSpec((tm, d), lambda i: (i, 0)), pl.BlockSpec((1, d), lambda i: (0, 0))],
        out_specs=pl.BlockSpec((tm, d), lambda i: (i, 0)),
        out_shape=jax.ShapeDtypeStruct((m, d), out_dtype),
        compiler_params=_params("parallel"),
        name="rmsnorm",
    )(x2d, g.reshape(1, d))


def _mm_kernel(a_ref, w_ref, o_ref, *, act):
    acc = _dot(a_ref[...], w_ref[...])
    if act == "silu":
        acc = acc * _sigmoid(acc)
    elif act == "sigmoid":
        acc = _sigmoid(acc)
    o_ref[...] = acc.astype(o_ref.dtype)


def _matmul(a, w_all, layer, col0, n, *, act, out_dtype, tm, name):
    m, k = a.shape
    tm = min(tm, m)
    tn = PROJ_TN
    j0 = col0 // tn
    return pl.pallas_call(
        functools.partial(_mm_kernel, act=act),
        grid=(m // tm, n // tn),
        in_specs=[pl.BlockSpec((tm, k), lambda i, j: (i, 0)),
                  pl.BlockSpec((None, k, tn), lambda i, j: (layer, 0, j0 + j))],
        out_specs=pl.BlockSpec((tm, tn), lambda i, j: (i, j)),
        out_shape=jax.ShapeDtypeStruct((m, n), out_dtype),
        compiler_params=_params("parallel", "arbitrary"),
        name=name,
    )(a, w_all)


def _pool_kernel(u_ref, prev_ref, hist_ref, zg_ref, w_ref, scale_ref, o_ref, *, tt, pos0):
    i = pl.program_id(1)
    u = u_ref[0]
    prev = jnp.where(i == 0, hist_ref[0], prev_ref[0])
    pos = pos0 + i * tt + lax.broadcasted_iota(jnp.int32, (tt, POOL_GROUP_DIM), 0)
    for g, w in enumerate(POOL_WINDOWS):
        sl = slice(g * POOL_GROUP_DIM, (g + 1) * POOL_GROUP_DIM)
        e = jnp.concatenate([prev[:, sl], u[:, sl]], axis=0)
        step = 1
        while step < w:
            e = e + pltpu.roll(e, step, 0)
            step *= 2
        s = e[16:, :]
        cnt = jnp.minimum(w, pos + 1).astype(F32)
        pooled = s / cnt - u[:, sl]
        o = _dot(pooled.astype(BF16), w_ref[g]) * scale_ref[:, sl]
        o_ref[0, :, sl] = (o * zg_ref[0, :, sl].astype(F32)).astype(o_ref.dtype)


def _pool_branch(u, hist16, gate, pool_w, layer, pool_scale, pos0):
    b, t, c = u.shape
    tt = min(t, 512)
    sub = tt // 16
    return pl.pallas_call(
        functools.partial(_pool_kernel, tt=tt, pos0=pos0),
        grid=(b, t // tt),
        in_specs=[
            pl.BlockSpec((1, tt, c), lambda bi, i: (bi, i, 0)),
            pl.BlockSpec((1, 16, c), lambda bi, i: (bi, jnp.maximum(i * sub - 1, 0), 0)),
            pl.BlockSpec((1, 16, c), lambda bi, i: (bi, 0, 0)),
            pl.BlockSpec((1, tt, c), lambda bi, i: (bi, i, 0)),
            pl.BlockSpec((None,) + pool_w.shape[1:], lambda bi, i: (layer, 0, 0, 0)),
            pl.BlockSpec((1, c), lambda bi, i: (0, 0)),
        ],
        out_specs=pl.BlockSpec((1, tt, c), lambda bi, i: (bi, i, 0)),
        out_shape=jax.ShapeDtypeStruct((b, t, c), BF16),
        compiler_params=_params("parallel", "parallel"),
        name="pool_branch",
    )(u, u, hist16, gate, pool_w, pool_scale)


def _rwkv_kernel(p_ref, sprev_ref, s0_ref, zg_ref, mu_ref, vec_ref, wl_ref, o_ref, s_ref, carry_ref,
                 *, n_valid):
    c = pl.program_id(1)
    C = CHUNK

    @pl.when(c == 0)
    def _():
        s_ref[...] = s0_ref[...]
        carry_ref[...] = sprev_ref[0]

    p = p_ref[0]
    row_w = lax.broadcasted_iota(jnp.int32, (C, SHIFT_WIDTH), 0)
    row = lax.broadcasted_iota(jnp.int32, (C, BRANCH_WIDTH), 0)
    prev = jnp.where(row_w == 0, carry_ref[...], pltpu.roll(p, 1, 0))
    carry_ref[...] = p[C - 1:C, :]
    xs = p + (prev - p) * mu_ref[...]

    w0, a0, k_k, k_a, r_k, gn_w, gn_b = (vec_ref[i:i + 1, :] for i in range(7))
    lane = lax.broadcasted_iota(jnp.int32, (C, LANES), 1)
    head0 = lane < HEAD_DIM
    lin = xs[:, 3 * BRANCH_WIDTH:]
    lin = jnp.where(head0, jnp.tanh(lin), lin)
    lora = _dot(lin.astype(BF16), wl_ref[...])
    log_decay = -math.exp(-0.5) * _sigmoid(w0 + lora[:, :BRANCH_WIDTH])
    iclr = _sigmoid(a0 + lora[:, BRANCH_WIDTH:])
    r = xs[:, :BRANCH_WIDTH]
    k = xs[:, BRANCH_WIDTH:2 * BRANCH_WIDTH]
    v = xs[:, 2 * BRANCH_WIDTH:3 * BRANCH_WIDTH]
    kk_raw = k * k_k
    k2 = k * (1.0 + (iclr - 1.0) * k_a)
    if n_valid < C:
        ok = row < n_valid
        log_decay = jnp.where(ok, log_decay, 0.0)
        kk_raw = jnp.where(ok, kk_raw, 0.0)
        k2 = jnp.where(ok, k2, 0.0)
        r = jnp.where(ok, r, 0.0)
        v = jnp.where(ok, v, 0.0)

    ti = lax.broadcasted_iota(jnp.int32, (C, C), 0)
    si = lax.broadcasted_iota(jnp.int32, (C, C), 1)
    tri = (si <= ti).astype(BF16)
    cum = _dot3_left_exact(tri, log_decay)
    cum_x = cum - log_decay
    cum_c = cum[C - 1:C, :]

    l0 = lax.broadcasted_iota(jnp.int32, (LANES, LANES), 0)
    l1 = lax.broadcasted_iota(jnp.int32, (LANES, LANES), 1)
    same_head = (l0 >= HEAD_DIM) == (l1 >= HEAD_DIM)
    ones_head = same_head.astype(BF16)
    eye = (l0 == l1).astype(F32)
    ri = lax.broadcasted_iota(jnp.int32, (2 * C, 2 * C), 0)
    ci = lax.broadcasted_iota(jnp.int32, (2 * C, 2 * C), 1)
    same_blk = (ri >= C) == (ci >= C)
    strict = same_blk & ((ci & (C - 1)) < (ri & (C - 1)))
    incl = same_blk & ((ci & (C - 1)) <= (ri & (C - 1)))
    eye2 = (ri == ci).astype(F32)

    def stack(x):
        return jnp.concatenate([jnp.where(head0, x, 0.0), jnp.where(head0, 0.0, x)], axis=0)

    def each(f, *cols):
        return [f(*xs) for xs in zip(*cols)]

    def bf(xs):
        return [x.astype(BF16) for x in xs]

    def head_sums(xs):
        tot = _dot3_right_exact(jnp.concatenate(xs, axis=0), ones_head)
        return [tot[i * C:(i + 1) * C] for i in range(len(xs))]

    def unfold(x):
        xr = pltpu.roll(x, C, 1)
        return jnp.where(ci < C, x, xr), jnp.where(ci < C, xr, x)

    for g0 in range(0, PAIRS, RWKV_GROUP):
        prs = list(range(g0, g0 + RWKV_GROUP))
        sls = [slice(pr * LANES, (pr + 1) * LANES) for pr in prs]
        r_p = [r[:, sl] for sl in sls]
        k_p = [k2[:, sl] for sl in sls]
        v_p = [v[:, sl] for sl in sls]
        kkr = [kk_raw[:, sl] for sl in sls]
        ic = [iclr[:, sl] for sl in sls]
        cm = [cum[:, sl] for sl in sls]
        cx = [cum_x[:, sl] for sl in sls]
        cc = [cum_c[:, sl] for sl in sls]
        n2 = head_sums(each(lambda x: x * x, kkr))
        kk = each(lambda x, n: x / jnp.maximum(jnp.sqrt(n), 1e-12), kkr, n2)
        b_p = each(lambda x, i: x * i, kk, ic)
        e_neg = each(lambda c_: jnp.exp(-c_), cm)
        e_end = each(lambda c_, e_: jnp.exp(e_ - c_), cm, cc)
        w_c = each(jnp.exp, cc)
        at_s = each(lambda x, c_: stack(-x * jnp.exp(c_)), kk, cx)
        rt_s = each(lambda x, c_: stack(x * jnp.exp(c_)), r_p, cm)
        v_s = each(stack, v_p)
        bb_b = bf(each(lambda x, e_: stack(x * e_), b_p, e_end))
        kb_b = bf(each(lambda x, e_: stack(x * e_), k_p, e_end))
        at_b, rt_b, v_b = bf(at_s), bf(rt_s), bf(v_s)
        bk_b = bf(each(lambda b_, k_, e_: jnp.concatenate([b_ * e_, k_ * e_], axis=0), b_p, k_p, e_neg))
        ar_b = each(lambda a_, r_: jnp.concatenate([a_, r_], axis=0), at_b, rt_b)
        prod = each(_dot_nt, ar_b, bk_b)
        a_pair = each(lambda x: unfold(x[:2 * C]), prod)
        r_pair = each(lambda x: unfold(x[2 * C:]), prod)
        a_ab = [jnp.where(strict, x[0], 0.0) for x in a_pair]
        a_ak = [jnp.where(strict, x[1], 0.0) for x in a_pair]
        a_rb = [jnp.where(incl, x[0], 0.0) for x in r_pair]
        a_rk = [jnp.where(incl, x[1], 0.0) for x in r_pair]
        tinv = each(lambda x: eye2 + x, a_ab)
        pw_b = bf(a_ab)
        pw_b = bf(each(_dot, pw_b, pw_b))
        n_sq = int(math.log2(C)) - 1
        for step in range(n_sq):
            if step < n_sq - 1:
                both = each(lambda t_, p_: _dot(jnp.concatenate([t_.astype(BF16), p_], axis=0), p_), tinv, pw_b)
                tinv = each(lambda t_, x: t_ + x[:2 * C], tinv, both)
                pw_b = bf([x[2 * C:] for x in both])
            else:
                tinv = each(lambda t_, p_: t_ + _dot(t_.astype(BF16), p_), tinv, pw_b)
        t_b = bf(tinv)
        akv = each(_dot, bf(a_ak), v_b)
        au = each(lambda t_, a_, k_: _dot(t_, jnp.concatenate([a_, k_.astype(BF16)], axis=1)), t_b, at_b, akv)
        ry = each(_dot, bf(a_rb), bf(au))
        rh = each(lambda x, y_: x + y_[:, :LANES], rt_s, ry)
        y0 = each(lambda y_, k_, v_: y_[:, LANES:] + _dot(k_, v_), ry, bf(a_rk), v_b)
        mg = each(lambda x, b_: _dot(x.T.astype(BF16), b_), au, bb_b)
        m_lr = [x[:LANES] for x in mg]
        g_bd = each(lambda x, v_, k_: x[LANES:] + _dot(v_.T.astype(BF16), k_), mg, v_s, kb_b)
        s_bd = [s_ref[0, pr] for pr in prs]
        s_b = bf(s_bd)
        y = each(lambda x, s_, y_: _dot_nt((x[:C] + x[C:]).astype(BF16), s_) + (y_[:C] + y_[C:]), rh, s_b, y0)
        s_new = each(lambda s_, w_, sb_, m_, g_: s_ * w_ + _dot(sb_, m_.astype(BF16)) + g_,
                     s_bd, w_c, s_b, m_lr, g_bd)
        for pr, s_ in zip(prs, s_new):
            s_ref[0, pr] = s_
        yc = each(lambda x, m_: x - m_ * (1.0 / HEAD_DIM), y, head_sums(y))
        var = head_sums(each(lambda x: x * x, yc))
        rk = head_sums(each(lambda r_, k_, sl: r_ * k_ * r_k[:, sl], r_p, k_p, sls))
        for sl, yc_, var_, rk_, v_ in zip(sls, yc, var, rk, v_p):
            yn = yc_ * lax.rsqrt(var_ * (1.0 / HEAD_DIM) + GN_EPS) * gn_w[:, sl] + gn_b[:, sl]
            o_ref[0, :, sl] = ((yn + rk_ * v_) * zg_ref[0, :, sl].astype(F32)).astype(o_ref.dtype)


def _rwkv_branch(p_shift, shift_prev, s0_bd, gate, mu, vecs, w_lora, n_valid):
    b, t, _ = p_shift.shape
    return pl.pallas_call(
        functools.partial(_rwkv_kernel, n_valid=n_valid),
        grid=(b, t // CHUNK),
        in_specs=[
            pl.BlockSpec((1, CHUNK, SHIFT_WIDTH), lambda bi, c: (bi, c, 0)),
            pl.BlockSpec((1, 1, SHIFT_WIDTH), lambda bi, c: (bi, 0, 0)),
            pl.BlockSpec((1, PAIRS, LANES, LANES), lambda bi, c: (bi, 0, 0, 0)),
            pl.BlockSpec((1, CHUNK, BRANCH_WIDTH), lambda bi, c: (bi, c, 1)),
            pl.BlockSpec((1, SHIFT_WIDTH), lambda bi, c: (0, 0)),
            pl.BlockSpec((8, BRANCH_WIDTH), lambda bi, c: (0, 0)),
            pl.BlockSpec((LANES, 2 * BRANCH_WIDTH), lambda bi, c: (0, 0)),
        ],
        out_specs=[
            pl.BlockSpec((1, CHUNK, BRANCH_WIDTH), lambda bi, c: (bi, c, 0)),
            pl.BlockSpec((1, PAIRS, LANES, LANES), lambda bi, c: (bi, 0, 0, 0)),
        ],
        out_shape=[
            jax.ShapeDtypeStruct((b, t, BRANCH_WIDTH), BF16),
            jax.ShapeDtypeStruct((b, PAIRS, LANES, LANES), F32),
        ],
        scratch_shapes=[pltpu.VMEM((1, SHIFT_WIDTH), F32)],
        compiler_params=_params("parallel", "arbitrary"),
        name="rwkv_branch",
    )(p_shift, shift_prev, s0_bd, gate, mu, vecs, w_lora)


def _state_to_blockdiag(s):
    b = s.shape[0]
    sp = s.reshape(b, PAIRS, 2, HEAD_DIM, HEAD_DIM)
    z = jnp.zeros_like(sp[:, :, 0])
    top = jnp.concatenate([sp[:, :, 0], z], axis=-1)
    bot = jnp.concatenate([z, sp[:, :, 1]], axis=-1)
    return jnp.concatenate([top, bot], axis=-2)


def _state_from_blockdiag(s_bd):
    b = s_bd.shape[0]
    h0 = s_bd[:, :, :HEAD_DIM, :HEAD_DIM]
    h1 = s_bd[:, :, HEAD_DIM:, HEAD_DIM:]
    return jnp.stack([h0, h1], axis=2).reshape(b, RWKV_HEADS, HEAD_DIM, HEAD_DIM)


def _attn_kernel(*refs, n_kv_parts, tq, n_keys, masked):
    q_ref = refs[0]
    k_refs = refs[1:1 + n_kv_parts]
    v_refs = refs[1 + n_kv_parts:1 + 2 * n_kv_parts]
    bias_ref, sink_ref, zg_ref, o_ref = refs[1 + 2 * n_kv_parts:]
    n = pl.program_id(1)
    pad = KEY_PAD - n_keys
    k_parts = [kr[0] for kr in k_refs]
    v_parts = [vr[0] for vr in v_refs]
    if pad:
        k_parts.append(jnp.zeros((pad, KV_WIDTH), F32))
        v_parts.append(jnp.zeros((pad, KV_WIDTH), F32))
    k_all = jnp.concatenate(k_parts, axis=0)
    v_all = jnp.concatenate(v_parts, axis=0)
    lane = lax.broadcasted_iota(jnp.int32, (tq, LANES), 1)
    head0 = lane < HEAD_DIM
    col = lax.broadcasted_iota(jnp.int32, (1, KEY_PAD), 1)
    valid = col < n_keys
    if masked:
        valid = valid & (col // CHUNK + n - BAND_CHUNKS >= 0)
    q = q_ref[0]
    zero = jnp.zeros((), q.dtype)
    lo_half = lambda s: jnp.where(head0, s, zero)
    hi_half = lambda s: jnp.where(head0, zero, s)
    lhs, keys, vals = [], [], []
    for m in range(ATT_KV_HEADS // 2):
        slabs = [q[:, (4 * m + i) * LANES:(4 * m + i + 1) * LANES] for i in range(4)]
        k_sl = k_all[:, m * LANES:(m + 1) * LANES]
        v_sl = v_all[:, m * LANES:(m + 1) * LANES]
        lhs.append(jnp.concatenate([lo_half(slabs[0]), lo_half(slabs[1]), hi_half(slabs[2]), hi_half(slabs[3])], 0))
        lhs.append(jnp.concatenate([hi_half(slabs[0]), hi_half(slabs[1]), lo_half(slabs[2]), lo_half(slabs[3])], 0))
        keys += [k_sl.astype(BF16), pltpu.roll(k_sl, HEAD_DIM, 1).astype(BF16)]
        vals += [v_sl.astype(BF16), pltpu.roll(v_sl, HEAD_DIM, 1).astype(BF16)]
    n_grp = len(lhs)
    logits = [jnp.where(valid, _dot_nt(lhs[i], keys[i]) + bias_ref[i], NEG_INF) for i in range(n_grp)]
    sink = [sink_ref[i] for i in range(n_grp)]
    mx = [jnp.maximum(jnp.max(logits[i], axis=-1, keepdims=True), sink[i]) for i in range(n_grp)]
    e = [jnp.exp(logits[i] - mx[i]) for i in range(n_grp)]
    denom = [jnp.sum(e[i], axis=-1, keepdims=True) + jnp.exp(sink[i] - mx[i]) for i in range(n_grp)]
    out = [_dot((e[i] / denom[i]).astype(BF16), vals[i]) for i in range(n_grp)]
    for m in range(ATT_KV_HEADS // 2):
        same, swap = out[2 * m], out[2 * m + 1]
        for i in range(4):
            a, b = same[i * tq:(i + 1) * tq], swap[i * tq:(i + 1) * tq]
            res = jnp.where(head0, a, b) if i < 2 else jnp.where(head0, b, a)
            sl = slice((4 * m + i) * LANES, (4 * m + i + 1) * LANES)
            o_ref[0, :, sl] = (res * zg_ref[0, :, sl].astype(F32)).astype(o_ref.dtype)


def _attn_prompt(q, kv, bias, sink_rows, gate):
    b, t, _ = q.shape
    nc = t // CHUNK
    k_specs, v_specs = ([pl.BlockSpec((1, CHUNK, KV_WIDTH), lambda bi, n, d=d, c=c: (bi, jnp.maximum(n - d, 0), c))
                         for d in range(BAND_CHUNKS, -1, -1)] for c in range(2))
    return pl.pallas_call(
        functools.partial(_attn_kernel, n_kv_parts=BAND_CHUNKS + 1, tq=CHUNK,
                          n_keys=(BAND_CHUNKS + 1) * CHUNK, masked=True),
        grid=(b, nc),
        in_specs=[pl.BlockSpec((1, CHUNK, BRANCH_WIDTH), lambda bi, n: (bi, n, 0))]
        + k_specs + v_specs
        + [pl.BlockSpec(bias.shape, lambda bi, n: (0, 0, 0)),
           pl.BlockSpec(sink_rows.shape, lambda bi, n: (0, 0, 0)),
           pl.BlockSpec((1, CHUNK, BRANCH_WIDTH), lambda bi, n: (bi, n, 2))],
        out_specs=pl.BlockSpec((1, CHUNK, BRANCH_WIDTH), lambda bi, n: (bi, n, 0)),
        out_shape=jax.ShapeDtypeStruct((b, t, BRANCH_WIDTH), BF16),
        compiler_params=_params("parallel", "parallel"),
        name="attn_prompt",
    )(q, kv, kv, kv, kv, kv, kv, bias, sink_rows, gate)


def _attn_sample(q, kv, k_cache, v_cache, bias, sink_rows, gate):
    b, t, _ = q.shape
    n_cache = k_cache.shape[1]
    cache_spec = pl.BlockSpec((1, n_cache, KV_WIDTH), lambda bi, n: (bi, 0, 0))
    return pl.pallas_call(
        functools.partial(_attn_kernel, n_kv_parts=2, tq=t, n_keys=n_cache + t, masked=False),
        grid=(b, 1),
        in_specs=[pl.BlockSpec((1, t, BRANCH_WIDTH), lambda bi, n: (bi, 0, 0)),
                  cache_spec, pl.BlockSpec((1, t, KV_WIDTH), lambda bi, n: (bi, 0, 0)),
                  cache_spec, pl.BlockSpec((1, t, KV_WIDTH), lambda bi, n: (bi, 0, 1)),
                  pl.BlockSpec(bias.shape, lambda bi, n: (0, 0, 0)),
                  pl.BlockSpec(sink_rows.shape, lambda bi, n: (0, 0, 0)),
                  pl.BlockSpec((1, t, BRANCH_WIDTH), lambda bi, n: (bi, 0, 2))],
        out_specs=pl.BlockSpec((1, t, BRANCH_WIDTH), lambda bi, n: (bi, 0, 0)),
        out_shape=jax.ShapeDtypeStruct((b, t, BRANCH_WIDTH), BF16),
        compiler_params=_params("parallel", "parallel"),
        name="attn_sample",
    )(q, k_cache, kv, v_cache, kv, bias, sink_rows, gate)


def _t5_bucket(rel):
    half = N_BUCKETS // 2
    n = -rel
    ret = jnp.where(n < 0, half, 0)
    n = jnp.abs(n)
    max_exact = half // 2
    large = max_exact + (jnp.log(jnp.maximum(n, 1).astype(jnp.float32) / max_exact)
                         / math.log(MAX_DISTANCE / max_exact) * (half - max_exact)).astype(jnp.int32)
    large = jnp.minimum(large, half - 1)
    return ret + jnp.where(n < max_exact, n, large)


def _bias_kernel(tab_ref, onehot_ref, o_ref):
    o_ref[...] = _dot3_right_exact(tab_ref[...], onehot_ref[...])


def _rel_bias(table, key_pos, n_q):
    n_keys = key_pos.shape[0]
    bucket = _t5_bucket(key_pos[None, :] - jnp.arange(n_q)[:, None])
    bucket = jnp.pad(bucket, ((0, 0), (0, KEY_PAD - n_keys)), constant_values=-1)
    onehot = (bucket.reshape(1, -1) == jnp.arange(N_BUCKETS)[:, None]).astype(BF16)
    out = pl.pallas_call(
        _bias_kernel,
        out_shape=jax.ShapeDtypeStruct((ATT_HEADS, n_q * KEY_PAD), F32),
        name="rel_bias",
    )(table.T[np.array(ATT_HEAD_ORDER)], onehot)
    return out.reshape(len(ATT_HEAD_ORDER) // 4, 4 * n_q, KEY_PAD)


def _merge_kernel(ga_ref, gb_ref, gc_ref, w_ref, m0_ref, m1_ref, m2_ref, o_ref):
    acc = m0_ref[...].astype(F32) * _dot(ga_ref[...], w_ref[0])
    acc = acc + m1_ref[...].astype(F32) * _dot(gb_ref[...], w_ref[1])
    acc = acc + m2_ref[...].astype(F32) * _dot(gc_ref[...], w_ref[2])
    o_ref[...] = acc.astype(o_ref.dtype)


def _merge(ga, gb, gc, w_branch, layer, mg, tm, tn):
    m, kdim = ga.shape
    tm = min(tm, m)
    nj = D_MODEL // tn
    g_spec = pl.BlockSpec((tm, kdim), lambda i, j: (i, 0))
    m_specs = [pl.BlockSpec((tm, tn), lambda i, j, br=br: (i, br * nj + j)) for br in range(N_BRANCH)]
    return pl.pallas_call(
        _merge_kernel,
        grid=(m // tm, nj),
        in_specs=[g_spec, g_spec, g_spec,
                  pl.BlockSpec((None, N_BRANCH, kdim, tn), lambda i, j: (layer, 0, 0, j))] + m_specs,
        out_specs=pl.BlockSpec((tm, tn), lambda i, j: (i, j)),
        out_shape=jax.ShapeDtypeStruct((m, D_MODEL), BF16),
        compiler_params=_params("parallel", "arbitrary"),
        name="merge",
    )(ga, gb, gc, w_branch, mg, mg, mg)


def _out_kernel(a_ref, w_ref, x_ref, g_ref, o_ref, *, final_norm):
    y = x_ref[...] + _dot(a_ref[...], w_ref[...])
    if final_norm:
        ms = jnp.mean(y * y, axis=-1, keepdims=True)
        y = y * lax.rsqrt(ms + RMS_EPS) * g_ref[...]
    o_ref[...] = y


def _out_proj(merged, w_out, layer, x2d, final_g, final_norm, tm):
    m, d = x2d.shape
    tm = min(tm, m)
    return pl.pallas_call(
        functools.partial(_out_kernel, final_norm=final_norm),
        grid=(m // tm,),
        in_specs=[pl.BlockSpec((tm, d), lambda i: (i, 0)), pl.BlockSpec((None, d, d), lambda i: (layer, 0, 0)),
                  pl.BlockSpec((tm, d), lambda i: (i, 0)), pl.BlockSpec((1, d), lambda i: (0, 0))],
        out_specs=pl.BlockSpec((tm, d), lambda i: (i, 0)),
        out_shape=jax.ShapeDtypeStruct((m, d), F32),
        compiler_params=_params("parallel"),
        name="out_proj",
    )(merged, w_out, x2d, final_g.reshape(1, d))


def _prep_w_in(w_in):
    seg = lambda a, b: w_in[:, :, a:b]
    parts = [seg(OFF_POOL, OFF_SHIFT),
             seg(OFF_Q, OFF_K) * (HEAD_DIM ** -0.5),
             seg(OFF_K, OFF_GATE), seg(OFF_GATE, OFF_MERGE), seg(OFF_MERGE, w_in.shape[-1]), seg(OFF_SHIFT, OFF_Q),
             jnp.zeros(w_in.shape[:2] + (SHIFT_PAD - SHIFT_WIDTH,), w_in.dtype)]
    return jnp.concatenate(parts, axis=-1).astype(BF16)


def _layer_vectors(l, pool_scale, rwkv_mu, rwkv_w0, rwkv_w_up, rwkv_a0, rwkv_a_up, rwkv_k_k, rwkv_k_a, rwkv_r_k,
                   rwkv_gn_w, rwkv_gn_b, attn_sink):
    w_lora = jnp.zeros((LANES, 2 * BRANCH_WIDTH), F32)
    w_lora = w_lora.at[:LORA_RANK, :BRANCH_WIDTH].set(rwkv_w_up[l]).at[LORA_RANK:, BRANCH_WIDTH:].set(rwkv_a_up[l])
    vecs = jnp.stack([rwkv_w0[l], rwkv_a0[l], rwkv_k_k[l], rwkv_k_a[l], rwkv_r_k[l], rwkv_gn_w[l], rwkv_gn_b[l],
                      jnp.zeros_like(rwkv_w0[l])])
    return dict(pool_scale=pool_scale[l].reshape(1, -1), mu=rwkv_mu[l].reshape(1, -1), vecs=vecs,
                w_lora=w_lora.astype(BF16), sink=attn_sink[l][np.array(ATT_HEAD_ORDER)])


def _sink_rows(sink_ordered, n_q):
    return jnp.repeat(sink_ordered.reshape(-1, 4), n_q, axis=1)[..., None]


def _mixer_layer(x, l, wts, lv, norm_g, final_g, final_norm, *, pos0, pool_hist16, shift_prev, s0_bd, bias, caches,
                 tm):
    b, t, d = x.shape
    x2 = x.reshape(b * t, d)
    h = _rmsnorm(x2, norm_g, BF16, tm)
    mm = functools.partial(_matmul, h, wts["w_in"], l, tm=tm)
    u_pool = mm(W_POOL, BRANCH_WIDTH, act=None, out_dtype=F32, name="proj_pool").reshape(b, t, -1)
    p_shift = mm(W_SHIFT, SHIFT_PAD, act=None, out_dtype=F32, name="proj_shift").reshape(b, t, -1)
    q = mm(W_Q, BRANCH_WIDTH, act=None, out_dtype=BF16, name="proj_q").reshape(b, t, -1)
    kv = mm(W_KV, 2 * KV_WIDTH, act=None, out_dtype=F32, name="proj_kv").reshape(b, t, -1)
    gate = mm(W_GATE, N_BRANCH * BRANCH_WIDTH, act="silu", out_dtype=BF16, name="proj_gate").reshape(b, t, -1)
    mg = mm(W_MERGE, N_BRANCH * D_MODEL, act="sigmoid", out_dtype=BF16, name="proj_merge")

    ga = _pool_branch(u_pool, pool_hist16, gate, wts["pool_w"], l, lv["pool_scale"], pos0)

    t_pad = -(-t // CHUNK) * CHUNK
    if t_pad != t:
        p_in = jnp.pad(p_shift, ((0, 0), (0, t_pad - t), (0, 0)))
        gate_in = jnp.pad(gate, ((0, 0), (0, t_pad - t), (0, 0)))
    else:
        p_in, gate_in = p_shift, gate
    gb, s_bd = _rwkv_branch(p_in, shift_prev, s0_bd, gate_in, lv["mu"], lv["vecs"], lv["w_lora"],
                            n_valid=CHUNK - (t_pad - t))
    gb = gb[:, :t]

    if caches is None:
        gc = _attn_prompt(q, kv, bias, _sink_rows(lv["sink"], CHUNK), gate)
    else:
        gc = _attn_sample(q, kv, caches[0], caches[1], bias, _sink_rows(lv["sink"], t), gate)

    merged = _merge(ga.reshape(b * t, -1), gb.reshape(b * t, -1), gc.reshape(b * t, -1), wts["w_branch"], l, mg,
                    tm=tm, tn=512)
    y = _out_proj(merged, wts["w_out"], l, x2, final_g, final_norm, tm=min(tm, 512)).reshape(b, t, d)
    return (y, kv[..., :KV_WIDTH], kv[..., KV_WIDTH:], _state_from_blockdiag(s_bd), p_shift[:, -1, :SHIFT_WIDTH],
            u_pool[:, t - POOL_HIST:])


def kernel(x_prompt, x_sample, cache_attn_k, cache_attn_v, state_rwkv, state_rwkv_shift, state_pool, norm_g, w_in,
           pool_w, pool_scale, rwkv_mu, rwkv_w0, rwkv_w_up, rwkv_a0, rwkv_a_up, rwkv_k_k, rwkv_k_a, rwkv_r_k,
           rwkv_gn_w, rwkv_gn_b, attn_sink, rel_pos_table, w_branch, w_out, final_norm_g):
    depth = norm_g.shape[0]
    bp, tp, _ = x_prompt.shape
    bs, ts, _ = x_sample.shape
    n_cache = cache_attn_k.shape[2]

    key_p = jnp.arange((BAND_CHUNKS + 1) * CHUNK) - BAND_CHUNKS * CHUNK
    bias_p = _rel_bias(rel_pos_table, key_p, CHUNK)
    key_s = jnp.concatenate([jnp.arange(n_cache) - n_cache, jnp.arange(ts)])
    bias_s = _rel_bias(rel_pos_table, key_s, ts)

    wts = dict(w_in=_prep_w_in(w_in), pool_w=pool_w.astype(BF16), w_branch=w_branch.astype(BF16),
               w_out=w_out.astype(BF16))
    xp, xs = x_prompt, x_sample
    outs_p, outs_s = [], []
    for l in range(depth):
        lv = _layer_vectors(l, pool_scale, rwkv_mu, rwkv_w0, rwkv_w_up, rwkv_a0, rwkv_a_up, rwkv_k_k, rwkv_k_a,
                            rwkv_r_k, rwkv_gn_w, rwkv_gn_b, attn_sink)
        last = l == depth - 1
        xp, kp, vp, sp, shp, plp = _mixer_layer(
            xp, l, wts, lv, norm_g[l], final_norm_g, last, pos0=0,
            pool_hist16=jnp.zeros((bp, 16, BRANCH_WIDTH), F32),
            shift_prev=jnp.zeros((bp, 1, SHIFT_WIDTH), F32),
            s0_bd=jnp.zeros((bp, PAIRS, LANES, LANES), F32),
            bias=bias_p, caches=None, tm=1024)
        outs_p.append((kp[:, -WINDOW:].reshape(bp, WINDOW, ATT_KV_HEADS, HEAD_DIM),
                       vp[:, -WINDOW:].reshape(bp, WINDOW, ATT_KV_HEADS, HEAD_DIM), sp, shp, plp))
        xs, k_s, v_s, s_s, sh_s, pl_s = _mixer_layer(
            xs, l, wts, lv, norm_g[l], final_norm_g, last, pos0=PAST_LEN,
            pool_hist16=jnp.pad(state_pool[l], ((0, 0), (1, 0), (0, 0))),
            shift_prev=state_rwkv_shift[l][:, None, :],
            s0_bd=_state_to_blockdiag(state_rwkv[l]),
            bias=bias_s,
            caches=(cache_attn_k[l].reshape(bs, n_cache, KV_WIDTH), cache_attn_v[l].reshape(bs, n_cache, KV_WIDTH)),
            tm=bs * ts)
        outs_s.append((k_s.reshape(bs, ts, ATT_KV_HEADS, HEAD_DIM), v_s.reshape(bs, ts, ATT_KV_HEADS, HEAD_DIM),
                       s_s, sh_s, pl_s))

    st = lambda outs, i: jnp.stack([o[i] for o in outs])
    return (xp, xs, st(outs_p, 0), st(outs_p, 1), st(outs_s, 0), st(outs_s, 1),
            st(outs_p, 2), st(outs_s, 2), st(outs_p, 3), st(outs_s, 3), st(outs_p, 4), st(outs_s, 4))
```

```python
import functools
import math

import jax
import jax.numpy as jnp
import numpy as np
from jax import lax
from jax.experimental import pallas as pl
from jax.experimental.pallas import tpu as pltpu

F32 = jnp.float32
BF16 = jnp.bfloat16

D_MODEL = 2048
BRANCH_WIDTH = 1024
POOL_WINDOWS = (2, 4, 8, 16)
POOL_GROUP_DIM = BRANCH_WIDTH // len(POOL_WINDOWS)
POOL_HIST = max(POOL_WINDOWS) - 1
HEAD_DIM = 64
RWKV_HEADS = BRANCH_WIDTH // HEAD_DIM
LORA_RANK = 64
SHIFT_WIDTH = 3 * BRANCH_WIDTH + 2 * LORA_RANK
GN_EPS = 64e-5
ATT_HEADS = 16
ATT_KV_HEADS = 4
ATT_GROUP = ATT_HEADS // ATT_KV_HEADS
KV_WIDTH = ATT_KV_HEADS * HEAD_DIM
CHUNK = 64
WINDOW = 128
BAND_CHUNKS = WINDOW // CHUNK
N_BUCKETS = 32
MAX_DISTANCE = 128
NEG_INF = -1e30
RMS_EPS = 1e-6
PAST_LEN = 2048
N_BRANCH = 3
OFF_POOL = 0
OFF_SHIFT = OFF_POOL + BRANCH_WIDTH
OFF_Q = OFF_SHIFT + SHIFT_WIDTH
OFF_K = OFF_Q + BRANCH_WIDTH
OFF_V = OFF_K + KV_WIDTH
OFF_GATE = OFF_V + KV_WIDTH
OFF_MERGE = OFF_GATE + N_BRANCH * BRANCH_WIDTH

LANES = 128
PAIRS = BRANCH_WIDTH // LANES
RWKV_GROUP = 8
KEY_PAD = 256
ATT_HEAD_ORDER = tuple(8 * m + o for m in range(ATT_KV_HEADS // 2) for o in (0, 2, 5, 7, 1, 3, 4, 6))
PROJ_TM = 512
PROJ_SEGMENTS = {
    "pool": (OFF_POOL, BRANCH_WIDTH, 1024, None, F32),
    "shift": (OFF_SHIFT, SHIFT_WIDTH, SHIFT_WIDTH, None, F32),
    "q": (OFF_Q, BRANCH_WIDTH, 1024, None, BF16),
    "kv": (OFF_K, 2 * KV_WIDTH, 2 * KV_WIDTH, None, F32),
    "gate": (OFF_GATE, N_BRANCH * BRANCH_WIDTH, 1536, "silu", BF16),
    "merge": (OFF_MERGE, N_BRANCH * D_MODEL, 2048, "sigmoid", BF16),
}
VMEM_LIMIT = 56 * 1024 * 1024


def _params(*sem):
    return pltpu.CompilerParams(dimension_semantics=sem, vmem_limit_bytes=VMEM_LIMIT)


def _sigmoid(x):
    return 1.0 / (1.0 + jnp.exp(-x))


def _dot(a, b):
    return jnp.dot(a, b, preferred_element_type=F32)


def _dot_nt(a, b):
    return lax.dot_general(a, b, (((1,), (1,)), ((), ())), preferred_element_type=F32)


def _split3(x):
    hi = x.astype(BF16)
    r1 = x - hi.astype(F32)
    mid = r1.astype(BF16)
    lo = (r1 - mid.astype(F32)).astype(BF16)
    return hi, mid, lo


def _dot3_right_exact(x, b):
    hi, mid, lo = _split3(x)
    return _dot(hi, b) + _dot(mid, b) + _dot(lo, b)


def _dot3_left_exact(a, x):
    hi, mid, lo = _split3(x)
    return _dot(a, hi) + _dot(a, mid) + _dot(a, lo)


def _rmsnorm_kernel(x_ref, g_ref, o_ref):
    x = x_ref[...]
    ms = jnp.mean(x * x, axis=-1, keepdims=True)
    o_ref[...] = (x * lax.rsqrt(ms + RMS_EPS) * g_ref[...]).astype(o_ref.dtype)


def _rmsnorm(x2d, g, out_dtype, tm):
    m, d = x2d.shape
    tm = min(tm, m)
    return pl.pallas_call(
        _rmsnorm_kernel,
        grid=(m // tm,),
        in_specs=[pl.BlockSpec((tm, d), lambda i: (i, 0)), pl.BlockSpec((1, d), lambda i: (0, 0))],
        out_specs=pl.BlockSpec((tm, d), lambda i: (i, 0)),
        out_shape=jax.ShapeDtypeStruct((m, d), out_dtype),
        compiler_params=_params("parallel"),
        name="rmsnorm",
    )(x2d, g.reshape(1, d))


def _mm_kernel(a_ref, w_ref, o_ref, *, act):
    acc = _dot(a_ref[...], w_ref[...])
    if act == "silu":
        acc = acc * _sigmoid(acc)
    elif act == "sigmoid":
        acc = _sigmoid(acc)
    o_ref[...] = acc.astype(o_ref.dtype)


def _matmul(a, w, layer, *, tn, act, out_dtype, name):
    m, k = a.shape
    n = w.shape[-1]
    tm = min(PROJ_TM, m)
    return pl.pallas_call(
        functools.partial(_mm_kernel, act=act),
        grid=(n // tn, m // tm),
        in_specs=[pl.BlockSpec((tm, k), lambda j, i: (i, 0)),
                  pl.BlockSpec((None, k, tn), lambda j, i: (layer, 0, j))],
        out_specs=pl.BlockSpec((tm, tn), lambda j, i: (i, j)),
        out_shape=jax.ShapeDtypeStruct((m, n), out_dtype),
        compiler_params=_params("arbitrary", "arbitrary"),
        name=name,
    )(a, w)


def _pool_kernel(u_ref, prev_ref, hist_ref, zg_ref, w_ref, scale_ref, o_ref, *, tt, pos0):
    i = pl.program_id(1)
    u = u_ref[0]
    prev = jnp.where(i == 0, hist_ref[0], prev_ref[0])
    pos = pos0 + i * tt + lax.broadcasted_iota(jnp.int32, (tt, POOL_GROUP_DIM), 0)
    for g, w in enumerate(POOL_WINDOWS):
        sl = slice(g * POOL_GROUP_DIM, (g + 1) * POOL_GROUP_DIM)
        e = jnp.concatenate([prev[:, sl], u[:, sl]], axis=0)
        step = 1
        while step < w:
            e = e + pltpu.roll(e, step, 0)
            step *= 2
        s = e[16:, :]
        cnt = jnp.minimum(w, pos + 1).astype(F32)
        pooled = s / cnt - u[:, sl]
        o = _dot(pooled.astype(BF16), w_ref[g]) * scale_ref[:, sl]
        o_ref[0, :, sl] = (o * zg_ref[0, :, sl].astype(F32)).astype(o_ref.dtype)


def _pool_branch(u, hist16, gate, pool_w, layer, pool_scale, pos0):
    b, t, c = u.shape
    tt = min(t, 512)
    sub = tt // 16
    return pl.pallas_call(
        functools.partial(_pool_kernel, tt=tt, pos0=pos0),
        grid=(b, t // tt),
        in_specs=[
            pl.BlockSpec((1, tt, c), lambda bi, i: (bi, i, 0)),
            pl.BlockSpec((1, 16, c), lambda bi, i: (bi, jnp.maximum(i * sub - 1, 0), 0)),
            pl.BlockSpec((1, 16, c), lambda bi, i: (bi, 0, 0)),
            pl.BlockSpec((1, tt, c), lambda bi, i: (bi, i, 0)),
            pl.BlockSpec((None,) + pool_w.shape[1:], lambda bi, i: (layer, 0, 0, 0)),
            pl.BlockSpec((1, c), lambda bi, i: (0, 0)),
        ],
        out_specs=pl.BlockSpec((1, tt, c), lambda bi, i: (bi, i, 0)),
        out_shape=jax.ShapeDtypeStruct((b, t, c), BF16),
        compiler_params=_params("parallel", "parallel"),
        name="pool_branch",
    )(u, u, hist16, gate, pool_w, pool_scale)


def _rwkv_kernel(p_ref, sprev_ref, s0_ref, zg_ref, mu_ref, vec_ref, wl_ref, o_ref, s_ref, carry_ref,
                 *, n_valid):
    c = pl.program_id(1)
    C = CHUNK

    @pl.when(c == 0)
    def _():
        s_ref[...] = s0_ref[...]
        carry_ref[...] = sprev_ref[0]

    p = p_ref[0]
    row_w = lax.broadcasted_iota(jnp.int32, (C, SHIFT_WIDTH), 0)
    row = lax.broadcasted_iota(jnp.int32, (C, BRANCH_WIDTH), 0)
    prev = jnp.where(row_w == 0, carry_ref[...], pltpu.roll(p, 1, 0))
    carry_ref[...] = p[C - 1:C, :]
    xs = p + (prev - p) * mu_ref[...]

    w0, a0, k_k, k_a, r_k, gn_w, gn_b = (vec_ref[i:i + 1, :] for i in range(7))
    lane = lax.broadcasted_iota(jnp.int32, (C, LANES), 1)
    head0 = lane < HEAD_DIM
    lin = xs[:, 3 * BRANCH_WIDTH:]
    lin = jnp.where(head0, jnp.tanh(lin), lin)
    lora = _dot(lin.astype(BF16), wl_ref[...])
    log_decay = -math.exp(-0.5) * _sigmoid(w0 + lora[:, :BRANCH_WIDTH])
    iclr = _sigmoid(a0 + lora[:, BRANCH_WIDTH:])
    r = xs[:, :BRANCH_WIDTH]
    k = xs[:, BRANCH_WIDTH:2 * BRANCH_WIDTH]
    v = xs[:, 2 * BRANCH_WIDTH:3 * BRANCH_WIDTH]
    kk_raw = k * k_k
    k2 = k * (1.0 + (iclr - 1.0) * k_a)
    if n_valid < C:
        ok = row < n_valid
        log_decay = jnp.where(ok, log_decay, 0.0)
        kk_raw = jnp.where(ok, kk_raw, 0.0)
        k2 = jnp.where(ok, k2, 0.0)
        r = jnp.where(ok, r, 0.0)
        v = jnp.where(ok, v, 0.0)

    ti = lax.broadcasted_iota(jnp.int32, (C, C), 0)
    si = lax.broadcasted_iota(jnp.int32, (C, C), 1)
    tri = (si <= ti).astype(BF16)
    cum = _dot3_left_exact(tri, log_decay)
    cum_x = cum - log_decay
    cum_c = cum[C - 1:C, :]

    ri =lax.broadcasted_iota(jnp.int32, (2 * C, 2 * C), 0)
    ci = lax.broadcasted_iota(jnp.int32, (2 * C, 2 * C), 1)
    same_blk = (ri >= C) == (ci >= C)
    strict = same_blk & ((ci & (C - 1)) < (ri & (C - 1)))
    incl = same_blk & ((ci & (C - 1)) <= (ri & (C - 1)))
    eye2 = (ri == ci).astype(F32)

    def stack(x):
        return jnp.concatenate([jnp.where(head0, x, 0.0), jnp.where(head0, 0.0, x)], axis=0)

    def each(f, *cols):
        return [f(*xs) for xs in zip(*cols)]

    def bf(xs):
        return [x.astype(BF16) for x in xs]

    def head_sums(xs):
        lo = [jnp.sum(jnp.where(head0, x, 0.0), axis=-1, keepdims=True) for x in xs]
        hi = [jnp.sum(jnp.where(head0, 0.0, x), axis=-1, keepdims=True) for x in xs]
        return [jnp.where(head0, a, b) for a, b in zip(lo, hi)]

    def stack_b(x):
        xb = x.astype(BF16)
        zero = jnp.zeros((), BF16)
        return jnp.concatenate([jnp.where(head0, xb, zero), jnp.where(head0, zero, xb)], axis=0)

    def unfold(x, keep):
        xr = pltpu.roll(x, C, 1)
        top, bot = keep[:C], keep[C:]
        from_p = jnp.concatenate([jnp.where(top, x[:C], 0.0), jnp.where(bot, xr[C:], 0.0)], axis=0)
        from_q = jnp.concatenate([jnp.where(top, xr[:C], 0.0), jnp.where(bot, x[C:], 0.0)], axis=0)
        return from_p, from_q

    for g0 in range(0, PAIRS, RWKV_GROUP):
        prs = list(range(g0, g0 + RWKV_GROUP))
        sls = [slice(pr * LANES, (pr + 1) * LANES) for pr in prs]
        r_p = [r[:, sl] for sl in sls]
        k_p = [k2[:, sl] for sl in sls]
        v_p = [v[:, sl] for sl in sls]
        kkr = [kk_raw[:, sl] for sl in sls]
        ic = [iclr[:, sl] for sl in sls]
        cm = [cum[:, sl] for sl in sls]
        cx = [cum_x[:, sl] for sl in sls]
        cc = [cum_c[:, sl] for sl in sls]
        n2 = head_sums(each(lambda x: x * x, kkr))
        kk = each(lambda x, n: x / jnp.maximum(jnp.sqrt(n), 1e-12), kkr, n2)
        b_p = each(lambda x, i: x * i, kk, ic)
        e_neg = each(lambda c_: jnp.exp(-c_), cm)
        e_end = each(lambda c_, e_: jnp.exp(e_ - c_), cm, cc)
        w_c = each(jnp.exp, cc)
        rt = each(lambda x, c_: x * jnp.exp(c_), r_p, cm)
        at_b = each(lambda x, c_: stack_b(-x * jnp.exp(c_)), kk, cx)
        rt_b = each(stack_b, rt)
        v_s = each(stack, v_p)
        v_b = bf(v_s)
        bb_b = each(lambda x, e_: stack_b(x * e_), b_p, e_end)
        kb_b = each(lambda x, e_: stack_b(x * e_), k_p, e_end)
        bk_b = bf(each(lambda b_, k_, e_: jnp.concatenate([b_ * e_, k_ * e_], axis=0), b_p, k_p, e_neg))
        ar_b = each(lambda a_, r_: jnp.concatenate([a_, r_], axis=0), at_b, rt_b)
        prod = each(_dot_nt, ar_b, bk_b)
        a_pair = each(lambda x: unfold(x[:2 * C], strict), prod)
        r_pair = each(lambda x: unfold(x[2 * C:], incl), prod)
        a_ab = [x[0] for x in a_pair]
        a_ak_b = bf([x[1] for x in a_pair])
        a_rb_b = bf([x[0] for x in r_pair])
        a_rk_b = bf([x[1] for x in r_pair])
        tinv = each(lambda x: eye2 + x, a_ab)
        pw_b = bf(a_ab)
        pw_b = bf(each(_dot, pw_b, pw_b))
        n_sq = int(math.log2(C)) - 1
        for step in range(n_sq):
            if step < n_sq - 1:
                both = each(lambda t_, p_: _dot(jnp.concatenate([t_.astype(BF16), p_], axis=0), p_), tinv, pw_b)
                tinv = each(lambda t_, x: t_ + x[:2 * C], tinv, both)
                pw_b = bf([x[2 * C:] for x in both])
            else:
                tinv = each(lambda t_, p_: t_ + _dot(t_.astype(BF16), p_), tinv, pw_b)
        t_b = bf(tinv)
        akv = each(_dot, a_ak_b, v_b)
        au = each(lambda t_, a_, k_: _dot(t_, jnp.concatenate([a_, k_.astype(BF16)], axis=1)), t_b, at_b, akv)
        ry = each(_dot, a_rb_b, bf(au))
        rh = each(lambda x, y_: x + y_[:C, :LANES] + y_[C:, :LANES], rt, ry)
        y0 = each(lambda y_, k_, v_: y_[:, LANES:] + _dot(k_, v_), ry, a_rk_b, v_b)
        mg = each(lambda x, b_: _dot(x.T.astype(BF16), b_), au, bb_b)
        m_lr = [x[:LANES] for x in mg]
        g_bd = each(lambda x, v_, k_: x[LANES:] + _dot(v_.T.astype(BF16), k_), mg, v_s, kb_b)
        s_bd = [s_ref[0, pr] for pr in prs]
        s_b = bf(s_bd)
        y = each(lambda x, s_, y_: _dot_nt(x.astype(BF16), s_) + (y_[:C] + y_[C:]), rh, s_b, y0)
        s_new = each(lambda s_, w_, sb_, m_, g_: s_ * w_ + _dot(sb_, m_.astype(BF16)) + g_,
                     s_bd, w_c, s_b, m_lr, g_bd)
        for pr, s_ in zip(prs, s_new):
            s_ref[0, pr] = s_
        yc = each(lambda x, m_: x - m_ * (1.0 / HEAD_DIM), y, head_sums(y))
        var = head_sums(each(lambda x: x * x, yc))
        rk = head_sums(each(lambda r_, k_, sl: r_ * k_ * r_k[:, sl], r_p, k_p, sls))
        for sl, yc_, var_, rk_, v_ in zip(sls, yc, var, rk, v_p):
            yn = yc_ * lax.rsqrt(var_ * (1.0 / HEAD_DIM) + GN_EPS) * gn_w[:, sl] + gn_b[:, sl]
            o_ref[0, :, sl] = ((yn + rk_ * v_) * zg_ref[0, :, sl].astype(F32)).astype(o_ref.dtype)


def _rwkv_branch(p_shift, shift_prev, s0_bd, gate, mu, vecs, w_lora, n_valid):
    b, t, _ = p_shift.shape
    return pl.pallas_call(
        functools.partial(_rwkv_kernel, n_valid=n_valid),
        grid=(b, t // CHUNK),
        in_specs=[
            pl.BlockSpec((1, CHUNK, SHIFT_WIDTH), lambda bi, c: (bi, c, 0)),
            pl.BlockSpec((1, 1, SHIFT_WIDTH), lambda bi, c: (bi, 0, 0)),
            pl.BlockSpec((1, PAIRS, LANES, LANES), lambda bi, c: (bi, 0, 0, 0)),
            pl.BlockSpec((1, CHUNK, BRANCH_WIDTH), lambda bi, c: (bi, c, 1)),
            pl.BlockSpec((1, SHIFT_WIDTH), lambda bi, c: (0, 0)),
            pl.BlockSpec((8, BRANCH_WIDTH), lambda bi, c: (0, 0)),
            pl.BlockSpec((LANES, 2 * BRANCH_WIDTH), lambda bi, c: (0, 0)),
        ],
        out_specs=[
            pl.BlockSpec((1, CHUNK, BRANCH_WIDTH), lambda bi, c: (bi, c, 0)),
            pl.BlockSpec((1, PAIRS, LANES, LANES), lambda bi, c: (bi, 0, 0, 0)),
        ],
        out_shape=[
            jax.ShapeDtypeStruct((b, t, BRANCH_WIDTH), BF16),
            jax.ShapeDtypeStruct((b, PAIRS, LANES, LANES), F32),
        ],
        scratch_shapes=[pltpu.VMEM((1, SHIFT_WIDTH), F32)],
        compiler_params=_params("parallel", "arbitrary"),
        name="rwkv_branch",
    )(p_shift, shift_prev, s0_bd, gate, mu, vecs, w_lora)


def _state_to_blockdiag(s):
    b = s.shape[0]
    sp = s.reshape(b, PAIRS, 2, HEAD_DIM, HEAD_DIM)
    z = jnp.zeros_like(sp[:, :, 0])
    top = jnp.concatenate([sp[:, :, 0], z], axis=-1)
    bot = jnp.concatenate([z, sp[:, :, 1]], axis=-1)
    return jnp.concatenate([top, bot], axis=-2)


def _state_from_blockdiag(s_bd):
    b = s_bd.shape[0]
    h0 = s_bd[:, :, :HEAD_DIM, :HEAD_DIM]
    h1 = s_bd[:, :, HEAD_DIM:, HEAD_DIM:]
    return jnp.stack([h0, h1], axis=2).reshape(b, RWKV_HEADS, HEAD_DIM, HEAD_DIM)


def _attn_kernel(*refs, n_kv_parts, tq, n_keys, masked):
    q_ref = refs[0]
    k_refs = refs[1:1 + n_kv_parts]
    v_refs = refs[1 + n_kv_parts:1 + 2 * n_kv_parts]
    bias_ref, sink_ref, zg_ref, o_ref = refs[1 + 2 * n_kv_parts:]
    n = pl.program_id(1)
    pad = KEY_PAD - n_keys
    k_parts = [kr[0] for kr in k_refs]
    v_parts = [vr[0] for vr in v_refs]
    if pad:
        k_parts.append(jnp.zeros((pad, KV_WIDTH), F32))
        v_parts.append(jnp.zeros((pad, KV_WIDTH), F32))
    k_all = jnp.concatenate(k_parts, axis=0)
    v_all = jnp.concatenate(v_parts, axis=0)
    lane = lax.broadcasted_iota(jnp.int32, (tq, LANES), 1)
    head0 = lane < HEAD_DIM
    col = lax.broadcasted_iota(jnp.int32, (1, KEY_PAD), 1)
    valid = col < n_keys
    if masked:
        valid = valid & (col // CHUNK + n - BAND_CHUNKS >= 0)
    q = q_ref[0]
    zero = jnp.zeros((), q.dtype)
    lo_half = lambda s: jnp.where(head0, s, zero)
    hi_half = lambda s: jnp.where(head0, zero, s)
    lhs, keys, vals = [], [], []
    for m in range(ATT_KV_HEADS // 2):
        slabs = [q[:, (4 * m + i) * LANES:(4 * m + i + 1) * LANES] for i in range(4)]
        k_sl = k_all[:, m * LANES:(m + 1) * LANES]
        v_sl = v_all[:, m * LANES:(m + 1) * LANES]
        lhs.append(jnp.concatenate([lo_half(slabs[0]), lo_half(slabs[1]), hi_half(slabs[2]), hi_half(slabs[3])], 0))
        lhs.append(jnp.concatenate([hi_half(slabs[0]), hi_half(slabs[1]), lo_half(slabs[2]), lo_half(slabs[3])], 0))
        keys += [k_sl.astype(BF16), pltpu.roll(k_sl, HEAD_DIM, 1).astype(BF16)]
        vals += [v_sl.astype(BF16), pltpu.roll(v_sl, HEAD_DIM, 1).astype(BF16)]
    n_grp = len(lhs)
    ones = jnp.ones((KEY_PAD, LANES), BF16)
    vals = [jnp.concatenate([x, ones], axis=1) for x in vals]
    logits = [jnp.where(valid, _dot_nt(lhs[i], keys[i]) + bias_ref[i], NEG_INF) for i in range(n_grp)]
    sink = [sink_ref[i] for i in range(n_grp)]
    mx = [jnp.maximum(jnp.max(logits[i], axis=-1, keepdims=True), sink[i]) for i in range(n_grp)]
    e = [jnp.exp(jnp.concatenate([logits[i][:, :LANES] - mx[i], logits[i][:, LANES:] - mx[i]], axis=1))
         for i in range(n_grp)]
    pv = [_dot(e[i].astype(BF16), vals[i]) for i in range(n_grp)]
    out = [pv[i][:, :LANES] / (pv[i][:, LANES:] + jnp.exp(sink[i] - mx[i])) for i in range(n_grp)]
    for m in range(ATT_KV_HEADS // 2):
        same, swap = out[2 * m], out[2 * m + 1]
        for i in range(4):
            a, b = same[i * tq:(i + 1) * tq], swap[i * tq:(i + 1) * tq]
            res = jnp.where(head0, a, b) if i < 2 else jnp.where(head0, b, a)
            sl = slice((4 * m + i) * LANES, (4 * m + i + 1) * LANES)
            o_ref[0, :, sl] = (res * zg_ref[0, :, sl].astype(F32)).astype(o_ref.dtype)


def _attn_prompt(q, kv, bias, sink_rows, gate):
    b, t, _ = q.shape
    nc = t // CHUNK
    k_specs, v_specs = ([pl.BlockSpec((1, CHUNK, KV_WIDTH), lambda bi, n, d=d, c=c: (bi, jnp.maximum(n - d, 0), c))
                         for d in range(BAND_CHUNKS, -1, -1)] for c in range(2))
    return pl.pallas_call(
        functools.partial(_attn_kernel, n_kv_parts=BAND_CHUNKS + 1, tq=CHUNK,
                          n_keys=(BAND_CHUNKS + 1) * CHUNK, masked=True),
        grid=(b, nc),
        in_specs=[pl.BlockSpec((1, CHUNK, BRANCH_WIDTH), lambda bi, n: (bi, n, 0))]
        + k_specs + v_specs
        + [pl.BlockSpec(bias.shape, lambda bi, n: (0, 0, 0)),
           pl.BlockSpec(sink_rows.shape, lambda bi, n: (0, 0, 0)),
           pl.BlockSpec((1, CHUNK, BRANCH_WIDTH), lambda bi, n: (bi, n, 2))],
        out_specs=pl.BlockSpec((1, CHUNK, BRANCH_WIDTH), lambda bi, n: (bi, n, 0)),
        out_shape=jax.ShapeDtypeStruct((b, t, BRANCH_WIDTH), BF16),
        compiler_params=_params("parallel", "parallel"),
        name="attn_prompt",
    )(q, kv, kv, kv, kv, kv, kv, bias, sink_rows, gate)


def _attn_sample(q, kv, k_cache, v_cache, bias, sink_rows, gate):
    b, t, _ = q.shape
    n_cache = k_cache.shape[1]
    cache_spec = pl.BlockSpec((1, n_cache, KV_WIDTH), lambda bi, n: (bi, 0, 0))
    return pl.pallas_call(
        functools.partial(_attn_kernel, n_kv_parts=2, tq=t, n_keys=n_cache + t, masked=False),
        grid=(b, 1),
        in_specs=[pl.BlockSpec((1, t, BRANCH_WIDTH), lambda bi, n: (bi, 0, 0)),
                  cache_spec, pl.BlockSpec((1, t, KV_WIDTH), lambda bi, n: (bi, 0, 0)),
                  cache_spec, pl.BlockSpec((1, t, KV_WIDTH), lambda bi, n: (bi, 0, 1)),
                  pl.BlockSpec(bias.shape, lambda bi, n: (0, 0, 0)),
                  pl.BlockSpec(sink_rows.shape, lambda bi, n: (0, 0, 0)),
                  pl.BlockSpec((1, t, BRANCH_WIDTH), lambda bi, n: (bi, 0, 2))],
        out_specs=pl.BlockSpec((1, t, BRANCH_WIDTH), lambda bi, n: (bi, 0, 0)),
        out_shape=jax.ShapeDtypeStruct((b, t, BRANCH_WIDTH), BF16),
        compiler_params=_params("parallel", "parallel"),
        name="attn_sample",
    )(q, k_cache, kv, v_cache, kv, bias, sink_rows, gate)


def _t5_bucket(rel):
    half = N_BUCKETS // 2
    n = -rel
    ret = jnp.where(n < 0, half, 0)
    n = jnp.abs(n)
    max_exact = half // 2
    large = max_exact + (jnp.log(jnp.maximum(n, 1).astype(jnp.float32) / max_exact)
                         / math.log(MAX_DISTANCE / max_exact) * (half - max_exact)).astype(jnp.int32)
    large = jnp.minimum(large, half - 1)
    return ret + jnp.where(n < max_exact, n, large)


def _bias_kernel(tab_ref, onehot_ref, o_ref):
    o_ref[...] = _dot3_right_exact(tab_ref[...], onehot_ref[...])


def _rel_bias(table, key_pos, n_q):
    n_keys = key_pos.shape[0]
    bucket = _t5_bucket(key_pos[None, :] - jnp.arange(n_q)[:, None])
    bucket = jnp.pad(bucket, ((0, 0), (0, KEY_PAD - n_keys)), constant_values=-1)
    onehot = (bucket.reshape(1, -1) == jnp.arange(N_BUCKETS)[:, None]).astype(BF16)
    out = pl.pallas_call(
        _bias_kernel,
        out_shape=jax.ShapeDtypeStruct((ATT_HEADS, n_q * KEY_PAD), F32),
        name="rel_bias",
    )(table.T[np.array(ATT_HEAD_ORDER)], onehot)
    return out.reshape(len(ATT_HEAD_ORDER) // 4, 4 * n_q, KEY_PAD)


def _merge_kernel(ga_ref, gb_ref, gc_ref, w_ref, m0_ref, m1_ref, m2_ref, o_ref):
    acc = m0_ref[...].astype(F32) * _dot(ga_ref[...], w_ref[0])
    acc = acc + m1_ref[...].astype(F32) * _dot(gb_ref[...], w_ref[1])
    acc = acc + m2_ref[...].astype(F32) * _dot(gc_ref[...], w_ref[2])
    o_ref[...] = acc.astype(o_ref.dtype)


def _merge(ga, gb, gc, w_branch, layer, mg, tm, tn):
    m, kdim = ga.shape
    tm = min(tm, m)
    nj = D_MODEL // tn
    g_spec = pl.BlockSpec((tm, kdim), lambda i, j: (i, 0))
    m_specs = [pl.BlockSpec((tm, tn), lambda i, j, br=br: (i, br * nj + j)) for br in range(N_BRANCH)]
    return pl.pallas_call(
        _merge_kernel,
        grid=(m // tm, nj),
        in_specs=[g_spec, g_spec, g_spec,
                  pl.BlockSpec((None, N_BRANCH, kdim, tn), lambda i, j: (layer, 0, 0, j))] + m_specs,
        out_specs=pl.BlockSpec((tm, tn), lambda i, j: (i, j)),
        out_shape=jax.ShapeDtypeStruct((m, D_MODEL), BF16),
        compiler_params=_params("parallel", "arbitrary"),
        name="merge",
    )(ga, gb, gc, w_branch, mg, mg, mg)


def _out_kernel(a_ref, w_ref, x_ref, g_ref, o_ref, *, final_norm):
    y = x_ref[...] + _dot(a_ref[...], w_ref[...])
    if final_norm:
        ms = jnp.mean(y * y, axis=-1, keepdims=True)
        y = y * lax.rsqrt(ms + RMS_EPS) * g_ref[...]
    o_ref[...] = y


def _out_proj(merged, w_out, layer, x2d, final_g, final_norm, tm):
    m, d = x2d.shape
    tm = min(tm, m)
    return pl.pallas_call(
        functools.partial(_out_kernel, final_norm=final_norm),
        grid=(m // tm,),
        in_specs=[pl.BlockSpec((tm, d), lambda i: (i, 0)), pl.BlockSpec((None, d, d), lambda i: (layer, 0, 0)),
                  pl.BlockSpec((tm, d), lambda i: (i, 0)), pl.BlockSpec((1, d), lambda i: (0, 0))],
        out_specs=pl.BlockSpec((tm, d), lambda i: (i, 0)),
        out_shape=jax.ShapeDtypeStruct((m, d), F32),
        compiler_params=_params("parallel"),
        name="out_proj",
    )(merged, w_out, x2d, final_g.reshape(1, d))


def _prep_w_in(w_in):
    out = {}
    for name, (col0, width, _, _, _) in PROJ_SEGMENTS.items():
        seg = w_in[:, :, col0:col0 + width]
        if name == "q":
            seg = seg * (HEAD_DIM ** -0.5)
        out[name] = seg.astype(BF16)
    return out


def _layer_vectors(l, pool_scale, rwkv_mu, rwkv_w0, rwkv_w_up, rwkv_a0, rwkv_a_up, rwkv_k_k, rwkv_k_a, rwkv_r_k,
                   rwkv_gn_w, rwkv_gn_b, attn_sink):
    w_lora = jnp.zeros((LANES, 2 * BRANCH_WIDTH), F32)
    w_lora = w_lora.at[:LORA_RANK, :BRANCH_WIDTH].set(rwkv_w_up[l]).at[LORA_RANK:, BRANCH_WIDTH:].set(rwkv_a_up[l])
    vecs = jnp.stack([rwkv_w0[l], rwkv_a0[l], rwkv_k_k[l], rwkv_k_a[l], rwkv_r_k[l], rwkv_gn_w[l], rwkv_gn_b[l],
                      jnp.zeros_like(rwkv_w0[l])])
    return dict(pool_scale=pool_scale[l].reshape(1, -1), mu=rwkv_mu[l].reshape(1, -1), vecs=vecs,
                w_lora=w_lora.astype(BF16), sink=attn_sink[l][np.array(ATT_HEAD_ORDER)])


def _sink_rows(sink_ordered, n_q):
    rows = jnp.repeat(sink_ordered.reshape(-1, 4), n_q, axis=1)
    return jnp.broadcast_to(rows[..., None], rows.shape + (LANES,))


def _mixer_layer(x, l, wts, lv, norm_g, final_g, final_norm, *, pos0, pool_hist16, shift_prev, s0_bd, bias, caches,
                 tm):
    b, t, d = x.shape
    x2 = x.reshape(b * t, d)
    h = _rmsnorm(x2, norm_g, BF16, tm)
    proj = {name: _matmul(h, wts["w_in"][name], l, tn=tn, act=act, out_dtype=dt, name="proj_" + name)
            for name, (_, _, tn, act, dt) in PROJ_SEGMENTS.items()}
    u_pool, p_shift, q, kv, gate = (proj[name].reshape(b, t, -1) for name in ("pool", "shift", "q", "kv", "gate"))
    mg = proj["merge"]

    ga = _pool_branch(u_pool, pool_hist16, gate, wts["pool_w"], l, lv["pool_scale"], pos0)

    t_pad = -(-t // CHUNK) * CHUNK
    if t_pad != t:
        p_in = jnp.pad(p_shift, ((0, 0), (0, t_pad - t), (0, 0)))
        gate_in = jnp.pad(gate, ((0, 0), (0, t_pad - t), (0, 0)))
    else:
        p_in, gate_in = p_shift, gate
    gb, s_bd = _rwkv_branch(p_in, shift_prev, s0_bd, gate_in, lv["mu"], lv["vecs"], lv["w_lora"],
                            n_valid=CHUNK - (t_pad - t))
    gb = gb[:, :t]

    if caches is None:
        gc = _attn_prompt(q, kv, bias, _sink_rows(lv["sink"], CHUNK), gate)
    else:
        gc = _attn_sample(q, kv, caches[0], caches[1], bias, _sink_rows(lv["sink"], t), gate)

    merged = _merge(ga.reshape(b * t, -1), gb.reshape(b * t, -1), gc.reshape(b * t, -1), wts["w_branch"], l, mg,
                    tm=tm, tn=512)
    y = _out_proj(merged, wts["w_out"], l, x2, final_g, final_norm, tm=min(tm, 512)).reshape(b, t, d)
    return (y, kv[..., :KV_WIDTH], kv[..., KV_WIDTH:], _state_from_blockdiag(s_bd), p_shift[:, -1, :SHIFT_WIDTH],
            u_pool[:, t - POOL_HIST:])


def kernel(x_prompt, x_sample, cache_attn_k, cache_attn_v, state_rwkv, state_rwkv_shift, state_pool, norm_g, w_in,
           pool_w, pool_scale, rwkv_mu, rwkv_w0, rwkv_w_up, rwkv_a0, rwkv_a_up, rwkv_k_k, rwkv_k_a, rwkv_r_k,
           rwkv_gn_w, rwkv_gn_b, attn_sink, rel_pos_table, w_branch, w_out, final_norm_g):
    depth = norm_g.shape[0]
    bp, tp, _ = x_prompt.shape
    bs, ts, _ = x_sample.shape
    n_cache = cache_attn_k.shape[2]

    key_p = jnp.arange((BAND_CHUNKS + 1) * CHUNK) - BAND_CHUNKS * CHUNK
    bias_p = _rel_bias(rel_pos_table, key_p, CHUNK)
    key_s = jnp.concatenate([jnp.arange(n_cache) - n_cache, jnp.arange(ts)])
    bias_s = _rel_bias(rel_pos_table, key_s, ts)

    wts = dict(w_in=_prep_w_in(w_in), pool_w=pool_w.astype(BF16), w_branch=w_branch.astype(BF16),
               w_out=w_out.astype(BF16))
    xp, xs = x_prompt, x_sample
    outs_p, outs_s = [], []
    for l in range(depth):
        lv = _layer_vectors(l, pool_scale, rwkv_mu, rwkv_w0, rwkv_w_up, rwkv_a0, rwkv_a_up, rwkv_k_k, rwkv_k_a,
                            rwkv_r_k, rwkv_gn_w, rwkv_gn_b, attn_sink)
        last = l == depth - 1
        xp, kp, vp, sp, shp, plp = _mixer_layer(
            xp, l, wts, lv, norm_g[l], final_norm_g, last, pos0=0,
            pool_hist16=jnp.zeros((bp, 16, BRANCH_WIDTH), F32),
            shift_prev=jnp.zeros((bp, 1, SHIFT_WIDTH), F32),
            s0_bd=jnp.zeros((bp, PAIRS, LANES, LANES), F32),
            bias=bias_p, caches=None, tm=1024)
        outs_p.append((kp[:, -WINDOW:].reshape(bp, WINDOW, ATT_KV_HEADS, HEAD_DIM),
                       vp[:, -WINDOW:].reshape(bp, WINDOW, ATT_KV_HEADS, HEAD_DIM), sp, shp, plp))
        xs, k_s, v_s, s_s, sh_s, pl_s = _mixer_layer(
            xs, l, wts, lv, norm_g[l], final_norm_g, last, pos0=PAST_LEN,
            pool_hist16=jnp.pad(state_pool[l], ((0, 0), (1, 0), (0, 0))),
            shift_prev=state_rwkv_shift[l][:, None, :],
            s0_bd=_state_to_blockdiag(state_rwkv[l]),
            bias=bias_s,
            caches=(cache_attn_k[l].reshape(bs, n_cache, KV_WIDTH), cache_attn_v[l].reshape(bs, n_cache, KV_WIDTH)),
            tm=bs * ts)
        outs_s.append((k_s.reshape(bs, ts, ATT_KV_HEADS, HEAD_DIM), v_s.reshape(bs, ts, ATT_KV_HEADS, HEAD_DIM),
                       s_s, sh_s, pl_s))

    st = lambda outs, i: jnp.stack([o[i] for o in outs])
    return (xp, xs, st(outs_p, 0), st(outs_p, 1), st(outs_s, 0), st(outs_s, 1),
            st(outs_p, 2), st(outs_s, 2), st(outs_p, 3), st(outs_s, 3), st(outs_p, 4), st(outs_s, 4))
```

```python
import functools
import math

import jax
import jax.numpy as jnp
import numpy as np
from jax import lax
from jax.experimental import pallas as pl
from jax.experimental.pallas import tpu as pltpu

F32 = jnp.float32
BF16 = jnp.bfloat16

D_MODEL = 2048
BRANCH_WIDTH = 1024
POOL_WINDOWS = (2, 4, 8, 16)
POOL_GROUP_DIM = BRANCH_WIDTH // len(POOL_WINDOWS)
POOL_HIST = max(POOL_WINDOWS) - 1
HEAD_DIM = 64
RWKV_HEADS = BRANCH_WIDTH // HEAD_DIM
LORA_RANK = 64
SHIFT_WIDTH = 3 * BRANCH_WIDTH + 2 * LORA_RANK
GN_EPS = 64e-5
ATT_HEADS = 16
ATT_KV_HEADS = 4
ATT_GROUP = ATT_HEADS // ATT_KV_HEADS
KV_WIDTH = ATT_KV_HEADS * HEAD_DIM
CHUNK = 64
WINDOW = 128
BAND_CHUNKS = WINDOW // CHUNK
N_BUCKETS = 32
MAX_DISTANCE = 128
NEG_INF = -1e30
RMS_EPS = 1e-6
PAST_LEN = 2048
N_BRANCH = 3
OFF_POOL = 0
OFF_SHIFT = OFF_POOL + BRANCH_WIDTH
OFF_Q = OFF_SHIFT + SHIFT_WIDTH
OFF_K = OFF_Q + BRANCH_WIDTH
OFF_V = OFF_K + KV_WIDTH
OFF_GATE = OFF_V + KV_WIDTH
OFF_MERGE = OFF_GATE + N_BRANCH * BRANCH_WIDTH

LANES = 128
PAIRS = BRANCH_WIDTH // LANES
RWKV_GROUP = 8
KEY_PAD = 256
ATT_HEAD_ORDER = tuple(8 * m + o for m in range(ATT_KV_HEADS // 2) for o in (0, 2, 5, 7, 1, 3, 4, 6))
PROJ_TM = 512
MERGE_TN = 1024
PROJ_SEGMENTS = {
    "pool": (OFF_POOL, BRANCH_WIDTH, 1024, None, F32),
    "shift": (OFF_SHIFT, SHIFT_WIDTH, SHIFT_WIDTH, None, F32),
    "q": (OFF_Q, BRANCH_WIDTH, 1024, "att_scale", BF16),
    "kv": (OFF_K, 2 * KV_WIDTH, 2 * KV_WIDTH, None, F32),
    "gate": (OFF_GATE, N_BRANCH * BRANCH_WIDTH, 3072, "silu", BF16),
    "merge": (OFF_MERGE, N_BRANCH * D_MODEL, 3072, "sigmoid", BF16),
}
VMEM_LIMIT = 56 * 1024 * 1024


def _params(*sem):
    return pltpu.CompilerParams(dimension_semantics=sem, vmem_limit_bytes=VMEM_LIMIT)


def _sigmoid(x):
    return 1.0 / (1.0 + jnp.exp(-x))


def _dot(a, b):
    return jnp.dot(a, b, preferred_element_type=F32)


def _dot_nt(a, b):
    return lax.dot_general(a, b, (((1,), (1,)), ((), ())), preferred_element_type=F32)


def _split3(x):
    hi = x.astype(BF16)
    r1 = x - hi.astype(F32)
    mid = r1.astype(BF16)
    lo = (r1 - mid.astype(F32)).astype(BF16)
    return hi, mid, lo


def _dot3_right_exact(x, b):
    hi, mid, lo = _split3(x)
    return _dot(hi, b) + _dot(mid, b) + _dot(lo, b)


def _dot3_left_exact(a, x):
    hi, mid, lo = _split3(x)
    return _dot(a, hi) + _dot(a, mid) + _dot(a, lo)


def _rmsnorm_kernel(x_ref, g_ref, o_ref):
    x = x_ref[...]
    ms = jnp.mean(x * x, axis=-1, keepdims=True)
    o_ref[...] = (x * lax.rsqrt(ms + RMS_EPS) * g_ref[...]).astype(o_ref.dtype)


def _rmsnorm(x2d, g, out_dtype, tm):
    m, d = x2d.shape
    tm = min(tm, m)
    return pl.pallas_call(
        _rmsnorm_kernel,
        grid=(m // tm,),
        in_specs=[pl.BlockSpec((tm, d), lambda i: (i, 0)), pl.BlockSpec((1, d), lambda i: (0, 0))],
        out_specs=pl.BlockSpec((tm, d), lambda i: (i, 0)),
        out_shape=jax.ShapeDtypeStruct((m, d), out_dtype),
        compiler_params=_params("parallel"),
        name="rmsnorm",
    )(x2d, g.reshape(1, d))


def _mm_kernel(a_ref, w_ref, o_ref, *, act):
    acc = _dot(a_ref[...], w_ref[0])
    if act == "silu":
        acc = acc * _sigmoid(acc)
    elif act == "sigmoid":
        acc = _sigmoid(acc)
    elif act == "att_scale":
        acc = acc * (HEAD_DIM ** -0.5)
    o_ref[...] = acc.astype(o_ref.dtype)


def _matmul(a, w, layer, col0, n, *, tn, act, out_dtype, name):
    m, k = a.shape
    tm = min(PROJ_TM, m)
    return pl.pallas_call(
        functools.partial(_mm_kernel, act=act),
        grid=(n // tn, m // tm),
        in_specs=[pl.BlockSpec((tm, k), lambda j, i: (i, 0)),
                  pl.BlockSpec((pl.Element(1), pl.Element(k), pl.Element(tn)),
                               lambda j, i: (layer, 0, pl.multiple_of(col0 + j * tn, LANES)))],
        out_specs=pl.BlockSpec((tm, tn), lambda j, i: (i, j)),
        out_shape=jax.ShapeDtypeStruct((m, n), out_dtype),
        compiler_params=_params("arbitrary", "arbitrary"),
        name=name,
    )(a, w)


def _pool_kernel(u_ref, prev_ref, hist_ref, zg_ref, w_ref, scale_ref, o_ref, *, tt, pos0):
    i = pl.program_id(1)
    u = u_ref[0]
    prev = jnp.where(i == 0, hist_ref[0], prev_ref[0])
    pos = pos0 + i * tt + lax.broadcasted_iota(jnp.int32, (tt, POOL_GROUP_DIM), 0)
    for g, w in enumerate(POOL_WINDOWS):
        sl = slice(g * POOL_GROUP_DIM, (g + 1) * POOL_GROUP_DIM)
        e = jnp.concatenate([prev[:, sl], u[:, sl]], axis=0)
        step = 1
        while step < w:
            e = e + pltpu.roll(e, step, 0)
            step *= 2
        s = e[16:, :]
        cnt = jnp.minimum(w, pos + 1).astype(F32)
        pooled = s / cnt - u[:, sl]
        o = _dot(pooled.astype(BF16), w_ref[g]) * scale_ref[:, sl]
        o_ref[0, :, sl] = (o * zg_ref[0, :, sl].astype(F32)).astype(o_ref.dtype)


def _pool_branch(u, hist16, gate, pool_w, layer, pool_scale, pos0):
    b, t, c = u.shape
    tt = min(t, 512)
    sub = tt // 16
    return pl.pallas_call(
        functools.partial(_pool_kernel, tt=tt, pos0=pos0),
        grid=(b, t // tt),
        in_specs=[
            pl.BlockSpec((1, tt, c), lambda bi, i: (bi, i, 0)),
            pl.BlockSpec((1, 16, c), lambda bi, i: (bi, jnp.maximum(i * sub - 1, 0), 0)),
            pl.BlockSpec((1, 16, c), lambda bi, i: (bi, 0, 0)),
            pl.BlockSpec((1, tt, c), lambda bi, i: (bi, i, 0)),
            pl.BlockSpec((None,) + pool_w.shape[1:], lambda bi, i: (layer, 0, 0, 0)),
            pl.BlockSpec((1, c), lambda bi, i: (0, 0)),
        ],
        out_specs=pl.BlockSpec((1, tt, c), lambda bi, i: (bi, i, 0)),
        out_shape=jax.ShapeDtypeStruct((b, t, c), BF16),
        compiler_params=_params("parallel", "parallel"),
        name="pool_branch",
    )(u, u, hist16, gate, pool_w, pool_scale)


def _rwkv_kernel(p_ref, sprev_ref, s0_ref, zg_ref, mu_ref, vec_ref, wl_ref, o_ref, s_ref, carry_ref,
                 *, n_valid):
    c = pl.program_id(1)
    C = CHUNK

    @pl.when(c == 0)
    def _():
        s_ref[...] = s0_ref[...]
        carry_ref[...] = sprev_ref[0]

    p = p_ref[0]
    row_w = lax.broadcasted_iota(jnp.int32, (C, SHIFT_WIDTH), 0)
    row = lax.broadcasted_iota(jnp.int32, (C, BRANCH_WIDTH), 0)
    prev = jnp.where(row_w == 0, carry_ref[...], pltpu.roll(p, 1, 0))
    carry_ref[...] = p[C - 1:C, :]
    xs = p + (prev - p) * mu_ref[...]

    w0, a0, k_k, k_a, r_k, gn_w, gn_b = (vec_ref[i:i + 1, :] for i in range(7))
    lane = lax.broadcasted_iota(jnp.int32, (C, LANES), 1)
    head0 = lane < HEAD_DIM
    lin = xs[:, 3 * BRANCH_WIDTH:]
    lin = jnp.where(head0, jnp.tanh(lin), lin)
    lora = _dot(lin.astype(BF16), wl_ref[...])
    log_decay = -math.exp(-0.5) * _sigmoid(w0 + lora[:, :BRANCH_WIDTH])
    iclr = _sigmoid(a0 + lora[:, BRANCH_WIDTH:])
    r = xs[:, :BRANCH_WIDTH]
    k = xs[:, BRANCH_WIDTH:2 * BRANCH_WIDTH]
    v = xs[:, 2 * BRANCH_WIDTH:3 * BRANCH_WIDTH]
    kk_raw = k * k_k
    k2 = k * (1.0 + (iclr - 1.0) * k_a)
    if n_valid < C:
        ok = row < n_valid
        log_decay = jnp.where(ok, log_decay, 0.0)
        kk_raw = jnp.where(ok, kk_raw, 0.0)
        k2 = jnp.where(ok, k2, 0.0)
        r = jnp.where(ok, r, 0.0)
        v = jnp.where(ok, v, 0.0)

    ti = lax.broadcasted_iota(jnp.int32, (C, C), 0)
    si = lax.broadcasted_iota(jnp.int32, (C, C), 1)
    tri = (si <= ti).astype(BF16)
    cum = _dot3_left_exact(tri, log_decay)
    cum_x = cum - log_decay
    cum_c = cum[C - 1:C, :]

    ri =lax.broadcasted_iota(jnp.int32, (2 * C, 2 * C), 0)
    ci = lax.broadcasted_iota(jnp.int32, (2 * C, 2 * C), 1)
    same_blk = (ri >= C) == (ci >= C)
    strict = same_blk & ((ci & (C - 1)) < (ri & (C - 1)))
    incl = same_blk & ((ci & (C - 1)) <= (ri & (C - 1)))
    eye2 = (ri == ci).astype(F32)

    def stack(x):
        return jnp.concatenate([jnp.where(head0, x, 0.0), jnp.where(head0, 0.0, x)], axis=0)

    def each(f, *cols):
        return [f(*xs) for xs in zip(*cols)]

    def bf(xs):
        return [x.astype(BF16) for x in xs]

    def head_sums(xs):
        lo = [jnp.sum(jnp.where(head0, x, 0.0), axis=-1, keepdims=True) for x in xs]
        hi = [jnp.sum(jnp.where(head0, 0.0, x), axis=-1, keepdims=True) for x in xs]
        return [jnp.where(head0, a, b) for a, b in zip(lo, hi)]

    def stack_b(x):
        xb = x.astype(BF16)
        zero = jnp.zeros((), BF16)
        return jnp.concatenate([jnp.where(head0, xb, zero), jnp.where(head0, zero, xb)], axis=0)

    def unfold(x, keep):
        xr = pltpu.roll(x, C, 1)
        top, bot = keep[:C], keep[C:]
        from_p = jnp.concatenate([jnp.where(top, x[:C], 0.0), jnp.where(bot, xr[C:], 0.0)], axis=0)
        from_q = jnp.concatenate([jnp.where(top, xr[:C], 0.0), jnp.where(bot, x[C:], 0.0)], axis=0)
        return from_p, from_q

    for g0 in range(0, PAIRS, RWKV_GROUP):
        prs = list(range(g0, g0 + RWKV_GROUP))
        sls = [slice(pr * LANES, (pr + 1) * LANES) for pr in prs]
        r_p = [r[:, sl] for sl in sls]
        k_p = [k2[:, sl] for sl in sls]
        v_p = [v[:, sl] for sl in sls]
        kkr = [kk_raw[:, sl] for sl in sls]
        ic = [iclr[:, sl] for sl in sls]
        cm = [cum[:, sl] for sl in sls]
        cx = [cum_x[:, sl] for sl in sls]
        cc = [cum_c[:, sl] for sl in sls]
        n2 = head_sums(each(lambda x: x * x, kkr))
        kk = each(lambda x, n: x / jnp.maximum(jnp.sqrt(n), 1e-12), kkr, n2)
        b_p = each(lambda x, i: x * i, kk, ic)
        e_neg = each(lambda c_: jnp.exp(-c_), cm)
        e_end = each(lambda c_, e_: jnp.exp(e_ - c_), cm, cc)
        w_c = each(jnp.exp, cc)
        rt = each(lambda x, c_: x * jnp.exp(c_), r_p, cm)
        at_b = each(lambda x, c_: stack_b(-x * jnp.exp(c_)), kk, cx)
        rt_b = each(stack_b, rt)
        v_s = each(stack, v_p)
        v_b = bf(v_s)
        bb_b = each(lambda x, e_: stack_b(x * e_), b_p, e_end)
        kb_b = each(lambda x, e_: stack_b(x * e_), k_p, e_end)
        bk_b = bf(each(lambda b_, k_, e_: jnp.concatenate([b_ * e_, k_ * e_], axis=0), b_p, k_p, e_neg))
        ar_b = each(lambda a_, r_: jnp.concatenate([a_, r_], axis=0), at_b, rt_b)
        prod = each(_dot_nt, ar_b, bk_b)
        a_pair = each(lambda x: unfold(x[:2 * C], strict), prod)
        r_pair = each(lambda x: unfold(x[2 * C:], incl), prod)
        a_ab = [x[0] for x in a_pair]
        a_ak_b = bf([x[1] for x in a_pair])
        a_rb_b = bf([x[0] for x in r_pair])
        a_rk_b = bf([x[1] for x in r_pair])
        tinv = each(lambda x: eye2 + x, a_ab)
        pw_b = bf(a_ab)
        pw_b = bf(each(_dot, pw_b, pw_b))
        n_sq = int(math.log2(C)) - 1
        for step in range(n_sq):
            if step < n_sq - 1:
                both = each(lambda t_, p_: _dot(p_, jnp.concatenate([t_.astype(BF16), p_], axis=1)), tinv, pw_b)
                tinv = each(lambda t_, x: t_ + x[:, :2 * C], tinv, both)
                pw_b = bf([x[:, 2 * C:] for x in both])
            else:
                tinv = each(lambda t_, p_: t_ + _dot(p_, t_.astype(BF16)), tinv, pw_b)
        t_b = bf(tinv)
        akv = each(_dot, a_ak_b, v_b)
        au = each(lambda t_, a_, k_: _dot(t_, jnp.concatenate([a_, k_.astype(BF16)], axis=1)), t_b, at_b, akv)
        ry = each(_dot, a_rb_b, bf(au))
        rh = each(lambda x, y_: x + y_[:C, :LANES] + y_[C:, :LANES], rt, ry)
        y0 = each(lambda y_, k_, v_: y_[:, LANES:] + _dot(k_, v_), ry, a_rk_b, v_b)
        mg = each(lambda x, b_: _dot(x.T.astype(BF16), b_), au, bb_b)
        m_lr = [x[:LANES] for x in mg]
        g_bd = each(lambda x, v_, k_: x[LANES:] + _dot(v_.T.astype(BF16), k_), mg, v_s, kb_b)
        s_bd = [s_ref[0, pr] for pr in prs]
        s_b = bf(s_bd)
        y = each(lambda x, s_, y_: _dot_nt(x.astype(BF16), s_) + (y_[:C] + y_[C:]), rh, s_b, y0)
        s_new = each(lambda s_, w_, sb_, m_, g_: s_ * w_ + _dot(sb_, m_.astype(BF16)) + g_,
                     s_bd, w_c, s_b, m_lr, g_bd)
        for pr, s_ in zip(prs, s_new):
            s_ref[0, pr] = s_
        yc = each(lambda x, m_: x - m_ * (1.0 / HEAD_DIM), y, head_sums(y))
        var = head_sums(each(lambda x: x * x, yc))
        rk = head_sums(each(lambda r_, k_, sl: r_ * k_ * r_k[:, sl], r_p, k_p, sls))
        for sl, yc_, var_, rk_, v_ in zip(sls, yc, var, rk, v_p):
            yn = yc_ * lax.rsqrt(var_ * (1.0 / HEAD_DIM) + GN_EPS) * gn_w[:, sl] + gn_b[:, sl]
            o_ref[0, :, sl] = ((yn + rk_ * v_) * zg_ref[0, :, sl].astype(F32)).astype(o_ref.dtype)


def _rwkv_branch(p_shift, shift_prev, s0_bd, gate, mu, vecs, w_lora, n_valid):
    b, t, _ = p_shift.shape
    return pl.pallas_call(
        functools.partial(_rwkv_kernel, n_valid=n_valid),
        grid=(b, t // CHUNK),
        in_specs=[
            pl.BlockSpec((1, CHUNK, SHIFT_WIDTH), lambda bi, c: (bi, c, 0)),
            pl.BlockSpec((1, 1, SHIFT_WIDTH), lambda bi, c: (bi, 0, 0)),
            pl.BlockSpec((1, PAIRS, LANES, LANES), lambda bi, c: (bi, 0, 0, 0)),
            pl.BlockSpec((1, CHUNK, BRANCH_WIDTH), lambda bi, c: (bi, c, 1)),
            pl.BlockSpec((1, SHIFT_WIDTH), lambda bi, c: (0, 0)),
            pl.BlockSpec((8, BRANCH_WIDTH), lambda bi, c: (0, 0)),
            pl.BlockSpec((LANES, 2 * BRANCH_WIDTH), lambda bi, c: (0, 0)),
        ],
        out_specs=[
            pl.BlockSpec((1, CHUNK, BRANCH_WIDTH), lambda bi, c: (bi, c, 0)),
            pl.BlockSpec((1, PAIRS, LANES, LANES), lambda bi, c: (bi, 0, 0, 0)),
        ],
        out_shape=[
            jax.ShapeDtypeStruct((b, t, BRANCH_WIDTH), BF16),
            jax.ShapeDtypeStruct((b, PAIRS, LANES, LANES), F32),
        ],
        scratch_shapes=[pltpu.VMEM((1, SHIFT_WIDTH), F32)],
        compiler_params=_params("parallel", "arbitrary"),
        name="rwkv_branch",
    )(p_shift, shift_prev, s0_bd, gate, mu, vecs, w_lora)


def _state_to_blockdiag(s):
    b = s.shape[0]
    sp = s.reshape(b, PAIRS, 2, HEAD_DIM, HEAD_DIM)
    z = jnp.zeros_like(sp[:, :, 0])
    top = jnp.concatenate([sp[:, :, 0], z], axis=-1)
    bot = jnp.concatenate([z, sp[:, :, 1]], axis=-1)
    return jnp.concatenate([top, bot], axis=-2)


def _state_from_blockdiag(s_bd):
    b = s_bd.shape[0]
    h0 = s_bd[:, :, :HEAD_DIM, :HEAD_DIM]
    h1 = s_bd[:, :, HEAD_DIM:, HEAD_DIM:]
    return jnp.stack([h0, h1], axis=2).reshape(b, RWKV_HEADS, HEAD_DIM, HEAD_DIM)


def _attn_kernel(*refs, n_kv_parts, tq, n_keys, masked):
    q_ref = refs[0]
    k_refs = refs[1:1 + n_kv_parts]
    v_refs = refs[1 + n_kv_parts:1 + 2 * n_kv_parts]
    bias_ref, sink_ref, zg_ref, o_ref = refs[1 + 2 * n_kv_parts:]
    n = pl.program_id(1)
    pad = KEY_PAD - n_keys
    k_parts = [kr[0] for kr in k_refs]
    v_parts = [vr[0] for vr in v_refs]
    if pad:
        k_parts.append(jnp.zeros((pad, KV_WIDTH), F32))
        v_parts.append(jnp.zeros((pad, KV_WIDTH), F32))
    k_all = jnp.concatenate(k_parts, axis=0)
    v_all = jnp.concatenate(v_parts, axis=0)
    lane = lax.broadcasted_iota(jnp.int32, (tq, LANES), 1)
    head0 = lane < HEAD_DIM
    col = lax.broadcasted_iota(jnp.int32, (1, KEY_PAD), 1)
    valid = col < n_keys
    if masked:
        valid = valid & (col // CHUNK + n - BAND_CHUNKS >= 0)
    q = q_ref[0]
    zero = jnp.zeros((), q.dtype)
    lo_half = lambda s: jnp.where(head0, s, zero)
    hi_half = lambda s: jnp.where(head0, zero, s)
    lhs, keys, vals = [], [], []
    for m in range(ATT_KV_HEADS // 2):
        slabs = [q[:, (4 * m + i) * LANES:(4 * m + i + 1) * LANES] for i in range(4)]
        k_sl = k_all[:, m * LANES:(m + 1) * LANES]
        v_sl = v_all[:, m * LANES:(m + 1) * LANES]
        lhs.append(jnp.concatenate([lo_half(slabs[0]), lo_half(slabs[1]), hi_half(slabs[2]), hi_half(slabs[3])], 0))
        lhs.append(jnp.concatenate([hi_half(slabs[0]), hi_half(slabs[1]), lo_half(slabs[2]), lo_half(slabs[3])], 0))
        keys += [k_sl.astype(BF16), pltpu.roll(k_sl, HEAD_DIM, 1).astype(BF16)]
        vals += [v_sl.astype(BF16), pltpu.roll(v_sl, HEAD_DIM, 1).astype(BF16)]
    n_grp = len(lhs)
    ones = jnp.ones((KEY_PAD, LANES), BF16)
    vals = [jnp.concatenate([x, ones], axis=1) for x in vals]
    logits = [jnp.where(valid, _dot_nt(lhs[i], keys[i]) + bias_ref[i], NEG_INF) for i in range(n_grp)]
    sink = [sink_ref[i] for i in range(n_grp)]
    mx = [jnp.maximum(jnp.max(logits[i], axis=-1, keepdims=True), sink[i]) for i in range(n_grp)]
    e = [jnp.exp(jnp.concatenate([logits[i][:, :LANES] - mx[i], logits[i][:, LANES:] - mx[i]], axis=1))
         for i in range(n_grp)]
    pv = [_dot(e[i].astype(BF16), vals[i]) for i in range(n_grp)]
    out = [pv[i][:, :LANES] / (pv[i][:, LANES:] + jnp.exp(sink[i] - mx[i])) for i in range(n_grp)]
    for m in range(ATT_KV_HEADS // 2):
        same, swap = out[2 * m], out[2 * m + 1]
        for i in range(4):
            a, b = same[i * tq:(i + 1) * tq], swap[i * tq:(i + 1) * tq]
            res = jnp.where(head0, a, b) if i < 2 else jnp.where(head0, b, a)
            sl = slice((4 * m + i) * LANES, (4 * m + i + 1) * LANES)
            o_ref[0, :, sl] = (res * zg_ref[0, :, sl].astype(F32)).astype(o_ref.dtype)


def _attn_prompt(q, kv, bias, sink_rows, gate):
    b, t, _ = q.shape
    nc = t // CHUNK
    k_specs, v_specs = ([pl.BlockSpec((1, CHUNK, KV_WIDTH), lambda bi, n, d=d, c=c: (bi, jnp.maximum(n - d, 0), c))
                         for d in range(BAND_CHUNKS, -1, -1)] for c in range(2))
    return pl.pallas_call(
        functools.partial(_attn_kernel, n_kv_parts=BAND_CHUNKS + 1, tq=CHUNK,
                          n_keys=(BAND_CHUNKS + 1) * CHUNK, masked=True),
        grid=(b, nc),
        in_specs=[pl.BlockSpec((1, CHUNK, BRANCH_WIDTH), lambda bi, n: (bi, n, 0))]
        + k_specs + v_specs
        + [pl.BlockSpec(bias.shape, lambda bi, n: (0, 0, 0)),
           pl.BlockSpec(sink_rows.shape, lambda bi, n: (0, 0, 0)),
           pl.BlockSpec((1, CHUNK, BRANCH_WIDTH), lambda bi, n: (bi, n, 2))],
        out_specs=pl.BlockSpec((1, CHUNK, BRANCH_WIDTH), lambda bi, n: (bi, n, 0)),
        out_shape=jax.ShapeDtypeStruct((b, t, BRANCH_WIDTH), BF16),
        compiler_params=_params("parallel", "parallel"),
        name="attn_prompt",
    )(q, kv, kv, kv, kv, kv, kv, bias, sink_rows, gate)


def _attn_sample(q, kv, k_cache, v_cache, bias, sink_rows, gate):
    b, t, _ = q.shape
    n_cache = k_cache.shape[1]
    cache_spec = pl.BlockSpec((1, n_cache, KV_WIDTH), lambda bi, n: (bi, 0, 0))
    return pl.pallas_call(
        functools.partial(_attn_kernel, n_kv_parts=2, tq=t, n_keys=n_cache + t, masked=False),
        grid=(b, 1),
        in_specs=[pl.BlockSpec((1, t, BRANCH_WIDTH), lambda bi, n: (bi, 0, 0)),
                  cache_spec, pl.BlockSpec((1, t, KV_WIDTH), lambda bi, n: (bi, 0, 0)),
                  cache_spec, pl.BlockSpec((1, t, KV_WIDTH), lambda bi, n: (bi, 0, 1)),
                  pl.BlockSpec(bias.shape, lambda bi, n: (0, 0, 0)),
                  pl.BlockSpec(sink_rows.shape, lambda bi, n: (0, 0, 0)),
                  pl.BlockSpec((1, t, BRANCH_WIDTH), lambda bi, n: (bi, 0, 2))],
        out_specs=pl.BlockSpec((1, t, BRANCH_WIDTH), lambda bi, n: (bi, 0, 0)),
        out_shape=jax.ShapeDtypeStruct((b, t, BRANCH_WIDTH), BF16),
        compiler_params=_params("parallel", "parallel"),
        name="attn_sample",
    )(q, k_cache, kv, v_cache, kv, bias, sink_rows, gate)


def _t5_bucket(rel):
    half = N_BUCKETS // 2
    n = -rel
    ret = jnp.where(n < 0, half, 0)
    n = jnp.abs(n)
    max_exact = half // 2
    large = max_exact + (jnp.log(jnp.maximum(n, 1).astype(jnp.float32) / max_exact)
                         / math.log(MAX_DISTANCE / max_exact) * (half - max_exact)).astype(jnp.int32)
    large = jnp.minimum(large, half - 1)
    return ret + jnp.where(n < max_exact, n, large)


def _bias_kernel(tab_ref, onehot_ref, o_ref):
    o_ref[...] = _dot3_right_exact(tab_ref[...], onehot_ref[...])


def _rel_bias(table, key_pos, n_q):
    n_keys = key_pos.shape[0]
    bucket = _t5_bucket(key_pos[None, :] - jnp.arange(n_q)[:, None])
    bucket = jnp.pad(bucket, ((0, 0), (0, KEY_PAD - n_keys)), constant_values=-1)
    onehot = (bucket.reshape(1, -1) == jnp.arange(N_BUCKETS)[:, None]).astype(BF16)
    out = pl.pallas_call(
        _bias_kernel,
        out_shape=jax.ShapeDtypeStruct((ATT_HEADS, n_q * KEY_PAD), F32),
        name="rel_bias",
    )(table.T[np.array(ATT_HEAD_ORDER)], onehot)
    return out.reshape(len(ATT_HEAD_ORDER) // 4, 4 * n_q, KEY_PAD)


def _merge_kernel(ga_ref, gb_ref, gc_ref, w_ref, m0_ref, m1_ref, m2_ref, o_ref):
    acc = m0_ref[...].astype(F32) * _dot(ga_ref[...], w_ref[0])
    acc = acc + m1_ref[...].astype(F32) * _dot(gb_ref[...], w_ref[1])
    acc = acc + m2_ref[...].astype(F32) * _dot(gc_ref[...], w_ref[2])
    o_ref[...] = acc.astype(o_ref.dtype)


def _merge(ga, gb, gc, w_branch, layer, mg):
    m, kdim = ga.shape
    tm = min(PROJ_TM, m)
    tn = MERGE_TN
    nj = D_MODEL // tn
    g_spec = pl.BlockSpec((tm, kdim), lambda j, i: (i, 0))
    m_specs = [pl.BlockSpec((tm, tn), lambda j, i, br=br: (i, br * nj + j)) for br in range(N_BRANCH)]
    return pl.pallas_call(
        _merge_kernel,
        grid=(nj, m // tm),
        in_specs=[g_spec, g_spec, g_spec,
                  pl.BlockSpec((None, N_BRANCH, kdim, tn), lambda j, i: (layer, 0, 0, j))] + m_specs,
        out_specs=pl.BlockSpec((tm, tn), lambda j, i: (i, j)),
        out_shape=jax.ShapeDtypeStruct((m, D_MODEL), BF16),
        compiler_params=_params("arbitrary", "arbitrary"),
        name="merge",
    )(ga, gb, gc, w_branch, mg, mg, mg)


def _out_kernel(a_ref, w_ref, x_ref, g_ref, *o_refs, final_norm):
    y = x_ref[...] + _dot(a_ref[...], w_ref[...])
    ms = jnp.mean(y * y, axis=-1, keepdims=True)
    normed = y * lax.rsqrt(ms + RMS_EPS) * g_ref[...]
    if final_norm:
        o_refs[0][...] = normed
    else:
        o_refs[0][...] = y
        o_refs[1][...] = normed.astype(o_refs[1].dtype)


def _out_proj(merged, w_out, layer, x2d, gain, final_norm):
    m, d = x2d.shape
    tm = min(PROJ_TM, m)
    row_spec = pl.BlockSpec((tm, d), lambda i: (i, 0))
    out_specs, out_shape = [row_spec], [jax.ShapeDtypeStruct((m, d), F32)]
    if not final_norm:
        out_specs.append(row_spec)
        out_shape.append(jax.ShapeDtypeStruct((m, d), BF16))
    return pl.pallas_call(
        functools.partial(_out_kernel, final_norm=final_norm),
        grid=(m // tm,),
        in_specs=[row_spec, pl.BlockSpec((None, d, d), lambda i: (layer, 0, 0)), row_spec,
                  pl.BlockSpec((1, d), lambda i: (0, 0))],
        out_specs=out_specs,
        out_shape=out_shape,
        compiler_params=_params("parallel"),
        name="out_proj",
    )(merged, w_out, x2d, gain.reshape(1, d))


def _layer_vectors(l, pool_scale, rwkv_mu, rwkv_w0, rwkv_w_up, rwkv_a0, rwkv_a_up, rwkv_k_k, rwkv_k_a, rwkv_r_k,
                   rwkv_gn_w, rwkv_gn_b, attn_sink):
    w_lora = jnp.zeros((LANES, 2 * BRANCH_WIDTH), F32)
    w_lora = w_lora.at[:LORA_RANK, :BRANCH_WIDTH].set(rwkv_w_up[l]).at[LORA_RANK:, BRANCH_WIDTH:].set(rwkv_a_up[l])
    vecs = jnp.stack([rwkv_w0[l], rwkv_a0[l], rwkv_k_k[l], rwkv_k_a[l], rwkv_r_k[l], rwkv_gn_w[l], rwkv_gn_b[l],
                      jnp.zeros_like(rwkv_w0[l])])
    return dict(pool_scale=pool_scale[l].reshape(1, -1), mu=rwkv_mu[l].reshape(1, -1), vecs=vecs,
                w_lora=w_lora.astype(BF16), sink=attn_sink[l][np.array(ATT_HEAD_ORDER)])


def _sink_rows(sink_ordered, n_q):
    rows = jnp.repeat(sink_ordered.reshape(-1, 4), n_q, axis=1)
    return jnp.broadcast_to(rows[..., None], rows.shape + (LANES,))


def _mixer_layer(x2, h, b, l, wts, lv, next_gain, final_norm, *, pos0, pool_hist16, shift_prev, s0_bd, bias, caches):
    t = x2.shape[0] // b
    proj ={name: _matmul(h, wts["w_in"], l, col0, n, tn=tn, act=act, out_dtype=dt, name="proj_" + name)
            for name, (col0, n, tn, act, dt) in PROJ_SEGMENTS.items()}
    u_pool, p_shift, q, kv, gate = (proj[name].reshape(b, t, -1) for name in ("pool", "shift", "q", "kv", "gate"))
    mg = proj["merge"]

    ga = _pool_branch(u_pool, pool_hist16, gate, wts["pool_w"], l, lv["pool_scale"], pos0)

    t_pad = -(-t // CHUNK) * CHUNK
    if t_pad != t:
        p_in = jnp.pad(p_shift, ((0, 0), (0, t_pad - t), (0, 0)))
        gate_in = jnp.pad(gate, ((0, 0), (0, t_pad - t), (0, 0)))
    else:
        p_in, gate_in = p_shift, gate
    gb, s_bd = _rwkv_branch(p_in, shift_prev, s0_bd, gate_in, lv["mu"], lv["vecs"], lv["w_lora"],
                            n_valid=CHUNK - (t_pad - t))
    gb = gb[:, :t]

    if caches is None:
        gc = _attn_prompt(q, kv, bias, _sink_rows(lv["sink"], CHUNK), gate)
    else:
        gc = _attn_sample(q, kv, caches[0], caches[1], bias, _sink_rows(lv["sink"], t), gate)

    merged = _merge(ga.reshape(b * t, -1), gb.reshape(b * t, -1), gc.reshape(b * t, -1), wts["w_branch"], l, mg)
    outs = _out_proj(merged, wts["w_out"], l, x2, next_gain, final_norm)
    y, h_next = (outs[0], None) if final_norm else outs
    return (y, h_next, kv[..., :KV_WIDTH], kv[..., KV_WIDTH:], _state_from_blockdiag(s_bd), p_shift[:, -1],
            u_pool[:, t - POOL_HIST:])


def kernel(x_prompt, x_sample, cache_attn_k, cache_attn_v, state_rwkv, state_rwkv_shift, state_pool, norm_g, w_in,
           pool_w, pool_scale, rwkv_mu, rwkv_w0, rwkv_w_up, rwkv_a0, rwkv_a_up, rwkv_k_k, rwkv_k_a, rwkv_r_k,
           rwkv_gn_w, rwkv_gn_b, attn_sink, rel_pos_table, w_branch, w_out, final_norm_g):
    depth = norm_g.shape[0]
    bp, tp, _ = x_prompt.shape
    bs, ts, _ = x_sample.shape
    n_cache = cache_attn_k.shape[2]

    key_p = jnp.arange((BAND_CHUNKS + 1) * CHUNK) - BAND_CHUNKS * CHUNK
    bias_p = _rel_bias(rel_pos_table, key_p, CHUNK)
    key_s = jnp.concatenate([jnp.arange(n_cache) - n_cache, jnp.arange(ts)])
    bias_s = _rel_bias(rel_pos_table, key_s, ts)

    wts = dict(w_in=w_in.astype(BF16), pool_w=pool_w.astype(BF16), w_branch=w_branch.astype(BF16),
               w_out=w_out.astype(BF16))
    xp, xs = x_prompt.reshape(bp * tp, -1), x_sample.reshape(bs * ts, -1)
    hp = _rmsnorm(xp, norm_g[0], BF16, PROJ_TM)
    hs = _rmsnorm(xs, norm_g[0], BF16, PROJ_TM)
    outs_p, outs_s = [], []
    for l in range(depth):
        lv = _layer_vectors(l, pool_scale, rwkv_mu, rwkv_w0, rwkv_w_up, rwkv_a0, rwkv_a_up, rwkv_k_k, rwkv_k_a,
                            rwkv_r_k, rwkv_gn_w, rwkv_gn_b, attn_sink)
        last = l == depth - 1
        next_gain = final_norm_g if last else norm_g[l + 1]
        xp, hp, kp, vp, sp, shp, plp = _mixer_layer(
            xp, hp, bp, l, wts, lv, next_gain, last, pos0=0,
            pool_hist16=jnp.zeros((bp, 16, BRANCH_WIDTH), F32),
            shift_prev=jnp.zeros((bp, 1, SHIFT_WIDTH), F32),
            s0_bd=jnp.zeros((bp, PAIRS, LANES, LANES), F32),
            bias=bias_p, caches=None)
        outs_p.append((kp[:, -WINDOW:].reshape(bp, WINDOW, ATT_KV_HEADS, HEAD_DIM),
                       vp[:, -WINDOW:].reshape(bp, WINDOW, ATT_KV_HEADS, HEAD_DIM), sp, shp, plp))
        xs, hs, k_s, v_s, s_s, sh_s, pl_s = _mixer_layer(
            xs, hs, bs, l, wts, lv, next_gain, last, pos0=PAST_LEN,
            pool_hist16=jnp.pad(state_pool[l], ((0, 0), (1, 0), (0, 0))),
            shift_prev=state_rwkv_shift[l][:, None, :],
            s0_bd=_state_to_blockdiag(state_rwkv[l]),
            bias=bias_s,
            caches=(cache_attn_k[l].reshape(bs, n_cache, KV_WIDTH), cache_attn_v[l].reshape(bs, n_cache, KV_WIDTH)))
        outs_s.append((k_s.reshape(bs, ts, ATT_KV_HEADS, HEAD_DIM), v_s.reshape(bs, ts, ATT_KV_HEADS, HEAD_DIM),
                       s_s, sh_s, pl_s))

    st = lambda outs, i: jnp.stack([o[i] for o in outs])
    return (xp.reshape(x_prompt.shape), xs.reshape(x_sample.shape), st(outs_p, 0), st(outs_p, 1), st(outs_s, 0), st(outs_s, 1),
            st(outs_p, 2), st(outs_s, 2), st(outs_p, 3), st(outs_s, 3), st(outs_p, 4), st(outs_s, 4))
```

```python
import functools
import math

import jax
import jax.numpy as jnp
import numpy as np
from jax import lax
from jax.experimental import pallas as pl
from jax.experimental.pallas import tpu as pltpu

F32 = jnp.float32
BF16 = jnp.bfloat16

D_MODEL = 2048
BRANCH_WIDTH = 1024
POOL_WINDOWS = (2, 4, 8, 16)
POOL_GROUP_DIM = BRANCH_WIDTH // len(POOL_WINDOWS)
POOL_HIST = max(POOL_WINDOWS) - 1
HEAD_DIM = 64
RWKV_HEADS = BRANCH_WIDTH // HEAD_DIM
LORA_RANK = 64
SHIFT_WIDTH = 3 * BRANCH_WIDTH + 2 * LORA_RANK
GN_EPS = 64e-5
ATT_HEADS = 16
ATT_KV_HEADS = 4
ATT_GROUP = ATT_HEADS // ATT_KV_HEADS
KV_WIDTH = ATT_KV_HEADS * HEAD_DIM
CHUNK = 64
WINDOW = 128
BAND_CHUNKS = WINDOW // CHUNK
N_BUCKETS = 32
MAX_DISTANCE = 128
NEG_INF = -1e30
RMS_EPS = 1e-6
PAST_LEN = 2048
N_BRANCH = 3
OFF_POOL = 0
OFF_SHIFT = OFF_POOL + BRANCH_WIDTH
OFF_Q = OFF_SHIFT + SHIFT_WIDTH
OFF_K = OFF_Q + BRANCH_WIDTH
OFF_V = OFF_K + KV_WIDTH
OFF_GATE = OFF_V + KV_WIDTH
OFF_MERGE = OFF_GATE + N_BRANCH * BRANCH_WIDTH

LANES = 128
PAIRS = BRANCH_WIDTH // LANES
ATTN_CHUNKS_PER_STEP = 4
RWKV_CHUNKS_PER_STEP = 4
RWKV_GROUP = 8
KEY_PAD = 256
ATT_HEAD_ORDER = tuple(8 * m + o for m in range(ATT_KV_HEADS // 2) for o in (0, 2, 5, 7, 1, 3, 4, 6))
PROJ_TM = 512
MERGE_TN = 1024
PROJ_SEGMENTS = {
    "pool": (OFF_POOL, BRANCH_WIDTH, 1024, None, F32),
    "shift": (OFF_SHIFT, SHIFT_WIDTH, SHIFT_WIDTH, None, F32),
    "q": (OFF_Q, BRANCH_WIDTH, 1024, "att_scale", BF16),
    "kv": (OFF_K, 2 * KV_WIDTH, 2 * KV_WIDTH, None, F32),
    "gate": (OFF_GATE, N_BRANCH * BRANCH_WIDTH, 3072, "silu", BF16),
    "merge": (OFF_MERGE, N_BRANCH * D_MODEL, 3072, "sigmoid", BF16),
}
VMEM_LIMIT = 56 * 1024 * 1024


def _params(*sem):
    return pltpu.CompilerParams(dimension_semantics=sem, vmem_limit_bytes=VMEM_LIMIT)


def _sigmoid(x):
    return 1.0 / (1.0 + jnp.exp(-x))


def _dot(a, b):
    return jnp.dot(a, b, preferred_element_type=F32)


def _dot_nt(a, b):
    return lax.dot_general(a, b, (((1,), (1,)), ((), ())), preferred_element_type=F32)


def _split3(x):
    hi = x.astype(BF16)
    r1 = x - hi.astype(F32)
    mid = r1.astype(BF16)
    lo = (r1 - mid.astype(F32)).astype(BF16)
    return hi, mid, lo


def _dot3_right_exact(x, b):
    hi, mid, lo = _split3(x)
    return _dot(hi, b) + _dot(mid, b) + _dot(lo, b)


def _dot3_left_exact(a, x):
    hi, mid, lo = _split3(x)
    return _dot(a, hi) + _dot(a, mid) + _dot(a, lo)


def _rmsnorm_kernel(x_ref, g_ref, o_ref):
    x = x_ref[...]
    ms = jnp.mean(x * x, axis=-1, keepdims=True)
    o_ref[...] = (x * lax.rsqrt(ms + RMS_EPS) * g_ref[...]).astype(o_ref.dtype)


def _rmsnorm(x2d, g, out_dtype, tm):
    m, d = x2d.shape
    tm = min(tm, m)
    return pl.pallas_call(
        _rmsnorm_kernel,
        grid=(m // tm,),
        in_specs=[pl.BlockSpec((tm, d), lambda i: (i, 0)), pl.BlockSpec((1, d), lambda i: (0, 0))],
        out_specs=pl.BlockSpec((tm, d), lambda i: (i, 0)),
        out_shape=jax.ShapeDtypeStruct((m, d), out_dtype),
        compiler_params=_params("parallel"),
        name="rmsnorm",
    )(x2d, g.reshape(1, d))


def _mm_kernel(a_ref, w_ref, o_ref, *, act):
    acc = _dot(a_ref[...], w_ref[0])
    if act == "silu":
        acc = acc * _sigmoid(acc)
    elif act == "sigmoid":
        acc = _sigmoid(acc)
    elif act == "att_scale":
        acc = acc * (HEAD_DIM ** -0.5)
    o_ref[...] = acc.astype(o_ref.dtype)


def _matmul(a, w, layer, col0, n, *, tn, act, out_dtype, name):
    m, k = a.shape
    tm = min(PROJ_TM, m)
    return pl.pallas_call(
        functools.partial(_mm_kernel, act=act),
        grid=(n // tn, m // tm),
        in_specs=[pl.BlockSpec((tm, k), lambda j, i: (i, 0)),
                  pl.BlockSpec((pl.Element(1), pl.Element(k), pl.Element(tn)),
                               lambda j, i: (layer, 0, pl.multiple_of(col0 + j * tn, LANES)))],
        out_specs=pl.BlockSpec((tm, tn), lambda j, i: (i, j)),
        out_shape=jax.ShapeDtypeStruct((m, n), out_dtype),
        compiler_params=_params("arbitrary", "arbitrary"),
        name=name,
    )(a, w)


def _pool_kernel(u_ref, prev_ref, hist_ref, zg_ref, w_ref, scale_ref, o_ref, *, tt, pos0):
    i = pl.program_id(1)
    u = u_ref[0]
    prev = jnp.where(i == 0, hist_ref[0], prev_ref[0])
    pos = pos0 + i * tt + lax.broadcasted_iota(jnp.int32, (tt, POOL_GROUP_DIM), 0)
    for g, w in enumerate(POOL_WINDOWS):
        sl = slice(g * POOL_GROUP_DIM, (g + 1) * POOL_GROUP_DIM)
        e = jnp.concatenate([prev[:, sl], u[:, sl]], axis=0)
        step = 1
        while step < w:
            e = e + pltpu.roll(e, step, 0)
            step *= 2
        s = e[16:, :]
        cnt = jnp.minimum(w, pos + 1).astype(F32)
        pooled = s / cnt - u[:, sl]
        o = _dot(pooled.astype(BF16), w_ref[g]) * scale_ref[:, sl]
        o_ref[0, :, sl] = (o * zg_ref[0, :, sl].astype(F32)).astype(o_ref.dtype)


def _pool_branch(u, hist16, gate, pool_w, layer, pool_scale, pos0):
    b, t, c = u.shape
    tt = min(t, 512)
    sub = tt // 16
    return pl.pallas_call(
        functools.partial(_pool_kernel, tt=tt, pos0=pos0),
        grid=(b, t // tt),
        in_specs=[
            pl.BlockSpec((1, tt, c), lambda bi, i: (bi, i, 0)),
            pl.BlockSpec((1, 16, c), lambda bi, i: (bi, jnp.maximum(i * sub - 1, 0), 0)),
            pl.BlockSpec((1, 16, c), lambda bi, i: (bi, 0, 0)),
            pl.BlockSpec((1, tt, c), lambda bi, i: (bi, i, 0)),
            pl.BlockSpec((None,) + pool_w.shape[1:], lambda bi, i: (layer, 0, 0, 0)),
            pl.BlockSpec((1, c), lambda bi, i: (0, 0)),
        ],
        out_specs=pl.BlockSpec((1, tt, c), lambda bi, i: (bi, i, 0)),
        out_shape=jax.ShapeDtypeStruct((b, t, c), BF16),
        compiler_params=_params("parallel", "parallel"),
        name="pool_branch",
    )(u, u, hist16, gate, pool_w, pool_scale)


def _rwkv_kernel(p_ref, sprev_ref, s0_ref, zg_ref, mu_ref, vec_ref, wl_ref, o_ref, s_ref, carry_ref,
                 *, n_valid, n_sub):
    @pl.when(pl.program_id(1) == 0)
    def _():
        s_ref[...] = s0_ref[...]
        carry_ref[...] = sprev_ref[0]

    for sub in range(n_sub):
        rows = pl.ds(sub * CHUNK, CHUNK)
        _rwkv_chunk(p_ref.at[:, rows, :], zg_ref.at[:, rows, :], mu_ref, vec_ref, wl_ref, o_ref.at[:, rows, :],
                    s_ref, carry_ref, n_valid=n_valid)


def _rwkv_chunk(p_ref, zg_ref, mu_ref, vec_ref, wl_ref, o_ref, s_ref, carry_ref, *, n_valid):
    C = CHUNK
    p = p_ref[0]
    row_w = lax.broadcasted_iota(jnp.int32, (C, SHIFT_WIDTH), 0)
    row = lax.broadcasted_iota(jnp.int32, (C, BRANCH_WIDTH), 0)
    prev = jnp.where(row_w == 0, carry_ref[...], pltpu.roll(p, 1, 0))
    carry_ref[...] = p[C - 1:C, :]
    xs = p + (prev - p) * mu_ref[...]

    w0, a0, k_k, k_a, r_k, gn_w, gn_b = (vec_ref[i:i + 1, :] for i in range(7))
    lane = lax.broadcasted_iota(jnp.int32, (C, LANES), 1)
    head0 = lane < HEAD_DIM
    lin = xs[:, 3 * BRANCH_WIDTH:]
    lin = jnp.where(head0, jnp.tanh(lin), lin)
    lora = _dot(lin.astype(BF16), wl_ref[...])
    log_decay = -math.exp(-0.5) * _sigmoid(w0 + lora[:, :BRANCH_WIDTH])
    iclr = _sigmoid(a0 + lora[:, BRANCH_WIDTH:])
    r = xs[:, :BRANCH_WIDTH]
    k = xs[:, BRANCH_WIDTH:2 * BRANCH_WIDTH]
    v = xs[:, 2 * BRANCH_WIDTH:3 * BRANCH_WIDTH]
    kk_raw = k * k_k
    k2 = k * (1.0 + (iclr - 1.0) * k_a)
    if n_valid < C:
        ok = row < n_valid
        log_decay = jnp.where(ok, log_decay, 0.0)
        kk_raw = jnp.where(ok, kk_raw, 0.0)
        k2 = jnp.where(ok, k2, 0.0)
        r = jnp.where(ok, r, 0.0)
        v = jnp.where(ok, v, 0.0)

    ti = lax.broadcasted_iota(jnp.int32, (C, C), 0)
    si = lax.broadcasted_iota(jnp.int32, (C, C), 1)
    tri = (si <= ti).astype(BF16)
    cum = _dot3_left_exact(tri, log_decay)
    cum_x = cum - log_decay
    cum_c = cum[C - 1:C, :]

    ri =lax.broadcasted_iota(jnp.int32, (2 * C, 2 * C), 0)
    ci = lax.broadcasted_iota(jnp.int32, (2 * C, 2 * C), 1)
    same_blk = (ri >= C) == (ci >= C)
    strict = same_blk & ((ci & (C - 1)) < (ri & (C - 1)))
    incl = same_blk & ((ci & (C - 1)) <= (ri & (C - 1)))

    def stack(x):
        return jnp.concatenate([jnp.where(head0, x, 0.0), jnp.where(head0, 0.0, x)], axis=0)

    def each(f, *cols):
        return [f(*xs) for xs in zip(*cols)]

    def bf(xs):
        return [x.astype(BF16) for x in xs]

    def head_sums(xs):
        lo = [jnp.sum(jnp.where(head0, x, 0.0), axis=-1, keepdims=True) for x in xs]
        hi = [jnp.sum(jnp.where(head0, 0.0, x), axis=-1, keepdims=True) for x in xs]
        return [jnp.where(head0, a, b) for a, b in zip(lo, hi)]

    def stack_b(x):
        xb = x.astype(BF16)
        zero = jnp.zeros((), BF16)
        return jnp.concatenate([jnp.where(head0, xb, zero), jnp.where(head0, zero, xb)], axis=0)

    def unfold(x, keep):
        xr = pltpu.roll(x, C, 1)
        top, bot = keep[:C], keep[C:]
        from_p = jnp.concatenate([jnp.where(top, x[:C], 0.0), jnp.where(bot, xr[C:], 0.0)], axis=0)
        from_q = jnp.concatenate([jnp.where(top, xr[:C], 0.0), jnp.where(bot, x[C:], 0.0)], axis=0)
        return from_p, from_q

    for g0 in range(0, PAIRS, RWKV_GROUP):
        prs = list(range(g0, g0 + RWKV_GROUP))
        sls = [slice(pr * LANES, (pr + 1) * LANES) for pr in prs]
        r_p = [r[:, sl] for sl in sls]
        k_p = [k2[:, sl] for sl in sls]
        v_p = [v[:, sl] for sl in sls]
        kkr = [kk_raw[:, sl] for sl in sls]
        ic = [iclr[:, sl] for sl in sls]
        cm = [cum[:, sl] for sl in sls]
        cx = [cum_x[:, sl] for sl in sls]
        cc = [cum_c[:, sl] for sl in sls]
        n2 = head_sums(each(lambda x: x * x, kkr))
        kk = each(lambda x, n: x * lax.rsqrt(jnp.maximum(n, 1e-24)), kkr, n2)
        b_p = each(lambda x, i: x * i, kk, ic)
        e_neg = each(lambda c_: jnp.exp(-c_), cm)
        e_end = each(lambda c_, e_: jnp.exp(e_ - c_), cm, cc)
        w_c = each(jnp.exp, cc)
        rt = each(lambda x, c_: x * jnp.exp(c_), r_p, cm)
        at_b = each(lambda x, c_: stack_b(-x * jnp.exp(c_)), kk, cx)
        rt_b = each(stack_b, rt)
        v_s = each(stack, v_p)
        v_b = bf(v_s)
        bb_b = each(lambda x, e_: stack_b(x * e_), b_p, e_end)
        kb_b = each(lambda x, e_: stack_b(x * e_), k_p, e_end)
        bk_b = bf(each(lambda b_, k_, e_: jnp.concatenate([b_ * e_, k_ * e_], axis=0), b_p, k_p, e_neg))
        ar_b = each(lambda a_, r_: jnp.concatenate([a_, r_], axis=0), at_b, rt_b)
        prod = each(_dot_nt, ar_b, bk_b)
        a_pair = each(lambda x: unfold(x[:2 * C], strict), prod)
        r_pair = each(lambda x: unfold(x[2 * C:], incl), prod)
        a_ab = [x[0] for x in a_pair]
        a_ak_b = bf([x[1] for x in a_pair])
        a_rb_b = bf([x[0] for x in r_pair])
        a_rk_b = bf([x[1] for x in r_pair])
        tinv = each(lambda x: jnp.where(ri == ci, 1.0, x), a_ab)
        pw_b = bf(a_ab)
        pw_b = bf(each(_dot, pw_b, pw_b))
        n_sq = int(math.log2(C)) - 1
        for step in range(n_sq):
            if step < n_sq - 1:
                both = each(lambda t_, p_: _dot(p_, jnp.concatenate([t_.astype(BF16), p_], axis=1)), tinv, pw_b)
                tinv = each(lambda t_, x: t_ + x[:, :2 * C], tinv, both)
                pw_b = bf([x[:, 2 * C:] for x in both])
            else:
                tinv = each(lambda t_, p_: t_ + _dot(p_, t_.astype(BF16)), tinv, pw_b)
        t_b = bf(tinv)
        akv = each(_dot, a_ak_b, v_b)
        au = each(lambda t_, a_, k_: _dot(t_, jnp.concatenate([a_, k_.astype(BF16)], axis=1)), t_b, at_b, akv)
        ry = each(_dot, a_rb_b, bf(au))
        rh = each(lambda x, y_: x + y_[:C, :LANES] + y_[C:, :LANES], rt, ry)
        y0 = each(lambda y_, k_, v_: y_[:, LANES:] + _dot(k_, v_), ry, a_rk_b, v_b)
        mg = each(lambda x, b_: _dot(x.T.astype(BF16), b_), au, bb_b)
        m_lr = [x[:LANES] for x in mg]
        g_bd = each(lambda x, v_, k_: x[LANES:] + _dot(v_.T.astype(BF16), k_), mg, v_s, kb_b)
        s_bd = [s_ref[0, pr] for pr in prs]
        s_b = bf(s_bd)
        y = each(lambda x, s_, y_: _dot_nt(x.astype(BF16), s_) + (y_[:C] + y_[C:]), rh, s_b, y0)
        s_new = each(lambda s_, w_, sb_, m_, g_: s_ * w_ + _dot(sb_, m_.astype(BF16)) + g_,
                     s_bd, w_c, s_b, m_lr, g_bd)
        for pr, s_ in zip(prs, s_new):
            s_ref[0, pr] = s_
        yc = each(lambda x, m_: x - m_ * (1.0 / HEAD_DIM), y, head_sums(y))
        var = head_sums(each(lambda x: x * x, yc))
        rk = head_sums(each(lambda r_, k_, sl: r_ * k_ * r_k[:, sl], r_p, k_p, sls))
        for sl, yc_, var_, rk_, v_ in zip(sls, yc, var, rk, v_p):
            yn = yc_ * lax.rsqrt(var_ * (1.0 / HEAD_DIM) + GN_EPS) * gn_w[:, sl] + gn_b[:, sl]
            o_ref[0, :, sl] = ((yn + rk_ * v_) * zg_ref[0, :, sl].astype(F32)).astype(o_ref.dtype)


def _rwkv_branch(p_shift, shift_prev, s0_bd, gate, mu, vecs, w_lora, n_valid):
    b, t, _ = p_shift.shape
    n_sub = math.gcd(t // CHUNK, RWKV_CHUNKS_PER_STEP)
    rows = n_sub * CHUNK
    return pl.pallas_call(
        functools.partial(_rwkv_kernel, n_valid=n_valid, n_sub=n_sub),
        grid=(b, t // rows),
        in_specs=[
            pl.BlockSpec((1, rows, SHIFT_WIDTH), lambda bi, c: (bi, c, 0)),
            pl.BlockSpec((1, 1, SHIFT_WIDTH), lambda bi, c: (bi, 0, 0)),
            pl.BlockSpec((1, PAIRS, LANES, LANES), lambda bi, c: (bi, 0, 0, 0)),
            pl.BlockSpec((1, rows, BRANCH_WIDTH), lambda bi, c: (bi, c, 1)),
            pl.BlockSpec((1, SHIFT_WIDTH), lambda bi, c: (0, 0)),
            pl.BlockSpec((8, BRANCH_WIDTH), lambda bi, c: (0, 0)),
            pl.BlockSpec((LANES, 2 * BRANCH_WIDTH), lambda bi, c: (0, 0)),
        ],
        out_specs=[
            pl.BlockSpec((1, rows, BRANCH_WIDTH), lambda bi, c: (bi, c, 0)),
            pl.BlockSpec((1, PAIRS, LANES, LANES), lambda bi, c: (bi, 0, 0, 0)),
        ],
        out_shape=[
            jax.ShapeDtypeStruct((b, t, BRANCH_WIDTH), BF16),
            jax.ShapeDtypeStruct((b, PAIRS, LANES, LANES), F32),
        ],
        scratch_shapes=[pltpu.VMEM((1, SHIFT_WIDTH), F32)],
        compiler_params=_params("parallel", "arbitrary"),
        name="rwkv_branch",
    )(p_shift, shift_prev, s0_bd, gate, mu, vecs, w_lora)


def _state_to_blockdiag(s):
    b = s.shape[0]
    sp = s.reshape(b, PAIRS, 2, HEAD_DIM, HEAD_DIM)
    z = jnp.zeros_like(sp[:, :, 0])
    top = jnp.concatenate([sp[:, :, 0], z], axis=-1)
    bot = jnp.concatenate([z, sp[:, :, 1]], axis=-1)
    return jnp.concatenate([top, bot], axis=-2)


def _state_from_blockdiag(s_bd):
    b = s_bd.shape[0]
    h0 = s_bd[:, :, :HEAD_DIM, :HEAD_DIM]
    h1 = s_bd[:, :, HEAD_DIM:, HEAD_DIM:]
    return jnp.stack([h0, h1], axis=2).reshape(b, RWKV_HEADS, HEAD_DIM, HEAD_DIM)


def _attn_kernel(*refs, n_kv_parts, tq, n_keys, masked, n_sub):
    q_ref = refs[0]
    k_refs = refs[1:1 + n_kv_parts]
    v_refs = refs[1 + n_kv_parts:1 + 2 * n_kv_parts]
    bias_ref, sink_ref, zg_ref, o_ref = refs[1 + 2 * n_kv_parts:]
    k_rows = jnp.concatenate([kr[0] for kr in k_refs], axis=0)
    v_rows = jnp.concatenate([vr[0] for vr in v_refs], axis=0)
    for sub in range(n_sub):
        rows = pl.ds(sub * tq, tq)
        _attn_chunk(q_ref.at[:, rows, :], k_rows[sub * tq:sub * tq + n_keys], v_rows[sub * tq:sub * tq + n_keys],
                    bias_ref, sink_ref, zg_ref.at[:, rows, :], o_ref.at[:, rows, :],
                    n=pl.program_id(1) * n_sub + sub, tq=tq, n_keys=n_keys, masked=masked)


def _attn_chunk(q_ref, k_seen, v_seen, bias_ref, sink_ref, zg_ref, o_ref, *, n, tq, n_keys, masked):
    pad = jnp.zeros((KEY_PAD - n_keys, KV_WIDTH), F32)
    k_all = jnp.concatenate([k_seen, pad], axis=0)
    v_all = jnp.concatenate([v_seen, pad], axis=0)
    lane =lax.broadcasted_iota(jnp.int32, (tq, LANES), 1)
    head0 = lane < HEAD_DIM
    col = lax.broadcasted_iota(jnp.int32, (1, KEY_PAD), 1)
    valid = col < n_keys
    if masked:
        valid = valid & (col // CHUNK + n - BAND_CHUNKS >= 0)
    q = q_ref[0]
    zero = jnp.zeros((), q.dtype)
    lo_half = lambda s: jnp.where(head0, s, zero)
    hi_half = lambda s: jnp.where(head0, zero, s)
    lhs, keys, vals = [], [], []
    for m in range(ATT_KV_HEADS // 2):
        slabs = [q[:, (4 * m + i) * LANES:(4 * m + i + 1) * LANES] for i in range(4)]
        k_sl = k_all[:, m * LANES:(m + 1) * LANES]
        v_sl = v_all[:, m * LANES:(m + 1) * LANES]
        lhs.append(jnp.concatenate([lo_half(slabs[0]), lo_half(slabs[1]), hi_half(slabs[2]), hi_half(slabs[3])], 0))
        lhs.append(jnp.concatenate([hi_half(slabs[0]), hi_half(slabs[1]), lo_half(slabs[2]), lo_half(slabs[3])], 0))
        keys += [k_sl.astype(BF16), pltpu.roll(k_sl, HEAD_DIM, 1).astype(BF16)]
        vals += [v_sl.astype(BF16), pltpu.roll(v_sl, HEAD_DIM, 1).astype(BF16)]
    n_grp = len(lhs)
    ones = jnp.ones((KEY_PAD, LANES), BF16)
    vals = [jnp.concatenate([x, ones], axis=1) for x in vals]
    logits = [jnp.where(valid, _dot_nt(lhs[i], keys[i]) + bias_ref[i], NEG_INF) for i in range(n_grp)]
    sink = [sink_ref[i] for i in range(n_grp)]
    mx = [jnp.maximum(jnp.max(logits[i], axis=-1, keepdims=True), sink[i]) for i in range(n_grp)]
    e = [jnp.exp(jnp.concatenate([logits[i][:, :LANES] - mx[i], logits[i][:, LANES:] - mx[i]], axis=1))
         for i in range(n_grp)]
    pv = [_dot(e[i].astype(BF16), vals[i]) for i in range(n_grp)]
    out = [pv[i][:, :LANES] / (pv[i][:, LANES:] + jnp.exp(sink[i] - mx[i])) for i in range(n_grp)]
    for m in range(ATT_KV_HEADS // 2):
        same, swap = out[2 * m], out[2 * m + 1]
        for i in range(4):
            a, b = same[i * tq:(i + 1) * tq], swap[i * tq:(i + 1) * tq]
            res = jnp.where(head0, a, b) if i < 2 else jnp.where(head0, b, a)
            sl = slice((4 * m + i) * LANES, (4 * m + i + 1) * LANES)
            o_ref[0, :, sl] = (res * zg_ref[0, :, sl].astype(F32)).astype(o_ref.dtype)


def _attn_prompt(q, kv, bias, sink_rows, gate):
    b, t, _ = q.shape
    nc = t // CHUNK
    n_sub = math.gcd(nc, ATTN_CHUNKS_PER_STEP)
    rows = n_sub * CHUNK
    k_specs, v_specs = ([pl.BlockSpec((1, CHUNK, KV_WIDTH),
                                      lambda bi, n, d=d, c=c: (bi, jnp.maximum(n * n_sub - d, 0), c))
                         for d in range(BAND_CHUNKS, 0, -1)]
                        + [pl.BlockSpec((1, rows, KV_WIDTH), lambda bi, n, c=c: (bi, n, c))] for c in range(2))
    return pl.pallas_call(
        functools.partial(_attn_kernel, n_kv_parts=BAND_CHUNKS + 1, tq=CHUNK,
                          n_keys=(BAND_CHUNKS + 1) * CHUNK, masked=True, n_sub=n_sub),
        grid=(b, nc // n_sub),
        in_specs=[pl.BlockSpec((1, rows, BRANCH_WIDTH), lambda bi, n: (bi, n, 0))]
        + k_specs + v_specs
        + [pl.BlockSpec(bias.shape, lambda bi, n: (0, 0, 0)),
           pl.BlockSpec(sink_rows.shape, lambda bi, n: (0, 0, 0)),
           pl.BlockSpec((1, rows, BRANCH_WIDTH), lambda bi, n: (bi, n, 2))],
        out_specs=pl.BlockSpec((1, rows, BRANCH_WIDTH), lambda bi, n: (bi, n, 0)),
        out_shape=jax.ShapeDtypeStruct((b, t, BRANCH_WIDTH), BF16),
        compiler_params=_params("parallel", "parallel"),
        name="attn_prompt",
    )(q, kv, kv, kv, kv, kv, kv, bias, sink_rows, gate)


def _attn_sample(q, kv, k_cache, v_cache, bias, sink_rows, gate):
    b, t, _ = q.shape
    n_cache = k_cache.shape[1]
    cache_spec = pl.BlockSpec((1, n_cache, KV_WIDTH), lambda bi, n: (bi, 0, 0))
    return pl.pallas_call(
        functools.partial(_attn_kernel, n_kv_parts=2, tq=t, n_keys=n_cache + t, masked=False, n_sub=1),
        grid=(b, 1),
        in_specs=[pl.BlockSpec((1, t, BRANCH_WIDTH), lambda bi, n: (bi, 0, 0)),
                  cache_spec, pl.BlockSpec((1, t, KV_WIDTH), lambda bi, n: (bi, 0, 0)),
                  cache_spec, pl.BlockSpec((1, t, KV_WIDTH), lambda bi, n: (bi, 0, 1)),
                  pl.BlockSpec(bias.shape, lambda bi, n: (0, 0, 0)),
                  pl.BlockSpec(sink_rows.shape, lambda bi, n: (0, 0, 0)),
                  pl.BlockSpec((1, t, BRANCH_WIDTH), lambda bi, n: (bi, 0, 2))],
        out_specs=pl.BlockSpec((1, t, BRANCH_WIDTH), lambda bi, n: (bi, 0, 0)),
        out_shape=jax.ShapeDtypeStruct((b, t, BRANCH_WIDTH), BF16),
        compiler_params=_params("parallel", "parallel"),
        name="attn_sample",
    )(q, k_cache, kv, v_cache, kv, bias, sink_rows, gate)


def _t5_bucket(rel):
    half = N_BUCKETS // 2
    n = -rel
    ret = jnp.where(n < 0, half, 0)
    n = jnp.abs(n)
    max_exact = half // 2
    large = max_exact + (jnp.log(jnp.maximum(n, 1).astype(jnp.float32) / max_exact)
                         / math.log(MAX_DISTANCE / max_exact) * (half - max_exact)).astype(jnp.int32)
    large = jnp.minimum(large, half - 1)
    return ret + jnp.where(n < max_exact, n, large)


def _bias_kernel(tab_ref, onehot_ref, o_ref):
    o_ref[...] = _dot3_right_exact(tab_ref[...], onehot_ref[...])


def _rel_bias(table, key_pos, n_q):
    n_keys = key_pos.shape[0]
    bucket = _t5_bucket(key_pos[None, :] - jnp.arange(n_q)[:, None])
    bucket = jnp.pad(bucket, ((0, 0), (0, KEY_PAD - n_keys)), constant_values=-1)
    onehot = (bucket.reshape(1, -1) == jnp.arange(N_BUCKETS)[:, None]).astype(BF16)
    out = pl.pallas_call(
        _bias_kernel,
        out_shape=jax.ShapeDtypeStruct((ATT_HEADS, n_q * KEY_PAD), F32),
        name="rel_bias",
    )(table.T[np.array(ATT_HEAD_ORDER)], onehot)
    return out.reshape(len(ATT_HEAD_ORDER) // 4, 4 * n_q, KEY_PAD)


def _merge_kernel(ga_ref, gb_ref, gc_ref, w_ref, m0_ref, m1_ref, m2_ref, o_ref):
    acc = m0_ref[...].astype(F32) * _dot(ga_ref[...], w_ref[0])
    acc = acc + m1_ref[...].astype(F32) * _dot(gb_ref[...], w_ref[1])
    acc = acc + m2_ref[...].astype(F32) * _dot(gc_ref[...], w_ref[2])
    o_ref[...] = acc.astype(o_ref.dtype)


def _merge(ga, gb, gc, w_branch, layer, mg):
    m, kdim = ga.shape
    tm = min(PROJ_TM, m)
    tn = MERGE_TN
    nj = D_MODEL // tn
    g_spec = pl.BlockSpec((tm, kdim), lambda j, i: (i, 0))
    m_specs = [pl.BlockSpec((tm, tn), lambda j, i, br=br: (i, br * nj + j)) for br in range(N_BRANCH)]
    return pl.pallas_call(
        _merge_kernel,
        grid=(nj, m // tm),
        in_specs=[g_spec, g_spec, g_spec,
                  pl.BlockSpec((None, N_BRANCH, kdim, tn), lambda j, i: (layer, 0, 0, j))] + m_specs,
        out_specs=pl.BlockSpec((tm, tn), lambda j, i: (i, j)),
        out_shape=jax.ShapeDtypeStruct((m, D_MODEL), BF16),
        compiler_params=_params("arbitrary", "arbitrary"),
        name="merge",
    )(ga, gb, gc, w_branch, mg, mg, mg)


def _out_kernel(a_ref, w_ref, x_ref, g_ref, *o_refs, final_norm):
    y = x_ref[...] + _dot(a_ref[...], w_ref[...])
    ms = jnp.mean(y * y, axis=-1, keepdims=True)
    normed = y * lax.rsqrt(ms + RMS_EPS) * g_ref[...]
    if final_norm:
        o_refs[0][...] = normed
    else:
        o_refs[0][...] = y
        o_refs[1][...] = normed.astype(o_refs[1].dtype)


def _out_proj(merged, w_out, layer, x2d, gain, final_norm):
    m, d = x2d.shape
    tm = min(PROJ_TM, m)
    row_spec = pl.BlockSpec((tm, d), lambda i: (i, 0))
    out_specs, out_shape = [row_spec], [jax.ShapeDtypeStruct((m, d), F32)]
    if not final_norm:
        out_specs.append(row_spec)
        out_shape.append(jax.ShapeDtypeStruct((m, d), BF16))
    return pl.pallas_call(
        functools.partial(_out_kernel, final_norm=final_norm),
        grid=(m // tm,),
        in_specs=[row_spec, pl.BlockSpec((None, d, d), lambda i: (layer, 0, 0)), row_spec,
                  pl.BlockSpec((1, d), lambda i: (0, 0))],
        out_specs=out_specs,
        out_shape=out_shape,
        compiler_params=_params("parallel"),
        name="out_proj",
    )(merged, w_out, x2d, gain.reshape(1, d))


def _layer_vectors(l, pool_scale, rwkv_mu, rwkv_w0, rwkv_w_up, rwkv_a0, rwkv_a_up, rwkv_k_k, rwkv_k_a, rwkv_r_k,
                   rwkv_gn_w, rwkv_gn_b, attn_sink):
    w_lora = jnp.zeros((LANES, 2 * BRANCH_WIDTH), F32)
    w_lora = w_lora.at[:LORA_RANK, :BRANCH_WIDTH].set(rwkv_w_up[l]).at[LORA_RANK:, BRANCH_WIDTH:].set(rwkv_a_up[l])
    vecs = jnp.stack([rwkv_w0[l], rwkv_a0[l], rwkv_k_k[l], rwkv_k_a[l], rwkv_r_k[l], rwkv_gn_w[l], rwkv_gn_b[l],
                      jnp.zeros_like(rwkv_w0[l])])
    return dict(pool_scale=pool_scale[l].reshape(1, -1), mu=rwkv_mu[l].reshape(1, -1), vecs=vecs,
                w_lora=w_lora.astype(BF16), sink=attn_sink[l][np.array(ATT_HEAD_ORDER)])


def _sink_rows(sink_ordered, n_q):
    rows = jnp.repeat(sink_ordered.reshape(-1, 4), n_q, axis=1)
    return jnp.broadcast_to(rows[..., None], rows.shape + (LANES,))


def _mixer_layer(x2, h, b, l, wts, lv, next_gain, final_norm, *, pos0, pool_hist16, shift_prev, s0_bd, bias, caches):
    t = x2.shape[0] // b
    proj ={name: _matmul(h, wts["w_in"], l, col0, n, tn=tn, act=act, out_dtype=dt, name="proj_" + name)
            for name, (col0, n, tn, act, dt) in PROJ_SEGMENTS.items()}
    u_pool, p_shift, q, kv, gate = (proj[name].reshape(b, t, -1) for name in ("pool", "shift", "q", "kv", "gate"))
    mg = proj["merge"]

    ga = _pool_branch(u_pool, pool_hist16, gate, wts["pool_w"], l, lv["pool_scale"], pos0)

    t_pad = -(-t // CHUNK) * CHUNK
    if t_pad != t:
        p_in = jnp.pad(p_shift, ((0, 0), (0, t_pad - t), (0, 0)))
        gate_in = jnp.pad(gate, ((0, 0), (0, t_pad - t), (0, 0)))
    else:
        p_in, gate_in = p_shift, gate
    gb, s_bd = _rwkv_branch(p_in, shift_prev, s0_bd, gate_in, lv["mu"], lv["vecs"], lv["w_lora"],
                            n_valid=CHUNK - (t_pad - t))
    gb = gb[:, :t]

    if caches is None:
        gc = _attn_prompt(q, kv, bias, _sink_rows(lv["sink"], CHUNK), gate)
    else:
        gc = _attn_sample(q, kv, caches[0], caches[1], bias, _sink_rows(lv["sink"], t), gate)

    merged = _merge(ga.reshape(b * t, -1), gb.reshape(b * t, -1), gc.reshape(b * t, -1), wts["w_branch"], l, mg)
    outs = _out_proj(merged, wts["w_out"], l, x2, next_gain, final_norm)
    y, h_next = (outs[0], None) if final_norm else outs
    return (y, h_next, kv[..., :KV_WIDTH], kv[..., KV_WIDTH:], _state_from_blockdiag(s_bd), p_shift[:, -1],
            u_pool[:, t - POOL_HIST:])


def kernel(x_prompt, x_sample, cache_attn_k, cache_attn_v, state_rwkv, state_rwkv_shift, state_pool, norm_g, w_in,
           pool_w, pool_scale, rwkv_mu, rwkv_w0, rwkv_w_up, rwkv_a0, rwkv_a_up, rwkv_k_k, rwkv_k_a, rwkv_r_k,
           rwkv_gn_w, rwkv_gn_b, attn_sink, rel_pos_table, w_branch, w_out, final_norm_g):
    depth = norm_g.shape[0]
    bp, tp, _ = x_prompt.shape
    bs, ts, _ = x_sample.shape
    n_cache = cache_attn_k.shape[2]

    key_p = jnp.arange((BAND_CHUNKS + 1) * CHUNK) - BAND_CHUNKS * CHUNK
    bias_p = _rel_bias(rel_pos_table, key_p, CHUNK)
    key_s = jnp.concatenate([jnp.arange(n_cache) - n_cache, jnp.arange(ts)])
    bias_s = _rel_bias(rel_pos_table, key_s, ts)

    wts = dict(w_in=w_in.astype(BF16), pool_w=pool_w.astype(BF16), w_branch=w_branch.astype(BF16),
               w_out=w_out.astype(BF16))
    xp, xs = x_prompt.reshape(bp * tp, -1), x_sample.reshape(bs * ts, -1)
    hp = _rmsnorm(xp, norm_g[0], BF16, PROJ_TM)
    hs = _rmsnorm(xs, norm_g[0], BF16, PROJ_TM)
    outs_p, outs_s = [], []
    for l in range(depth):
        lv = _layer_vectors(l, pool_scale, rwkv_mu, rwkv_w0, rwkv_w_up, rwkv_a0, rwkv_a_up, rwkv_k_k, rwkv_k_a,
                            rwkv_r_k, rwkv_gn_w, rwkv_gn_b, attn_sink)
        last = l == depth - 1
        next_gain = final_norm_g if last else norm_g[l + 1]
        xp, hp, kp, vp, sp, shp, plp = _mixer_layer(
            xp, hp, bp, l, wts, lv, next_gain, last, pos0=0,
            pool_hist16=jnp.zeros((bp, 16, BRANCH_WIDTH), F32),
            shift_prev=jnp.zeros((bp, 1, SHIFT_WIDTH), F32),
            s0_bd=jnp.zeros((bp, PAIRS, LANES, LANES), F32),
            bias=bias_p, caches=None)
        outs_p.append((kp[:, -WINDOW:].reshape(bp, WINDOW, ATT_KV_HEADS, HEAD_DIM),
                       vp[:, -WINDOW:].reshape(bp, WINDOW, ATT_KV_HEADS, HEAD_DIM), sp, shp, plp))
        xs, hs, k_s, v_s, s_s, sh_s, pl_s = _mixer_layer(
            xs, hs, bs, l, wts, lv, next_gain, last, pos0=PAST_LEN,
            pool_hist16=jnp.pad(state_pool[l], ((0, 0), (1, 0), (0, 0))),
            shift_prev=state_rwkv_shift[l][:, None, :],
            s0_bd=_state_to_blockdiag(state_rwkv[l]),
            bias=bias_s,
            caches=(cache_attn_k[l].reshape(bs, n_cache, KV_WIDTH), cache_attn_v[l].reshape(bs, n_cache, KV_WIDTH)))
        outs_s.append((k_s.reshape(bs, ts, ATT_KV_HEADS, HEAD_DIM), v_s.reshape(bs, ts, ATT_KV_HEADS, HEAD_DIM),
                       s_s, sh_s, pl_s))

    st = lambda outs, i: jnp.stack([o[i] for o in outs])
    return (xp.reshape(x_prompt.shape), xs.reshape(x_sample.shape), st(outs_p, 0), st(outs_p, 1), st(outs_s, 0), st(outs_s, 1),
            st(outs_p, 2), st(outs_s, 2), st(outs_p, 3), st(outs_s, 3), st(outs_p, 4), st(outs_s, 4))
```

```python
import functools
import math

import jax
import jax.numpy as jnp
import numpy as np
from jax import lax
from jax.experimental import pallas as pl
from jax.experimental.pallas import tpu as pltpu

F32 = jnp.float32
BF16 = jnp.bfloat16

D_MODEL = 2048
BRANCH_WIDTH = 1024
POOL_WINDOWS = (2, 4, 8, 16)
POOL_GROUP_DIM = BRANCH_WIDTH // len(POOL_WINDOWS)
POOL_HIST = max(POOL_WINDOWS) - 1
HEAD_DIM = 64
RWKV_HEADS = BRANCH_WIDTH // HEAD_DIM
LORA_RANK = 64
SHIFT_WIDTH = 3 * BRANCH_WIDTH + 2 * LORA_RANK
GN_EPS = 64e-5
ATT_HEADS = 16
ATT_KV_HEADS = 4
ATT_GROUP = ATT_HEADS // ATT_KV_HEADS
KV_WIDTH = ATT_KV_HEADS * HEAD_DIM
CHUNK = 64
WINDOW = 128
BAND_CHUNKS = WINDOW // CHUNK
N_BUCKETS = 32
MAX_DISTANCE = 128
NEG_INF = -1e30
RMS_EPS = 1e-6
PAST_LEN = 2048
N_BRANCH = 3
OFF_POOL = 0
OFF_SHIFT = OFF_POOL + BRANCH_WIDTH
OFF_Q = OFF_SHIFT + SHIFT_WIDTH
OFF_K = OFF_Q + BRANCH_WIDTH
OFF_V = OFF_K + KV_WIDTH
OFF_GATE = OFF_V + KV_WIDTH
OFF_MERGE = OFF_GATE + N_BRANCH * BRANCH_WIDTH

LANES = 128
PAIRS = BRANCH_WIDTH // LANES
ATTN_CHUNKS_PER_STEP = 4
RWKV_CHUNKS_PER_STEP = 4
RWKV_GROUP = 8
KEY_PAD = 256
ATT_HEAD_ORDER = tuple(8 * m + o for m in range(ATT_KV_HEADS // 2) for o in (0, 2, 5, 7, 1, 3, 4, 6))
ROW_TM = 512
MERGE_TM, MERGE_TN = 1024, 1024
PROJ_SEGMENTS = {
    "pool": (OFF_POOL, BRANCH_WIDTH, 1024, 1024, None, F32),
    "shift": (OFF_SHIFT, SHIFT_WIDTH, SHIFT_WIDTH, 512, None, F32),
    "q": (OFF_Q, BRANCH_WIDTH, 1024, 1024, "att_scale", BF16),
    "kv": (OFF_K, 2 * KV_WIDTH, 2 * KV_WIDTH, 1024, None, F32),
    "gate": (OFF_GATE, N_BRANCH * BRANCH_WIDTH, 3072, 1024, "silu", BF16),
    "merge": (OFF_MERGE, N_BRANCH * D_MODEL, 3072, 1024, "sigmoid", BF16),
}
VMEM_LIMIT = 56 * 1024 * 1024


def _params(*sem):
    return pltpu.CompilerParams(dimension_semantics=sem, vmem_limit_bytes=VMEM_LIMIT)


def _sigmoid(x):
    return 1.0 / (1.0 + jnp.exp(-x))


def _dot(a, b):
    return jnp.dot(a, b, preferred_element_type=F32)


def _dot_nt(a, b):
    return lax.dot_general(a, b, (((1,), (1,)), ((), ())), preferred_element_type=F32)


def _split3(x):
    hi = x.astype(BF16)
    r1 = x - hi.astype(F32)
    mid = r1.astype(BF16)
    lo = (r1 - mid.astype(F32)).astype(BF16)
    return hi, mid, lo


def _dot3_right_exact(x, b):
    hi, mid, lo = _split3(x)
    return _dot(hi, b) + _dot(mid, b) + _dot(lo, b)


def _dot3_left_exact(a, x):
    hi, mid, lo = _split3(x)
    return _dot(a, hi) + _dot(a, mid) + _dot(a, lo)


def _rmsnorm_kernel(x_ref, g_ref, o_ref):
    x = x_ref[...]
    ms = jnp.mean(x * x, axis=-1, keepdims=True)
    o_ref[...] = (x * lax.rsqrt(ms + RMS_EPS) * g_ref[...]).astype(o_ref.dtype)


def _rmsnorm(x2d, g, out_dtype, tm):
    m, d = x2d.shape
    tm = min(tm, m)
    return pl.pallas_call(
        _rmsnorm_kernel,
        grid=(m // tm,),
        in_specs=[pl.BlockSpec((tm, d), lambda i: (i, 0)), pl.BlockSpec((1, d), lambda i: (0, 0))],
        out_specs=pl.BlockSpec((tm, d), lambda i: (i, 0)),
        out_shape=jax.ShapeDtypeStruct((m, d), out_dtype),
        compiler_params=_params("parallel"),
        name="rmsnorm",
    )(x2d, g.reshape(1, d))


def _mm_kernel(a_ref, w_ref, o_ref, *, act):
    acc = _dot(a_ref[...], w_ref[0])
    if act == "silu":
        acc = acc * _sigmoid(acc)
    elif act == "sigmoid":
        acc = _sigmoid(acc)
    elif act == "att_scale":
        acc = acc * (HEAD_DIM ** -0.5)
    o_ref[...] = acc.astype(o_ref.dtype)


def _matmul(a, w, layer, col0, n, *, tn, tm, act, out_dtype, name):
    m, k = a.shape
    tm = min(tm, m)
    return pl.pallas_call(
        functools.partial(_mm_kernel, act=act),
        grid=(n // tn, m // tm),
        in_specs=[pl.BlockSpec((tm, k), lambda j, i: (i, 0)),
                  pl.BlockSpec((pl.Element(1), pl.Element(k), pl.Element(tn)),
                               lambda j, i: (layer, 0, pl.multiple_of(col0 + j * tn, LANES)))],
        out_specs=pl.BlockSpec((tm, tn), lambda j, i: (i, j)),
        out_shape=jax.ShapeDtypeStruct((m, n), out_dtype),
        compiler_params=_params("arbitrary", "arbitrary"),
        name=name,
    )(a, w)


def _pool_kernel(u_ref, prev_ref, hist_ref, zg_ref, w_ref, scale_ref, o_ref, *, tt, pos0):
    i = pl.program_id(1)
    u = u_ref[0]
    prev = jnp.where(i == 0, hist_ref[0], prev_ref[0])
    pos = pos0 + i * tt + lax.broadcasted_iota(jnp.int32, (tt, POOL_GROUP_DIM), 0)
    for g, w in enumerate(POOL_WINDOWS):
        sl = slice(g * POOL_GROUP_DIM, (g + 1) * POOL_GROUP_DIM)
        e = jnp.concatenate([prev[:, sl], u[:, sl]], axis=0)
        step = 1
        while step < w:
            e = e + pltpu.roll(e, step, 0)
            step *= 2
        s = e[16:, :]
        cnt = jnp.minimum(w, pos + 1).astype(F32)
        pooled = s / cnt - u[:, sl]
        o = _dot(pooled.astype(BF16), w_ref[g]) * scale_ref[:, sl]
        o_ref[0, :, sl] = (o * zg_ref[0, :, sl].astype(F32)).astype(o_ref.dtype)


def _pool_branch(u, hist16, gate, pool_w, layer, pool_scale, pos0):
    b, t, c = u.shape
    tt = min(t, 512)
    sub = tt // 16
    return pl.pallas_call(
        functools.partial(_pool_kernel, tt=tt, pos0=pos0),
        grid=(b, t // tt),
        in_specs=[
            pl.BlockSpec((1, tt, c), lambda bi, i: (bi, i, 0)),
            pl.BlockSpec((1, 16, c), lambda bi, i: (bi, jnp.maximum(i * sub - 1, 0), 0)),
            pl.BlockSpec((1, 16, c), lambda bi, i: (bi, 0, 0)),
            pl.BlockSpec((1, tt, c), lambda bi, i: (bi, i, 0)),
            pl.BlockSpec((None,) + pool_w.shape[1:], lambda bi, i: (layer, 0, 0, 0)),
            pl.BlockSpec((1, c), lambda bi, i: (0, 0)),
        ],
        out_specs=pl.BlockSpec((1, tt, c), lambda bi, i: (bi, i, 0)),
        out_shape=jax.ShapeDtypeStruct((b, t, c), BF16),
        compiler_params=_params("parallel", "parallel"),
        name="pool_branch",
    )(u, u, hist16, gate, pool_w, pool_scale)


def _rwkv_kernel(p_ref, sprev_ref, s0_ref, zg_ref, mu_ref, vec_ref, wl_ref, o_ref, s_ref, carry_ref,
                 *, n_valid, n_sub):
    @pl.when(pl.program_id(1) == 0)
    def _():
        s_ref[...] = s0_ref[...]
        carry_ref[...] = sprev_ref[0]

    for sub in range(n_sub):
        rows = pl.ds(sub * CHUNK, CHUNK)
        _rwkv_chunk(p_ref.at[:, rows, :], zg_ref.at[:, rows, :], mu_ref, vec_ref, wl_ref, o_ref.at[:, rows, :],
                    s_ref, carry_ref, n_valid=n_valid)


def _rwkv_chunk(p_ref, zg_ref, mu_ref, vec_ref, wl_ref, o_ref, s_ref, carry_ref, *, n_valid):
    C = CHUNK
    p = p_ref[0]
    row_w = lax.broadcasted_iota(jnp.int32, (C, SHIFT_WIDTH), 0)
    row = lax.broadcasted_iota(jnp.int32, (C, BRANCH_WIDTH), 0)
    prev = jnp.where(row_w == 0, carry_ref[...], pltpu.roll(p, 1, 0))
    carry_ref[...] = p[C - 1:C, :]
    xs = p + (prev - p) * mu_ref[...]

    w0, a0, k_k, k_a, r_k, gn_w, gn_b = (vec_ref[i:i + 1, :] for i in range(7))
    lane = lax.broadcasted_iota(jnp.int32, (C, LANES), 1)
    head0 = lane < HEAD_DIM
    lin = xs[:, 3 * BRANCH_WIDTH:]
    lin = jnp.where(head0, jnp.tanh(lin), lin)
    lora = _dot(lin.astype(BF16), wl_ref[...])
    log_decay = -math.exp(-0.5) * _sigmoid(w0 + lora[:, :BRANCH_WIDTH])
    iclr = _sigmoid(a0 + lora[:, BRANCH_WIDTH:])
    r = xs[:, :BRANCH_WIDTH]
    k = xs[:, BRANCH_WIDTH:2 * BRANCH_WIDTH]
    v = xs[:, 2 * BRANCH_WIDTH:3 * BRANCH_WIDTH]
    kk_raw = k * k_k
    k2 = k * (1.0 + (iclr - 1.0) * k_a)
    if n_valid < C:
        ok = row < n_valid
        log_decay = jnp.where(ok, log_decay, 0.0)
        kk_raw = jnp.where(ok, kk_raw, 0.0)
        k2 = jnp.where(ok, k2, 0.0)
        r = jnp.where(ok, r, 0.0)
        v = jnp.where(ok, v, 0.0)

    ti = lax.broadcasted_iota(jnp.int32, (C, C), 0)
    si = lax.broadcasted_iota(jnp.int32, (C, C), 1)
    tri = (si <= ti).astype(BF16)
    cum = _dot3_left_exact(tri, log_decay)
    cum_x = cum - log_decay
    cum_c = cum[C - 1:C, :]

    ri =lax.broadcasted_iota(jnp.int32, (2 * C, 2 * C), 0)
    ci = lax.broadcasted_iota(jnp.int32, (2 * C, 2 * C), 1)
    same_blk = (ri >= C) == (ci >= C)
    strict = same_blk & ((ci & (C - 1)) < (ri & (C - 1)))
    incl = same_blk & ((ci & (C - 1)) <= (ri & (C - 1)))

    def stack(x):
        return jnp.concatenate([jnp.where(head0, x, 0.0), jnp.where(head0, 0.0, x)], axis=0)

    def each(f, *cols):
        return [f(*xs) for xs in zip(*cols)]

    def bf(xs):
        return [x.astype(BF16) for x in xs]

    def head_sums(xs):
        lo = [jnp.sum(jnp.where(head0, x, 0.0), axis=-1, keepdims=True) for x in xs]
        hi = [jnp.sum(jnp.where(head0, 0.0, x), axis=-1, keepdims=True) for x in xs]
        return [jnp.where(head0, a, b) for a, b in zip(lo, hi)]

    def stack_b(x):
        xb = x.astype(BF16)
        zero = jnp.zeros((), BF16)
        return jnp.concatenate([jnp.where(head0, xb, zero), jnp.where(head0, zero, xb)], axis=0)

    def unfold(x, keep):
        xr = pltpu.roll(x, C, 1)
        top, bot = keep[:C], keep[C:]
        from_p = jnp.concatenate([jnp.where(top, x[:C], 0.0), jnp.where(bot, xr[C:], 0.0)], axis=0)
        from_q = jnp.concatenate([jnp.where(top, xr[:C], 0.0), jnp.where(bot, x[C:], 0.0)], axis=0)
        return from_p, from_q

    for g0 in range(0, PAIRS, RWKV_GROUP):
        prs = list(range(g0, g0 + RWKV_GROUP))
        sls = [slice(pr * LANES, (pr + 1) * LANES) for pr in prs]
        r_p = [r[:, sl] for sl in sls]
        k_p = [k2[:, sl] for sl in sls]
        v_p = [v[:, sl] for sl in sls]
        kkr = [kk_raw[:, sl] for sl in sls]
        ic = [iclr[:, sl] for sl in sls]
        cm = [cum[:, sl] for sl in sls]
        cx = [cum_x[:, sl] for sl in sls]
        cc = [cum_c[:, sl] for sl in sls]
        n2 = head_sums(each(lambda x: x * x, kkr))
        kk = each(lambda x, n: x * lax.rsqrt(jnp.maximum(n, 1e-24)), kkr, n2)
        b_p = each(lambda x, i: x * i, kk, ic)
        e_neg = each(lambda c_: jnp.exp(-c_), cm)
        e_end = each(lambda c_, e_: jnp.exp(e_ - c_), cm, cc)
        w_c = each(jnp.exp, cc)
        rt = each(lambda x, c_: x * jnp.exp(c_), r_p, cm)
        at_b = each(lambda x, c_: stack_b(-x * jnp.exp(c_)), kk, cx)
        rt_b = each(stack_b, rt)
        v_s = each(stack, v_p)
        v_b = bf(v_s)
        bb_b = each(lambda x, e_: stack_b(x * e_), b_p, e_end)
        kb_b = each(lambda x, e_: stack_b(x * e_), k_p, e_end)
        bk_b = bf(each(lambda b_, k_, e_: jnp.concatenate([b_ * e_, k_ * e_], axis=0), b_p, k_p, e_neg))
        ar_b = each(lambda a_, r_: jnp.concatenate([a_, r_], axis=0), at_b, rt_b)
        prod = each(_dot_nt, ar_b, bk_b)
        a_pair = each(lambda x: unfold(x[:2 * C], strict), prod)
        r_pair = each(lambda x: unfold(x[2 * C:], incl), prod)
        a_ab = [x[0] for x in a_pair]
        a_ak_b = bf([x[1] for x in a_pair])
        a_rb_b = bf([x[0] for x in r_pair])
        a_rk_b = bf([x[1] for x in r_pair])
        tinv = each(lambda x: jnp.where(ri == ci, 1.0, x), a_ab)
        pw_b = bf(a_ab)
        pw_b = bf(each(_dot, pw_b, pw_b))
        n_sq = int(math.log2(C)) - 1
        for step in range(n_sq):
            if step < n_sq - 1:
                both = each(lambda t_, p_: _dot(p_, jnp.concatenate([t_.astype(BF16), p_], axis=1)), tinv, pw_b)
                tinv = each(lambda t_, x: t_ + x[:, :2 * C], tinv, both)
                pw_b = bf([x[:, 2 * C:] for x in both])
            else:
                tinv = each(lambda t_, p_: t_ + _dot(p_, t_.astype(BF16)), tinv, pw_b)
        t_b = bf(tinv)
        akv = each(_dot, a_ak_b, v_b)
        au = each(lambda t_, a_, k_: _dot(t_, jnp.concatenate([a_, k_.astype(BF16)], axis=1)), t_b, at_b, akv)
        ry = each(_dot, a_rb_b, bf(au))
        rh = each(lambda x, y_: x + y_[:C, :LANES] + y_[C:, :LANES], rt, ry)
        y0 = each(lambda y_, k_, v_: y_[:, LANES:] + _dot(k_, v_), ry, a_rk_b, v_b)
        mg = each(lambda x, b_: _dot(x.T.astype(BF16), b_), au, bb_b)
        m_lr = [x[:LANES] for x in mg]
        g_bd = each(lambda x, v_, k_: x[LANES:] + _dot(v_.T.astype(BF16), k_), mg, v_s, kb_b)
        s_bd = [s_ref[0, pr] for pr in prs]
        s_b = bf(s_bd)
        y = each(lambda x, s_, y_: _dot_nt(x.astype(BF16), s_) + (y_[:C] + y_[C:]), rh, s_b, y0)
        s_new = each(lambda s_, w_, sb_, m_, g_: s_ * w_ + _dot(sb_, m_.astype(BF16)) + g_,
                     s_bd, w_c, s_b, m_lr, g_bd)
        for pr, s_ in zip(prs, s_new):
            s_ref[0, pr] = s_
        yc = each(lambda x, m_: x - m_ * (1.0 / HEAD_DIM), y, head_sums(y))
        var = head_sums(each(lambda x: x * x, yc))
        rk = head_sums(each(lambda r_, k_, sl: r_ * k_ * r_k[:, sl], r_p, k_p, sls))
        for sl, yc_, var_, rk_, v_ in zip(sls, yc, var, rk, v_p):
            yn = yc_ * lax.rsqrt(var_ * (1.0 / HEAD_DIM) + GN_EPS) * gn_w[:, sl] + gn_b[:, sl]
            o_ref[0, :, sl] = ((yn + rk_ * v_) * zg_ref[0, :, sl].astype(F32)).astype(o_ref.dtype)


def _rwkv_branch(p_shift, shift_prev, s0_bd, gate, mu, vecs, w_lora, n_valid):
    b, t, _ = p_shift.shape
    n_sub = math.gcd(t // CHUNK, RWKV_CHUNKS_PER_STEP)
    rows = n_sub * CHUNK
    return pl.pallas_call(
        functools.partial(_rwkv_kernel, n_valid=n_valid, n_sub=n_sub),
        grid=(b, t // rows),
        in_specs=[
            pl.BlockSpec((1, rows, SHIFT_WIDTH), lambda bi, c: (bi, c, 0)),
            pl.BlockSpec((1, 1, SHIFT_WIDTH), lambda bi, c: (bi, 0, 0)),
            pl.BlockSpec((1, PAIRS, LANES, LANES), lambda bi, c: (bi, 0, 0, 0)),
            pl.BlockSpec((1, rows, BRANCH_WIDTH), lambda bi, c: (bi, c, 1)),
            pl.BlockSpec((1, SHIFT_WIDTH), lambda bi, c: (0, 0)),
            pl.BlockSpec((8, BRANCH_WIDTH), lambda bi, c: (0, 0)),
            pl.BlockSpec((LANES, 2 * BRANCH_WIDTH), lambda bi, c: (0, 0)),
        ],
        out_specs=[
            pl.BlockSpec((1, rows, BRANCH_WIDTH), lambda bi, c: (bi, c, 0)),
            pl.BlockSpec((1, PAIRS, LANES, LANES), lambda bi, c: (bi, 0, 0, 0)),
        ],
        out_shape=[
            jax.ShapeDtypeStruct((b, t, BRANCH_WIDTH), BF16),
            jax.ShapeDtypeStruct((b, PAIRS, LANES, LANES), F32),
        ],
        scratch_shapes=[pltpu.VMEM((1, SHIFT_WIDTH), F32)],
        compiler_params=_params("parallel", "arbitrary"),
        name="rwkv_branch",
    )(p_shift, shift_prev, s0_bd, gate, mu, vecs, w_lora)


def _state_to_blockdiag(s):
    b = s.shape[0]
    sp = s.reshape(b, PAIRS, 2, HEAD_DIM, HEAD_DIM)
    z = jnp.zeros_like(sp[:, :, 0])
    top = jnp.concatenate([sp[:, :, 0], z], axis=-1)
    bot = jnp.concatenate([z, sp[:, :, 1]], axis=-1)
    return jnp.concatenate([top, bot], axis=-2)


def _state_from_blockdiag(s_bd):
    b = s_bd.shape[0]
    h0 = s_bd[:, :, :HEAD_DIM, :HEAD_DIM]
    h1 = s_bd[:, :, HEAD_DIM:, HEAD_DIM:]
    return jnp.stack([h0, h1], axis=2).reshape(b, RWKV_HEADS, HEAD_DIM, HEAD_DIM)


def _attn_kernel(*refs, n_kv_parts, tq, n_keys, masked, n_sub):
    q_ref = refs[0]
    k_refs = refs[1:1 + n_kv_parts]
    v_refs = refs[1 + n_kv_parts:1 + 2 * n_kv_parts]
    bias_ref, sink_ref, zg_ref, o_ref = refs[1 + 2 * n_kv_parts:]
    k_rows = jnp.concatenate([kr[0] for kr in k_refs], axis=0)
    v_rows = jnp.concatenate([vr[0] for vr in v_refs], axis=0)
    for sub in range(n_sub):
        rows = pl.ds(sub * tq, tq)
        _attn_chunk(q_ref.at[:, rows, :], k_rows[sub * tq:sub * tq + n_keys], v_rows[sub * tq:sub * tq + n_keys],
                    bias_ref, sink_ref, zg_ref.at[:, rows, :], o_ref.at[:, rows, :],
                    n=pl.program_id(1) * n_sub + sub, tq=tq, n_keys=n_keys, masked=masked)


def _attn_chunk(q_ref, k_seen, v_seen, bias_ref, sink_ref, zg_ref, o_ref, *, n, tq, n_keys, masked):
    pad = jnp.zeros((KEY_PAD - n_keys, KV_WIDTH), F32)
    k_all = jnp.concatenate([k_seen, pad], axis=0)
    v_all = jnp.concatenate([v_seen, pad], axis=0)
    lane =lax.broadcasted_iota(jnp.int32, (tq, LANES), 1)
    head0 = lane < HEAD_DIM
    col = lax.broadcasted_iota(jnp.int32, (1, KEY_PAD), 1)
    valid = col < n_keys
    if masked:
        valid = valid & (col // CHUNK + n - BAND_CHUNKS >= 0)
    q = q_ref[0]
    zero = jnp.zeros((), q.dtype)
    lo_half = lambda s: jnp.where(head0, s, zero)
    hi_half = lambda s: jnp.where(head0, zero, s)
    lhs, keys, vals = [], [], []
    for m in range(ATT_KV_HEADS // 2):
        slabs = [q[:, (4 * m + i) * LANES:(4 * m + i + 1) * LANES] for i in range(4)]
        k_sl = k_all[:, m * LANES:(m + 1) * LANES]
        v_sl = v_all[:, m * LANES:(m + 1) * LANES]
        lhs.append(jnp.concatenate([lo_half(slabs[0]), lo_half(slabs[1]), hi_half(slabs[2]), hi_half(slabs[3])], 0))
        lhs.append(jnp.concatenate([hi_half(slabs[0]), hi_half(slabs[1]), lo_half(slabs[2]), lo_half(slabs[3])], 0))
        keys += [k_sl.astype(BF16), pltpu.roll(k_sl, HEAD_DIM, 1).astype(BF16)]
        vals += [v_sl.astype(BF16), pltpu.roll(v_sl, HEAD_DIM, 1).astype(BF16)]
    n_grp = len(lhs)
    ones = jnp.ones((KEY_PAD, LANES), BF16)
    vals = [jnp.concatenate([x, ones], axis=1) for x in vals]
    logits = [jnp.where(valid, _dot_nt(lhs[i], keys[i]) + bias_ref[i], NEG_INF) for i in range(n_grp)]
    sink = [sink_ref[i] for i in range(n_grp)]
    mx = [jnp.maximum(jnp.max(logits[i], axis=-1, keepdims=True), sink[i]) for i in range(n_grp)]
    e = [jnp.exp(jnp.concatenate([logits[i][:, :LANES] - mx[i], logits[i][:, LANES:] - mx[i]], axis=1))
         for i in range(n_grp)]
    pv = [_dot(e[i].astype(BF16), vals[i]) for i in range(n_grp)]
    out = [pv[i][:, :LANES] / (pv[i][:, LANES:] + jnp.exp(sink[i] - mx[i])) for i in range(n_grp)]
    for m in range(ATT_KV_HEADS // 2):
        same, swap = out[2 * m], out[2 * m + 1]
        for i in range(4):
            a, b = same[i * tq:(i + 1) * tq], swap[i * tq:(i + 1) * tq]
            res = jnp.where(head0, a, b) if i < 2 else jnp.where(head0, b, a)
            sl = slice((4 * m + i) * LANES, (4 * m + i + 1) * LANES)
            o_ref[0, :, sl] = (res * zg_ref[0, :, sl].astype(F32)).astype(o_ref.dtype)


def _attn_prompt(q, kv, bias, sink_rows, gate):
    b, t, _ = q.shape
    nc = t // CHUNK
    n_sub = math.gcd(nc, ATTN_CHUNKS_PER_STEP)
    rows = n_sub * CHUNK
    k_specs, v_specs = ([pl.BlockSpec((1, CHUNK, KV_WIDTH),
                                      lambda bi, n, d=d, c=c: (bi, jnp.maximum(n * n_sub - d, 0), c))
                         for d in range(BAND_CHUNKS, 0, -1)]
                        + [pl.BlockSpec((1, rows, KV_WIDTH), lambda bi, n, c=c: (bi, n, c))] for c in range(2))
    return pl.pallas_call(
        functools.partial(_attn_kernel, n_kv_parts=BAND_CHUNKS + 1, tq=CHUNK,
                          n_keys=(BAND_CHUNKS + 1) * CHUNK, masked=True, n_sub=n_sub),
        grid=(b, nc // n_sub),
        in_specs=[pl.BlockSpec((1, rows, BRANCH_WIDTH), lambda bi, n: (bi, n, 0))]
        + k_specs + v_specs
        + [pl.BlockSpec(bias.shape, lambda bi, n: (0, 0, 0)),
           pl.BlockSpec(sink_rows.shape, lambda bi, n: (0, 0, 0)),
           pl.BlockSpec((1, rows, BRANCH_WIDTH), lambda bi, n: (bi, n, 2))],
        out_specs=pl.BlockSpec((1, rows, BRANCH_WIDTH), lambda bi, n: (bi, n, 0)),
        out_shape=jax.ShapeDtypeStruct((b, t, BRANCH_WIDTH), BF16),
        compiler_params=_params("parallel", "parallel"),
        name="attn_prompt",
    )(q, kv, kv, kv, kv, kv, kv, bias, sink_rows, gate)


def _attn_sample(q, kv, k_cache, v_cache, bias, sink_rows, gate):
    b, t, _ = q.shape
    n_cache = k_cache.shape[1]
    cache_spec = pl.BlockSpec((1, n_cache, KV_WIDTH), lambda bi, n: (bi, 0, 0))
    return pl.pallas_call(
        functools.partial(_attn_kernel, n_kv_parts=2, tq=t, n_keys=n_cache + t, masked=False, n_sub=1),
        grid=(b, 1),
        in_specs=[pl.BlockSpec((1, t, BRANCH_WIDTH), lambda bi, n: (bi, 0, 0)),
                  cache_spec, pl.BlockSpec((1, t, KV_WIDTH), lambda bi, n: (bi, 0, 0)),
                  cache_spec, pl.BlockSpec((1, t, KV_WIDTH), lambda bi, n: (bi, 0, 1)),
                  pl.BlockSpec(bias.shape, lambda bi, n: (0, 0, 0)),
                  pl.BlockSpec(sink_rows.shape, lambda bi, n: (0, 0, 0)),
                  pl.BlockSpec((1, t, BRANCH_WIDTH), lambda bi, n: (bi, 0, 2))],
        out_specs=pl.BlockSpec((1, t, BRANCH_WIDTH), lambda bi, n: (bi, 0, 0)),
        out_shape=jax.ShapeDtypeStruct((b, t, BRANCH_WIDTH), BF16),
        compiler_params=_params("parallel", "parallel"),
        name="attn_sample",
    )(q, k_cache, kv, v_cache, kv, bias, sink_rows, gate)


def _t5_bucket(rel):
    half = N_BUCKETS // 2
    n = -rel
    ret = jnp.where(n < 0, half, 0)
    n = jnp.abs(n)
    max_exact = half // 2
    large = max_exact + (jnp.log(jnp.maximum(n, 1).astype(jnp.float32) / max_exact)
                         / math.log(MAX_DISTANCE / max_exact) * (half - max_exact)).astype(jnp.int32)
    large = jnp.minimum(large, half - 1)
    return ret + jnp.where(n < max_exact, n, large)


def _bias_kernel(tab_ref, onehot_ref, o_ref):
    o_ref[...] = _dot3_right_exact(tab_ref[...], onehot_ref[...])


def _rel_bias(table, key_pos, n_q):
    n_keys = key_pos.shape[0]
    bucket = _t5_bucket(key_pos[None, :] - jnp.arange(n_q)[:, None])
    bucket = jnp.pad(bucket, ((0, 0), (0, KEY_PAD - n_keys)), constant_values=-1)
    onehot = (bucket.reshape(1, -1) == jnp.arange(N_BUCKETS)[:, None]).astype(BF16)
    out = pl.pallas_call(
        _bias_kernel,
        out_shape=jax.ShapeDtypeStruct((ATT_HEADS, n_q * KEY_PAD), F32),
        name="rel_bias",
    )(table.T[np.array(ATT_HEAD_ORDER)], onehot)
    return out.reshape(len(ATT_HEAD_ORDER) // 4, 4 * n_q, KEY_PAD)


def _merge_kernel(ga_ref, gb_ref, gc_ref, w_ref, m0_ref, m1_ref, m2_ref, o_ref):
    acc = m0_ref[...].astype(F32) * _dot(ga_ref[...], w_ref[0])
    acc = acc + m1_ref[...].astype(F32) * _dot(gb_ref[...], w_ref[1])
    acc = acc + m2_ref[...].astype(F32) * _dot(gc_ref[...], w_ref[2])
    o_ref[...] = acc.astype(o_ref.dtype)


def _merge(ga, gb, gc, w_branch, layer, mg):
    m, kdim = ga.shape
    tm = min(MERGE_TM, m)
    tn = MERGE_TN
    nj = D_MODEL // tn
    g_spec = pl.BlockSpec((tm, kdim), lambda j, i: (i, 0))
    m_specs = [pl.BlockSpec((tm, tn), lambda j, i, br=br: (i, br * nj + j)) for br in range(N_BRANCH)]
    return pl.pallas_call(
        _merge_kernel,
        grid=(nj, m // tm),
        in_specs=[g_spec, g_spec, g_spec,
                  pl.BlockSpec((None, N_BRANCH, kdim, tn), lambda j, i: (layer, 0, 0, j))] + m_specs,
        out_specs=pl.BlockSpec((tm, tn), lambda j, i: (i, j)),
        out_shape=jax.ShapeDtypeStruct((m, D_MODEL), BF16),
        compiler_params=_params("arbitrary", "arbitrary"),
        name="merge",
    )(ga, gb, gc, w_branch, mg, mg, mg)


def _out_kernel(a_ref, w_ref, x_ref, g_ref, *o_refs, final_norm):
    y = x_ref[...] + _dot(a_ref[...], w_ref[...])
    ms = jnp.mean(y * y, axis=-1, keepdims=True)
    normed = y * lax.rsqrt(ms + RMS_EPS) * g_ref[...]
    if final_norm:
        o_refs[0][...] = normed
    else:
        o_refs[0][...] = y
        o_refs[1][...] = normed.astype(o_refs[1].dtype)


def _out_proj(merged, w_out, layer, x2d, gain, final_norm):
    m, d = x2d.shape
    tm = min(ROW_TM, m)
    row_spec = pl.BlockSpec((tm, d), lambda i: (i, 0))
    out_specs, out_shape = [row_spec], [jax.ShapeDtypeStruct((m, d), F32)]
    if not final_norm:
        out_specs.append(row_spec)
        out_shape.append(jax.ShapeDtypeStruct((m, d), BF16))
    return pl.pallas_call(
        functools.partial(_out_kernel, final_norm=final_norm),
        grid=(m // tm,),
        in_specs=[row_spec, pl.BlockSpec((None, d, d), lambda i: (layer, 0, 0)), row_spec,
                  pl.BlockSpec((1, d), lambda i: (0, 0))],
        out_specs=out_specs,
        out_shape=out_shape,
        compiler_params=_params("parallel"),
        name="out_proj",
    )(merged, w_out, x2d, gain.reshape(1, d))


def _layer_vectors(l, pool_scale, rwkv_mu, rwkv_w0, rwkv_w_up, rwkv_a0, rwkv_a_up, rwkv_k_k, rwkv_k_a, rwkv_r_k,
                   rwkv_gn_w, rwkv_gn_b, attn_sink):
    w_lora = jnp.zeros((LANES, 2 * BRANCH_WIDTH), F32)
    w_lora = w_lora.at[:LORA_RANK, :BRANCH_WIDTH].set(rwkv_w_up[l]).at[LORA_RANK:, BRANCH_WIDTH:].set(rwkv_a_up[l])
    vecs = jnp.stack([rwkv_w0[l], rwkv_a0[l], rwkv_k_k[l], rwkv_k_a[l], rwkv_r_k[l], rwkv_gn_w[l], rwkv_gn_b[l],
                      jnp.zeros_like(rwkv_w0[l])])
    return dict(pool_scale=pool_scale[l].reshape(1, -1), mu=rwkv_mu[l].reshape(1, -1), vecs=vecs,
                w_lora=w_lora.astype(BF16), sink=attn_sink[l][np.array(ATT_HEAD_ORDER)])


def _sink_rows(sink_ordered, n_q):
    rows = jnp.repeat(sink_ordered.reshape(-1, 4), n_q, axis=1)
    return jnp.broadcast_to(rows[..., None], rows.shape + (LANES,))


def _mixer_layer(x2, h, b, l, wts, lv, next_gain, final_norm, *, pos0, pool_hist16, shift_prev, s0_bd, bias, caches):
    t = x2.shape[0] // b
    proj = {name: _matmul(h, wts["w_in"], l, col0, n, tn=tn, tm=tm, act=act, out_dtype=dt, name="proj_" + name)
            for name, (col0, n, tn, tm, act, dt) in PROJ_SEGMENTS.items()}
    u_pool, p_shift, q, kv, gate = (proj[name].reshape(b, t, -1) for name in ("pool", "shift", "q", "kv", "gate"))
    mg = proj["merge"]

    ga = _pool_branch(u_pool, pool_hist16, gate, wts["pool_w"], l, lv["pool_scale"], pos0)

    t_pad = -(-t // CHUNK) * CHUNK
    if t_pad != t:
        p_in = jnp.pad(p_shift, ((0, 0), (0, t_pad - t), (0, 0)))
        gate_in = jnp.pad(gate, ((0, 0), (0, t_pad - t), (0, 0)))
    else:
        p_in, gate_in = p_shift, gate
    gb, s_bd = _rwkv_branch(p_in, shift_prev, s0_bd, gate_in, lv["mu"], lv["vecs"], lv["w_lora"],
                            n_valid=CHUNK - (t_pad - t))
    gb = gb[:, :t]

    if caches is None:
        gc = _attn_prompt(q, kv, bias, _sink_rows(lv["sink"], CHUNK), gate)
    else:
        gc = _attn_sample(q, kv, caches[0], caches[1], bias, _sink_rows(lv["sink"], t), gate)

    merged = _merge(ga.reshape(b * t, -1), gb.reshape(b * t, -1), gc.reshape(b * t, -1), wts["w_branch"], l, mg)
    outs = _out_proj(merged, wts["w_out"], l, x2, next_gain, final_norm)
    y, h_next = (outs[0], None) if final_norm else outs
    return (y, h_next, kv[..., :KV_WIDTH], kv[..., KV_WIDTH:], _state_from_blockdiag(s_bd), p_shift[:, -1],
            u_pool[:, t - POOL_HIST:])


def kernel(x_prompt, x_sample, cache_attn_k, cache_attn_v, state_rwkv, state_rwkv_shift, state_pool, norm_g, w_in,
           pool_w, pool_scale, rwkv_mu, rwkv_w0, rwkv_w_up, rwkv_a0, rwkv_a_up, rwkv_k_k, rwkv_k_a, rwkv_r_k,
           rwkv_gn_w, rwkv_gn_b, attn_sink, rel_pos_table, w_branch, w_out, final_norm_g):
    depth = norm_g.shape[0]
    bp, tp, _ = x_prompt.shape
    bs, ts, _ = x_sample.shape
    n_cache = cache_attn_k.shape[2]

    key_p = jnp.arange((BAND_CHUNKS + 1) * CHUNK) - BAND_CHUNKS * CHUNK
    bias_p = _rel_bias(rel_pos_table, key_p, CHUNK)
    key_s = jnp.concatenate([jnp.arange(n_cache) - n_cache, jnp.arange(ts)])
    bias_s = _rel_bias(rel_pos_table, key_s, ts)

    wts = dict(w_in=w_in.astype(BF16), pool_w=pool_w.astype(BF16), w_branch=w_branch.astype(BF16),
               w_out=w_out.astype(BF16))
    xp, xs = x_prompt.reshape(bp * tp, -1), x_sample.reshape(bs * ts, -1)
    hp = _rmsnorm(xp, norm_g[0], BF16, ROW_TM)
    hs = _rmsnorm(xs, norm_g[0], BF16, ROW_TM)
    outs_p, outs_s = [], []
    for l in range(depth):
        lv = _layer_vectors(l, pool_scale, rwkv_mu, rwkv_w0, rwkv_w_up, rwkv_a0, rwkv_a_up, rwkv_k_k, rwkv_k_a,
                            rwkv_r_k, rwkv_gn_w, rwkv_gn_b, attn_sink)
        last = l == depth - 1
        next_gain = final_norm_g if last else norm_g[l + 1]
        xp, hp, kp, vp, sp, shp, plp = _mixer_layer(
            xp, hp, bp, l, wts, lv, next_gain, last, pos0=0,
            pool_hist16=jnp.zeros((bp, 16, BRANCH_WIDTH), F32),
            shift_prev=jnp.zeros((bp, 1, SHIFT_WIDTH), F32),
            s0_bd=jnp.zeros((bp, PAIRS, LANES, LANES), F32),
            bias=bias_p, caches=None)
        outs_p.append((kp[:, -WINDOW:].reshape(bp, WINDOW, ATT_KV_HEADS, HEAD_DIM),
                       vp[:, -WINDOW:].reshape(bp, WINDOW, ATT_KV_HEADS, HEAD_DIM), sp, shp, plp))
        xs, hs, k_s, v_s, s_s, sh_s, pl_s = _mixer_layer(
            xs, hs, bs, l, wts, lv, next_gain, last, pos0=PAST_LEN,
            pool_hist16=jnp.pad(state_pool[l], ((0, 0), (1, 0), (0, 0))),
            shift_prev=state_rwkv_shift[l][:, None, :],
            s0_bd=_state_to_blockdiag(state_rwkv[l]),
            bias=bias_s,
            caches=(cache_attn_k[l].reshape(bs, n_cache, KV_WIDTH), cache_attn_v[l].reshape(bs, n_cache, KV_WIDTH)))
        outs_s.append((k_s.reshape(bs, ts, ATT_KV_HEADS, HEAD_DIM), v_s.reshape(bs, ts, ATT_KV_HEADS, HEAD_DIM),
                       s_s, sh_s, pl_s))

    st = lambda outs, i: jnp.stack([o[i] for o in outs])
    return (xp.reshape(x_prompt.shape), xs.reshape(x_sample.shape), st(outs_p, 0), st(outs_p, 1), st(outs_s, 0), st(outs_s, 1),
            st(outs_p, 2), st(outs_s, 2), st(outs_p, 3), st(outs_s, 3), st(outs_p, 4), st(outs_s, 4))
```

```python
import functools
import math

import jax
import jax.numpy as jnp
import numpy as np
from jax import lax
from jax.experimental import pallas as pl
from jax.experimental.pallas import tpu as pltpu

F32 = jnp.float32
BF16 = jnp.bfloat16

D_MODEL = 2048
BRANCH_WIDTH = 1024
POOL_WINDOWS = (2, 4, 8, 16)
POOL_GROUP_DIM = BRANCH_WIDTH // len(POOL_WINDOWS)
POOL_HIST = max(POOL_WINDOWS) - 1
HEAD_DIM = 64
RWKV_HEADS = BRANCH_WIDTH // HEAD_DIM
LORA_RANK = 64
SHIFT_WIDTH = 3 * BRANCH_WIDTH + 2 * LORA_RANK
GN_EPS = 64e-5
ATT_HEADS = 16
ATT_KV_HEADS = 4
ATT_GROUP = ATT_HEADS // ATT_KV_HEADS
KV_WIDTH = ATT_KV_HEADS * HEAD_DIM
CHUNK = 64
WINDOW = 128
BAND_CHUNKS = WINDOW // CHUNK
N_BUCKETS = 32
MAX_DISTANCE = 128
NEG_INF = -1e30
RMS_EPS = 1e-6
PAST_LEN = 2048
N_BRANCH = 3
OFF_POOL = 0
OFF_SHIFT = OFF_POOL + BRANCH_WIDTH
OFF_Q = OFF_SHIFT + SHIFT_WIDTH
OFF_K = OFF_Q + BRANCH_WIDTH
OFF_V = OFF_K + KV_WIDTH
OFF_GATE = OFF_V + KV_WIDTH
OFF_MERGE = OFF_GATE + N_BRANCH * BRANCH_WIDTH

LANES = 128
PAIRS = BRANCH_WIDTH // LANES
ATTN_CHUNKS_PER_STEP = 8
RWKV_CHUNKS_PER_STEP = 4
RWKV_GROUP = 8
KEY_PAD = 256
ATT_HEAD_ORDER = tuple(8 * m + o for m in range(ATT_KV_HEADS // 2) for o in (0, 2, 5, 7, 1, 3, 4, 6))
POOL_FUSED_TM = 1024
ROW_TM = 512
MERGE_TM, MERGE_TN = 1024, 1024
PROJ_SEGMENTS = {
    "pool": (OFF_POOL, BRANCH_WIDTH, 1024, 1024, None, F32),
    "shift": (OFF_SHIFT, SHIFT_WIDTH, SHIFT_WIDTH, 512, None, F32),
    "q": (OFF_Q, BRANCH_WIDTH, 1024, 1024, "att_scale", BF16),
    "kv": (OFF_K, 2 * KV_WIDTH, 2 * KV_WIDTH, 1024, None, F32),
    "gate": (OFF_GATE, N_BRANCH * BRANCH_WIDTH, 3072, 1024, "silu", BF16),
    "merge": (OFF_MERGE, N_BRANCH * D_MODEL, 3072, 1024, "sigmoid", BF16),
}
VMEM_LIMIT = 56 * 1024 * 1024


def _params(*sem):
    return pltpu.CompilerParams(dimension_semantics=sem, vmem_limit_bytes=VMEM_LIMIT)


def _sigmoid(x):
    return 1.0 / (1.0 + jnp.exp(-x))


def _dot(a, b):
    return jnp.dot(a, b, preferred_element_type=F32)


def _dot_nt(a, b):
    return lax.dot_general(a, b, (((1,), (1,)), ((), ())), preferred_element_type=F32)


def _split3(x):
    hi = x.astype(BF16)
    r1 = x - hi.astype(F32)
    mid = r1.astype(BF16)
    lo = (r1 - mid.astype(F32)).astype(BF16)
    return hi, mid, lo


def _dot3_right_exact(x, b):
    hi, mid, lo = _split3(x)
    return _dot(hi, b) + _dot(mid, b) + _dot(lo, b)


def _dot3_left_exact(a, x):
    hi, mid, lo = _split3(x)
    return _dot(a, hi) + _dot(a, mid) + _dot(a, lo)


def _rmsnorm_kernel(x_ref, g_ref, o_ref):
    x = x_ref[...]
    ms = jnp.mean(x * x, axis=-1, keepdims=True)
    o_ref[...] = (x * lax.rsqrt(ms + RMS_EPS) * g_ref[...]).astype(o_ref.dtype)


def _rmsnorm(x2d, g, out_dtype, tm):
    m, d = x2d.shape
    tm = min(tm, m)
    return pl.pallas_call(
        _rmsnorm_kernel,
        grid=(m // tm,),
        in_specs=[pl.BlockSpec((tm, d), lambda i: (i, 0)), pl.BlockSpec((1, d), lambda i: (0, 0))],
        out_specs=pl.BlockSpec((tm, d), lambda i: (i, 0)),
        out_shape=jax.ShapeDtypeStruct((m, d), out_dtype),
        compiler_params=_params("parallel"),
        name="rmsnorm",
    )(x2d, g.reshape(1, d))


def _mm_kernel(a_ref, w_ref, o_ref, *, act):
    acc = _dot(a_ref[...], w_ref[0])
    if act == "silu":
        acc = acc * _sigmoid(acc)
    elif act == "sigmoid":
        acc = _sigmoid(acc)
    elif act == "att_scale":
        acc = acc * (HEAD_DIM ** -0.5)
    o_ref[...] = acc.astype(o_ref.dtype)


def _matmul(a, w, layer, col0, n, *, tn, tm, act, out_dtype, name):
    m, k = a.shape
    tm = min(tm, m)
    return pl.pallas_call(
        functools.partial(_mm_kernel, act=act),
        grid=(n // tn, m // tm),
        in_specs=[pl.BlockSpec((tm, k), lambda j, i: (i, 0)),
                  pl.BlockSpec((pl.Element(1), pl.Element(k), pl.Element(tn)),
                               lambda j, i: (layer, 0, pl.multiple_of(col0 + j * tn, LANES)))],
        out_specs=pl.BlockSpec((tm, tn), lambda j, i: (i, j)),
        out_shape=jax.ShapeDtypeStruct((m, n), out_dtype),
        compiler_params=_params("arbitrary", "arbitrary"),
        name=name,
    )(a, w)


def _pool_mix(u, prev, pos, zg_ref, w_ref, scale_ref, o_ref):
    for g, w in enumerate(POOL_WINDOWS):
        sl = slice(g * POOL_GROUP_DIM, (g + 1) * POOL_GROUP_DIM)
        e = jnp.concatenate([prev[:, sl], u[:, sl]], axis=0)
        step = 1
        while step < w:
            e = e + pltpu.roll(e, step, 0)
            step *= 2
        s = e[16:, :]
        cnt = jnp.minimum(w, pos + 1).astype(F32)
        pooled = s / cnt - u[:, sl]
        o = _dot(pooled.astype(BF16), w_ref[g]) * scale_ref[:, sl]
        o_ref[:, sl] = (o * zg_ref[:, sl].astype(F32)).astype(o_ref.dtype)


def _pool_kernel(u_ref, prev_ref, hist_ref, zg_ref, w_ref, scale_ref, o_ref, *, tt, pos0):
    i = pl.program_id(1)
    prev = jnp.where(i == 0, hist_ref[0], prev_ref[0])
    pos = pos0 + i * tt + lax.broadcasted_iota(jnp.int32, (tt, POOL_GROUP_DIM), 0)
    _pool_mix(u_ref[0], prev, pos, zg_ref.at[0], w_ref, scale_ref, o_ref.at[0])


def _proj_pool_kernel(a_ref, w_ref, zg_ref, pw_ref, scale_ref, u_ref, o_ref, carry_ref, *, tm, tiles_per_seq):
    step = lax.rem(pl.program_id(0), tiles_per_seq)
    u = _dot(a_ref[...], w_ref[0])
    u_ref[...] = u
    prev = jnp.where(step == 0, 0.0, carry_ref[...])
    carry_ref[...] = u[tm - 16:, :]
    pos = step * tm + lax.broadcasted_iota(jnp.int32, (tm, POOL_GROUP_DIM), 0)
    _pool_mix(u, prev, pos, zg_ref, pw_ref, scale_ref, o_ref)


def _proj_pool(h, w_in, layer, gate2d, pool_w, pool_scale, seq_len):
    m, k = h.shape
    c = BRANCH_WIDTH
    tm = min(POOL_FUSED_TM, seq_len)
    return pl.pallas_call(
        functools.partial(_proj_pool_kernel, tm=tm, tiles_per_seq=seq_len // tm),
        grid=(m // tm,),
        in_specs=[
            pl.BlockSpec((tm, k), lambda i: (i, 0)),
            pl.BlockSpec((pl.Element(1), pl.Element(k), pl.Element(c)), lambda i: (layer, 0, OFF_POOL)),
            pl.BlockSpec((tm, c), lambda i: (i, 0)),
            pl.BlockSpec((None,) + pool_w.shape[1:], lambda i: (layer, 0, 0, 0)),
            pl.BlockSpec((1, c), lambda i: (0, 0)),
        ],
        out_specs=[pl.BlockSpec((tm, c), lambda i: (i, 0)), pl.BlockSpec((tm, c), lambda i: (i, 0))],
        out_shape=[jax.ShapeDtypeStruct((m, c), F32), jax.ShapeDtypeStruct((m, c), BF16)],
        scratch_shapes=[pltpu.VMEM((16, c), F32)],
        compiler_params=_params("arbitrary"),
        name="proj_pool_fused",
    )(h, w_in, gate2d, pool_w, pool_scale)


def _pool_branch(u, hist16, gate, pool_w, layer, pool_scale, pos0):
    b, t, c = u.shape
    tt = min(t, 512)
    sub = tt // 16
    return pl.pallas_call(
        functools.partial(_pool_kernel, tt=tt, pos0=pos0),
        grid=(b, t // tt),
        in_specs=[
            pl.BlockSpec((1, tt, c), lambda bi, i: (bi, i, 0)),
            pl.BlockSpec((1, 16, c), lambda bi, i: (bi, jnp.maximum(i * sub - 1, 0), 0)),
            pl.BlockSpec((1, 16, c), lambda bi, i: (bi, 0, 0)),
            pl.BlockSpec((1, tt, c), lambda bi, i: (bi, i, 0)),
            pl.BlockSpec((None,) + pool_w.shape[1:], lambda bi, i: (layer, 0, 0, 0)),
            pl.BlockSpec((1, c), lambda bi, i: (0, 0)),
        ],
        out_specs=pl.BlockSpec((1, tt, c), lambda bi, i: (bi, i, 0)),
        out_shape=jax.ShapeDtypeStruct((b, t, c), BF16),
        compiler_params=_params("parallel", "parallel"),
        name="pool_branch",
    )(u, u, hist16, gate, pool_w, pool_scale)


def _rwkv_kernel(p_ref, sprev_ref, s0_ref, zg_ref, mu_ref, vec_ref, wl_ref, o_ref, s_ref, carry_ref,
                 *, n_valid, n_sub):
    @pl.when(pl.program_id(1) == 0)
    def _():
        s_ref[...] = s0_ref[...]
        carry_ref[...] = sprev_ref[0]

    for sub in range(n_sub):
        rows = pl.ds(sub * CHUNK, CHUNK)
        _rwkv_chunk(p_ref.at[:, rows, :], zg_ref.at[:, rows, :], mu_ref, vec_ref, wl_ref, o_ref.at[:, rows, :],
                    s_ref, carry_ref, n_valid=n_valid)


def _rwkv_chunk(p_ref, zg_ref, mu_ref, vec_ref, wl_ref, o_ref, s_ref, carry_ref, *, n_valid):
    C = CHUNK
    p = p_ref[0]
    row_w = lax.broadcasted_iota(jnp.int32, (C, SHIFT_WIDTH), 0)
    row = lax.broadcasted_iota(jnp.int32, (C, BRANCH_WIDTH), 0)
    prev = jnp.where(row_w == 0, carry_ref[...], pltpu.roll(p, 1, 0))
    carry_ref[...] = p[C - 1:C, :]
    xs = p + (prev - p) * mu_ref[...]

    w0, a0, k_k, k_a, r_k, gn_w, gn_b = (vec_ref[i:i + 1, :] for i in range(7))
    lane = lax.broadcasted_iota(jnp.int32, (C, LANES), 1)
    head0 = lane < HEAD_DIM
    lin = xs[:, 3 * BRANCH_WIDTH:]
    lin = jnp.where(head0, jnp.tanh(lin), lin)
    lora = _dot(lin.astype(BF16), wl_ref[...])
    log_decay = -math.exp(-0.5) * _sigmoid(w0 + lora[:, :BRANCH_WIDTH])
    iclr = _sigmoid(a0 + lora[:, BRANCH_WIDTH:])
    r = xs[:, :BRANCH_WIDTH]
    k = xs[:, BRANCH_WIDTH:2 * BRANCH_WIDTH]
    v = xs[:, 2 * BRANCH_WIDTH:3 * BRANCH_WIDTH]
    kk_raw = k * k_k
    k2 = k * (1.0 + (iclr - 1.0) * k_a)
    if n_valid < C:
        ok = row < n_valid
        log_decay = jnp.where(ok, log_decay, 0.0)
        kk_raw = jnp.where(ok, kk_raw, 0.0)
        k2 = jnp.where(ok, k2, 0.0)
        r = jnp.where(ok, r, 0.0)
        v = jnp.where(ok, v, 0.0)

    ti = lax.broadcasted_iota(jnp.int32, (C, C), 0)
    si = lax.broadcasted_iota(jnp.int32, (C, C), 1)
    tri = (si <= ti).astype(BF16)
    cum = _dot3_left_exact(tri, log_decay)
    cum_x = cum - log_decay
    cum_c = cum[C - 1:C, :]

    ri =lax.broadcasted_iota(jnp.int32, (2 * C, 2 * C), 0)
    ci = lax.broadcasted_iota(jnp.int32, (2 * C, 2 * C), 1)
    same_blk = (ri >= C) == (ci >= C)
    strict = same_blk & ((ci & (C - 1)) < (ri & (C - 1)))
    incl = same_blk & ((ci & (C - 1)) <= (ri & (C - 1)))

    def stack(x):
        return jnp.concatenate([jnp.where(head0, x, 0.0), jnp.where(head0, 0.0, x)], axis=0)

    def each(f, *cols):
        return [f(*xs) for xs in zip(*cols)]

    def bf(xs):
        return [x.astype(BF16) for x in xs]

    def head_sums(xs):
        lo = [jnp.sum(jnp.where(head0, x, 0.0), axis=-1, keepdims=True) for x in xs]
        hi = [jnp.sum(jnp.where(head0, 0.0, x), axis=-1, keepdims=True) for x in xs]
        return [jnp.where(head0, a, b) for a, b in zip(lo, hi)]

    def stack_b(x):
        xb = x.astype(BF16)
        zero = jnp.zeros((), BF16)
        return jnp.concatenate([jnp.where(head0, xb, zero), jnp.where(head0, zero, xb)], axis=0)

    def unfold(x, keep):
        xr = pltpu.roll(x, C, 1)
        top, bot = keep[:C], keep[C:]
        from_p = jnp.concatenate([jnp.where(top, x[:C], 0.0), jnp.where(bot, xr[C:], 0.0)], axis=0)
        from_q = jnp.concatenate([jnp.where(top, xr[:C], 0.0), jnp.where(bot, x[C:], 0.0)], axis=0)
        return from_p, from_q

    for g0 in range(0, PAIRS, RWKV_GROUP):
        prs = list(range(g0, g0 + RWKV_GROUP))
        sls = [slice(pr * LANES, (pr + 1) * LANES) for pr in prs]
        r_p = [r[:, sl] for sl in sls]
        k_p = [k2[:, sl] for sl in sls]
        v_p = [v[:, sl] for sl in sls]
        kkr = [kk_raw[:, sl] for sl in sls]
        ic = [iclr[:, sl] for sl in sls]
        cm = [cum[:, sl] for sl in sls]
        cx = [cum_x[:, sl] for sl in sls]
        cc = [cum_c[:, sl] for sl in sls]
        n2 = head_sums(each(lambda x: x * x, kkr))
        kk = each(lambda x, n: x * lax.rsqrt(jnp.maximum(n, 1e-24)), kkr, n2)
        b_p = each(lambda x, i: x * i, kk, ic)
        e_neg = each(lambda c_: jnp.exp(-c_), cm)
        e_end = each(lambda c_, e_: jnp.exp(e_ - c_), cm, cc)
        w_c = each(jnp.exp, cc)
        rt = each(lambda x, c_: x * jnp.exp(c_), r_p, cm)
        at_b = each(lambda x, c_: stack_b(-x * jnp.exp(c_)), kk, cx)
        rt_b = each(stack_b, rt)
        v_s = each(stack, v_p)
        v_b = bf(v_s)
        bb_b = each(lambda x, e_: stack_b(x * e_), b_p, e_end)
        kb_b = each(lambda x, e_: stack_b(x * e_), k_p, e_end)
        bk_b = bf(each(lambda b_, k_, e_: jnp.concatenate([b_ * e_, k_ * e_], axis=0), b_p, k_p, e_neg))
        ar_b = each(lambda a_, r_: jnp.concatenate([a_, r_], axis=0), at_b, rt_b)
        prod = each(_dot_nt, ar_b, bk_b)
        a_pair = each(lambda x: unfold(x[:2 * C], strict), prod)
        r_pair = each(lambda x: unfold(x[2 * C:], incl), prod)
        a_ab = [x[0] for x in a_pair]
        a_ak_b = bf([x[1] for x in a_pair])
        a_rb_b = bf([x[0] for x in r_pair])
        a_rk_b = bf([x[1] for x in r_pair])
        tinv = each(lambda x: jnp.where(ri == ci, 1.0, x), a_ab)
        pw_b = bf(a_ab)
        pw_b = bf(each(_dot, pw_b, pw_b))
        n_sq = int(math.log2(C)) - 1
        for step in range(n_sq):
            if step < n_sq - 1:
                both = each(lambda t_, p_: _dot(p_, jnp.concatenate([t_.astype(BF16), p_], axis=1)), tinv, pw_b)
                tinv = each(lambda t_, x: t_ + x[:, :2 * C], tinv, both)
                pw_b = bf([x[:, 2 * C:] for x in both])
            else:
                tinv = each(lambda t_, p_: t_ + _dot(p_, t_.astype(BF16)), tinv, pw_b)
        t_b = bf(tinv)
        akv = each(_dot, a_ak_b, v_b)
        au = each(lambda t_, a_, k_: _dot(t_, jnp.concatenate([a_, k_.astype(BF16)], axis=1)), t_b, at_b, akv)
        ry = each(_dot, a_rb_b, bf(au))
        rh = each(lambda x, y_: x + y_[:C, :LANES] + y_[C:, :LANES], rt, ry)
        y0 = each(lambda y_, k_, v_: y_[:, LANES:] + _dot(k_, v_), ry, a_rk_b, v_b)
        mg = each(lambda x, b_: _dot(x.T.astype(BF16), b_), au, bb_b)
        m_lr = [x[:LANES] for x in mg]
        g_bd = each(lambda x, v_, k_: x[LANES:] + _dot(v_.T.astype(BF16), k_), mg, v_s, kb_b)
        s_bd = [s_ref[0, pr] for pr in prs]
        s_b = bf(s_bd)
        y = each(lambda x, s_, y_: _dot_nt(x.astype(BF16), s_) + (y_[:C] + y_[C:]), rh, s_b, y0)
        s_new = each(lambda s_, w_, sb_, m_, g_: s_ * w_ + _dot(sb_, m_.astype(BF16)) + g_,
                     s_bd, w_c, s_b, m_lr, g_bd)
        for pr, s_ in zip(prs, s_new):
            s_ref[0, pr] = s_
        yc = each(lambda x, m_: x - m_ * (1.0 / HEAD_DIM), y, head_sums(y))
        var = head_sums(each(lambda x: x * x, yc))
        rk = head_sums(each(lambda r_, k_, sl: r_ * k_ * r_k[:, sl], r_p, k_p, sls))
        for sl, yc_, var_, rk_, v_ in zip(sls, yc, var, rk, v_p):
            yn = yc_ * lax.rsqrt(var_ * (1.0 / HEAD_DIM) + GN_EPS) * gn_w[:, sl] + gn_b[:, sl]
            o_ref[0, :, sl] = ((yn + rk_ * v_) * zg_ref[0, :, sl].astype(F32)).astype(o_ref.dtype)


def _rwkv_branch(p_shift, shift_prev, s0_bd, gate, mu, vecs, w_lora, n_valid):
    b, t, _ = p_shift.shape
    n_sub = math.gcd(t // CHUNK, RWKV_CHUNKS_PER_STEP)
    rows = n_sub * CHUNK
    return pl.pallas_call(
        functools.partial(_rwkv_kernel, n_valid=n_valid, n_sub=n_sub),
        grid=(b, t // rows),
        in_specs=[
            pl.BlockSpec((1, rows, SHIFT_WIDTH), lambda bi, c: (bi, c, 0)),
            pl.BlockSpec((1, 1, SHIFT_WIDTH), lambda bi, c: (bi, 0, 0)),
            pl.BlockSpec((1, PAIRS, LANES, LANES), lambda bi, c: (bi, 0, 0, 0)),
            pl.BlockSpec((1, rows, BRANCH_WIDTH), lambda bi, c: (bi, c, 1)),
            pl.BlockSpec((1, SHIFT_WIDTH), lambda bi, c: (0, 0)),
            pl.BlockSpec((8, BRANCH_WIDTH), lambda bi, c: (0, 0)),
            pl.BlockSpec((LANES, 2 * BRANCH_WIDTH), lambda bi, c: (0, 0)),
        ],
        out_specs=[
            pl.BlockSpec((1, rows, BRANCH_WIDTH), lambda bi, c: (bi, c, 0)),
            pl.BlockSpec((1, PAIRS, LANES, LANES), lambda bi, c: (bi, 0, 0, 0)),
        ],
        out_shape=[
            jax.ShapeDtypeStruct((b, t, BRANCH_WIDTH), BF16),
            jax.ShapeDtypeStruct((b, PAIRS, LANES, LANES), F32),
        ],
        scratch_shapes=[pltpu.VMEM((1, SHIFT_WIDTH), F32)],
        compiler_params=_params("parallel", "arbitrary"),
        name="rwkv_branch",
    )(p_shift, shift_prev, s0_bd, gate, mu, vecs, w_lora)


def _state_to_blockdiag(s):
    b = s.shape[0]
    sp = s.reshape(b, PAIRS, 2, HEAD_DIM, HEAD_DIM)
    z = jnp.zeros_like(sp[:, :, 0])
    top = jnp.concatenate([sp[:, :, 0], z], axis=-1)
    bot = jnp.concatenate([z, sp[:, :, 1]], axis=-1)
    return jnp.concatenate([top, bot], axis=-2)


def _state_from_blockdiag(s_bd):
    b = s_bd.shape[0]
    h0 = s_bd[:, :, :HEAD_DIM, :HEAD_DIM]
    h1 = s_bd[:, :, HEAD_DIM:, HEAD_DIM:]
    return jnp.stack([h0, h1], axis=2).reshape(b, RWKV_HEADS, HEAD_DIM, HEAD_DIM)


def _attn_kernel(*refs, n_kv_parts, tq, n_keys, masked, n_sub):
    q_ref = refs[0]
    k_refs = refs[1:1 + n_kv_parts]
    v_refs = refs[1 + n_kv_parts:1 + 2 * n_kv_parts]
    bias_ref, sink_ref, zg_ref, o_ref = refs[1 + 2 * n_kv_parts:]
    k_rows = jnp.concatenate([kr[0] for kr in k_refs], axis=0)
    v_rows = jnp.concatenate([vr[0] for vr in v_refs], axis=0)
    for sub in range(n_sub):
        rows = pl.ds(sub * tq, tq)
        _attn_chunk(q_ref.at[:, rows, :], k_rows[sub * tq:sub * tq + n_keys], v_rows[sub * tq:sub * tq + n_keys],
                    bias_ref, sink_ref, zg_ref.at[:, rows, :], o_ref.at[:, rows, :],
                    n=pl.program_id(1) * n_sub + sub, tq=tq, n_keys=n_keys, masked=masked)


def _attn_chunk(q_ref, k_seen, v_seen, bias_ref, sink_ref, zg_ref, o_ref, *, n, tq, n_keys, masked):
    pad = jnp.zeros((KEY_PAD - n_keys, KV_WIDTH), F32)
    k_all = jnp.concatenate([k_seen, pad], axis=0)
    v_all = jnp.concatenate([v_seen, pad], axis=0)
    lane =lax.broadcasted_iota(jnp.int32, (tq, LANES), 1)
    head0 = lane < HEAD_DIM
    col = lax.broadcasted_iota(jnp.int32, (1, KEY_PAD), 1)
    valid = col < n_keys
    if masked:
        valid = valid & (col // CHUNK + n - BAND_CHUNKS >= 0)
    q = q_ref[0]
    zero = jnp.zeros((), q.dtype)
    lo_half = lambda s: jnp.where(head0, s, zero)
    hi_half = lambda s: jnp.where(head0, zero, s)
    lhs, keys, vals = [], [], []
    for m in range(ATT_KV_HEADS // 2):
        slabs = [q[:, (4 * m + i) * LANES:(4 * m + i + 1) * LANES] for i in range(4)]
        k_sl = k_all[:, m * LANES:(m + 1) * LANES]
        v_sl = v_all[:, m * LANES:(m + 1) * LANES]
        lhs.append(jnp.concatenate([lo_half(slabs[0]), lo_half(slabs[1]), hi_half(slabs[2]), hi_half(slabs[3])], 0))
        lhs.append(jnp.concatenate([hi_half(slabs[0]), hi_half(slabs[1]), lo_half(slabs[2]), lo_half(slabs[3])], 0))
        keys += [k_sl.astype(BF16), pltpu.roll(k_sl, HEAD_DIM, 1).astype(BF16)]
        vals += [v_sl.astype(BF16), pltpu.roll(v_sl, HEAD_DIM, 1).astype(BF16)]
    n_grp = len(lhs)
    ones = jnp.ones((KEY_PAD, LANES), BF16)
    vals = [jnp.concatenate([x, ones], axis=1) for x in vals]
    logits = [jnp.where(valid, _dot_nt(lhs[i], keys[i]) + bias_ref[i], NEG_INF) for i in range(n_grp)]
    sink = [sink_ref[i] for i in range(n_grp)]
    mx = [jnp.maximum(jnp.max(logits[i], axis=-1, keepdims=True), sink[i]) for i in range(n_grp)]
    e = [jnp.exp(jnp.concatenate([logits[i][:, :LANES] - mx[i], logits[i][:, LANES:] - mx[i]], axis=1))
         for i in range(n_grp)]
    pv = [_dot(e[i].astype(BF16), vals[i]) for i in range(n_grp)]
    out = [pv[i][:, :LANES] / (pv[i][:, LANES:] + jnp.exp(sink[i] - mx[i])) for i in range(n_grp)]
    for m in range(ATT_KV_HEADS // 2):
        same, swap = out[2 * m], out[2 * m + 1]
        for i in range(4):
            a, b = same[i * tq:(i + 1) * tq], swap[i * tq:(i + 1) * tq]
            res = jnp.where(head0, a, b) if i < 2 else jnp.where(head0, b, a)
            sl = slice((4 * m + i) * LANES, (4 * m + i + 1) * LANES)
            o_ref[0, :, sl] = (res * zg_ref[0, :, sl].astype(F32)).astype(o_ref.dtype)


def _attn_prompt(q, kv, bias, sink_rows, gate):
    b, t, _ = q.shape
    nc = t // CHUNK
    n_sub = math.gcd(nc, ATTN_CHUNKS_PER_STEP)
    rows = n_sub * CHUNK
    k_specs, v_specs = ([pl.BlockSpec((1, CHUNK, KV_WIDTH),
                                      lambda bi, n, d=d, c=c: (bi, jnp.maximum(n * n_sub - d, 0), c))
                         for d in range(BAND_CHUNKS, 0, -1)]
                        + [pl.BlockSpec((1, rows, KV_WIDTH), lambda bi, n, c=c: (bi, n, c))] for c in range(2))
    return pl.pallas_call(
        functools.partial(_attn_kernel, n_kv_parts=BAND_CHUNKS + 1, tq=CHUNK,
                          n_keys=(BAND_CHUNKS + 1) * CHUNK, masked=True, n_sub=n_sub),
        grid=(b, nc // n_sub),
        in_specs=[pl.BlockSpec((1, rows, BRANCH_WIDTH), lambda bi, n: (bi, n, 0))]
        + k_specs + v_specs
        + [pl.BlockSpec(bias.shape, lambda bi, n: (0, 0, 0)),
           pl.BlockSpec(sink_rows.shape, lambda bi, n: (0, 0, 0)),
           pl.BlockSpec((1, rows, BRANCH_WIDTH), lambda bi, n: (bi, n, 2))],
        out_specs=pl.BlockSpec((1, rows, BRANCH_WIDTH), lambda bi, n: (bi, n, 0)),
        out_shape=jax.ShapeDtypeStruct((b, t, BRANCH_WIDTH), BF16),
        compiler_params=_params("parallel", "parallel"),
        name="attn_prompt",
    )(q, kv, kv, kv, kv, kv, kv, bias, sink_rows, gate)


def _attn_sample(q, kv, k_cache, v_cache, bias, sink_rows, gate):
    b, t, _ = q.shape
    n_cache = k_cache.shape[1]
    cache_spec = pl.BlockSpec((1, n_cache, KV_WIDTH), lambda bi, n: (bi, 0, 0))
    return pl.pallas_call(
        functools.partial(_attn_kernel, n_kv_parts=2, tq=t, n_keys=n_cache + t, masked=False, n_sub=1),
        grid=(b, 1),
        in_specs=[pl.BlockSpec((1, t, BRANCH_WIDTH), lambda bi, n: (bi, 0, 0)),
                  cache_spec, pl.BlockSpec((1, t, KV_WIDTH), lambda bi, n: (bi, 0, 0)),
                  cache_spec, pl.BlockSpec((1, t, KV_WIDTH), lambda bi, n: (bi, 0, 1)),
                  pl.BlockSpec(bias.shape, lambda bi, n: (0, 0, 0)),
                  pl.BlockSpec(sink_rows.shape, lambda bi, n: (0, 0, 0)),
                  pl.BlockSpec((1, t, BRANCH_WIDTH), lambda bi, n: (bi, 0, 2))],
        out_specs=pl.BlockSpec((1, t, BRANCH_WIDTH), lambda bi, n: (bi, 0, 0)),
        out_shape=jax.ShapeDtypeStruct((b, t, BRANCH_WIDTH), BF16),
        compiler_params=_params("parallel", "parallel"),
        name="attn_sample",
    )(q, k_cache, kv, v_cache, kv, bias, sink_rows, gate)


def _t5_bucket(rel):
    half = N_BUCKETS // 2
    n = -rel
    ret = jnp.where(n < 0, half, 0)
    n = jnp.abs(n)
    max_exact = half // 2
    large = max_exact + (jnp.log(jnp.maximum(n, 1).astype(jnp.float32) / max_exact)
                         / math.log(MAX_DISTANCE / max_exact) * (half - max_exact)).astype(jnp.int32)
    large = jnp.minimum(large, half - 1)
    return ret + jnp.where(n < max_exact, n, large)


def _bias_kernel(tab_ref, onehot_ref, o_ref):
    o_ref[...] = _dot3_right_exact(tab_ref[...], onehot_ref[...])


def _rel_bias(table, key_pos, n_q):
    n_keys = key_pos.shape[0]
    bucket = _t5_bucket(key_pos[None, :] - jnp.arange(n_q)[:, None])
    bucket = jnp.pad(bucket, ((0, 0), (0, KEY_PAD - n_keys)), constant_values=-1)
    onehot = (bucket.reshape(1, -1) == jnp.arange(N_BUCKETS)[:, None]).astype(BF16)
    out = pl.pallas_call(
        _bias_kernel,
        out_shape=jax.ShapeDtypeStruct((ATT_HEADS, n_q * KEY_PAD), F32),
        name="rel_bias",
    )(table.T[np.array(ATT_HEAD_ORDER)], onehot)
    return out.reshape(len(ATT_HEAD_ORDER) // 4, 4 * n_q, KEY_PAD)


def _merge_kernel(ga_ref, gb_ref, gc_ref, w_ref, m0_ref, m1_ref, m2_ref, o_ref):
    acc = m0_ref[...].astype(F32) * _dot(ga_ref[...], w_ref[0])
    acc = acc + m1_ref[...].astype(F32) * _dot(gb_ref[...], w_ref[1])
    acc = acc + m2_ref[...].astype(F32) * _dot(gc_ref[...], w_ref[2])
    o_ref[...] = acc.astype(o_ref.dtype)


def _merge(ga, gb, gc, w_branch, layer, mg):
    m, kdim = ga.shape
    tm = min(MERGE_TM, m)
    tn = MERGE_TN
    nj = D_MODEL // tn
    g_spec = pl.BlockSpec((tm, kdim), lambda j, i: (i, 0))
    m_specs = [pl.BlockSpec((tm, tn), lambda j, i, br=br: (i, br * nj + j)) for br in range(N_BRANCH)]
    return pl.pallas_call(
        _merge_kernel,
        grid=(nj, m // tm),
        in_specs=[g_spec, g_spec, g_spec,
                  pl.BlockSpec((None, N_BRANCH, kdim, tn), lambda j, i: (layer, 0, 0, j))] + m_specs,
        out_specs=pl.BlockSpec((tm, tn), lambda j, i: (i, j)),
        out_shape=jax.ShapeDtypeStruct((m, D_MODEL), BF16),
        compiler_params=_params("arbitrary", "arbitrary"),
        name="merge",
    )(ga, gb, gc, w_branch, mg, mg, mg)


def _out_kernel(a_ref, w_ref, x_ref, g_ref, *o_refs, final_norm):
    y = x_ref[...] + _dot(a_ref[...], w_ref[...])
    ms = jnp.mean(y * y, axis=-1, keepdims=True)
    normed = y * lax.rsqrt(ms + RMS_EPS) * g_ref[...]
    if final_norm:
        o_refs[0][...] = normed
    else:
        o_refs[0][...] = y
        o_refs[1][...] = normed.astype(o_refs[1].dtype)


def _out_proj(merged, w_out, layer, x2d, gain, final_norm):
    m, d = x2d.shape
    tm = min(ROW_TM, m)
    row_spec = pl.BlockSpec((tm, d), lambda i: (i, 0))
    out_specs, out_shape = [row_spec], [jax.ShapeDtypeStruct((m, d), F32)]
    if not final_norm:
        out_specs.append(row_spec)
        out_shape.append(jax.ShapeDtypeStruct((m, d), BF16))
    return pl.pallas_call(
        functools.partial(_out_kernel, final_norm=final_norm),
        grid=(m // tm,),
        in_specs=[row_spec, pl.BlockSpec((None, d, d), lambda i: (layer, 0, 0)), row_spec,
                  pl.BlockSpec((1, d), lambda i: (0, 0))],
        out_specs=out_specs,
        out_shape=out_shape,
        compiler_params=_params("parallel"),
        name="out_proj",
    )(merged, w_out, x2d, gain.reshape(1, d))


def _layer_vectors(l, pool_scale, rwkv_mu, rwkv_w0, rwkv_w_up, rwkv_a0, rwkv_a_up, rwkv_k_k, rwkv_k_a, rwkv_r_k,
                   rwkv_gn_w, rwkv_gn_b, attn_sink):
    w_lora = jnp.zeros((LANES, 2 * BRANCH_WIDTH), F32)
    w_lora = w_lora.at[:LORA_RANK, :BRANCH_WIDTH].set(rwkv_w_up[l]).at[LORA_RANK:, BRANCH_WIDTH:].set(rwkv_a_up[l])
    vecs = jnp.stack([rwkv_w0[l], rwkv_a0[l], rwkv_k_k[l], rwkv_k_a[l], rwkv_r_k[l], rwkv_gn_w[l], rwkv_gn_b[l],
                      jnp.zeros_like(rwkv_w0[l])])
    return dict(pool_scale=pool_scale[l].reshape(1, -1), mu=rwkv_mu[l].reshape(1, -1), vecs=vecs,
                w_lora=w_lora.astype(BF16), sink=attn_sink[l][np.array(ATT_HEAD_ORDER)])


def _sink_rows(sink_ordered, n_q):
    rows = jnp.repeat(sink_ordered.reshape(-1, 4), n_q, axis=1)
    return jnp.broadcast_to(rows[..., None], rows.shape + (LANES,))


def _mixer_layer(x2, h, b, l, wts, lv, next_gain, final_norm, *, pos0, pool_hist16, shift_prev, s0_bd, bias, caches):
    t = x2.shape[0] // b
    fuse_pool = pool_hist16 is None and t % min(POOL_FUSED_TM, t) == 0
    proj = {name: _matmul(h, wts["w_in"], l, col0, n, tn=tn, tm=tm, act=act, out_dtype=dt, name="proj_" + name)
            for name, (col0, n, tn, tm, act, dt) in PROJ_SEGMENTS.items() if not (fuse_pool and name == "pool")}
    p_shift, q, kv, gate = (proj[name].reshape(b, t, -1) for name in ("shift", "q", "kv", "gate"))
    mg = proj["merge"]

    if fuse_pool:
        u_pool, ga = _proj_pool(h, wts["w_in"], l, proj["gate"], wts["pool_w"], lv["pool_scale"], t)
        u_pool = u_pool.reshape(b, t, -1)
    else:
        u_pool = proj["pool"].reshape(b, t, -1)
        if pool_hist16 is None:
            pool_hist16 = jnp.zeros((b, 16, BRANCH_WIDTH), F32)
        ga = _pool_branch(u_pool, pool_hist16, gate, wts["pool_w"], l, lv["pool_scale"], pos0)

    t_pad = -(-t // CHUNK) * CHUNK
    if t_pad != t:
        p_in = jnp.pad(p_shift, ((0, 0), (0, t_pad - t), (0, 0)))
        gate_in = jnp.pad(gate, ((0, 0), (0, t_pad - t), (0, 0)))
    else:
        p_in, gate_in = p_shift, gate
    gb, s_bd = _rwkv_branch(p_in, shift_prev, s0_bd, gate_in, lv["mu"], lv["vecs"], lv["w_lora"],
                            n_valid=CHUNK - (t_pad - t))
    gb = gb[:, :t]

    if caches is None:
        gc = _attn_prompt(q, kv, bias, _sink_rows(lv["sink"], CHUNK), gate)
    else:
        gc = _attn_sample(q, kv, caches[0], caches[1], bias, _sink_rows(lv["sink"], t), gate)

    merged = _merge(ga.reshape(b * t, -1), gb.reshape(b * t, -1), gc.reshape(b * t, -1), wts["w_branch"], l, mg)
    outs = _out_proj(merged, wts["w_out"], l, x2, next_gain, final_norm)
    y, h_next = (outs[0], None) if final_norm else outs
    return (y, h_next, kv[..., :KV_WIDTH], kv[..., KV_WIDTH:], _state_from_blockdiag(s_bd), p_shift[:, -1],
            u_pool[:, t - POOL_HIST:])


def kernel(x_prompt, x_sample, cache_attn_k, cache_attn_v, state_rwkv, state_rwkv_shift, state_pool, norm_g, w_in,
           pool_w, pool_scale, rwkv_mu, rwkv_w0, rwkv_w_up, rwkv_a0, rwkv_a_up, rwkv_k_k, rwkv_k_a, rwkv_r_k,
           rwkv_gn_w, rwkv_gn_b, attn_sink, rel_pos_table, w_branch, w_out, final_norm_g):
    depth = norm_g.shape[0]
    bp, tp, _ = x_prompt.shape
    bs, ts, _ = x_sample.shape
    n_cache = cache_attn_k.shape[2]

    key_p = jnp.arange((BAND_CHUNKS + 1) * CHUNK) - BAND_CHUNKS * CHUNK
    bias_p = _rel_bias(rel_pos_table, key_p, CHUNK)
    key_s = jnp.concatenate([jnp.arange(n_cache) - n_cache, jnp.arange(ts)])
    bias_s = _rel_bias(rel_pos_table, key_s, ts)

    wts = dict(w_in=w_in.astype(BF16), pool_w=pool_w.astype(BF16), w_branch=w_branch.astype(BF16),
               w_out=w_out.astype(BF16))
    xp, xs = x_prompt.reshape(bp * tp, -1), x_sample.reshape(bs * ts, -1)
    hp = _rmsnorm(xp, norm_g[0], BF16, ROW_TM)
    hs = _rmsnorm(xs, norm_g[0], BF16, ROW_TM)
    outs_p, outs_s = [], []
    for l in range(depth):
        lv = _layer_vectors(l, pool_scale, rwkv_mu, rwkv_w0, rwkv_w_up, rwkv_a0, rwkv_a_up, rwkv_k_k, rwkv_k_a,
                            rwkv_r_k, rwkv_gn_w, rwkv_gn_b, attn_sink)
        last = l == depth - 1
        next_gain = final_norm_g if last else norm_g[l + 1]
        xp, hp, kp, vp, sp, shp, plp = _mixer_layer(
            xp, hp, bp, l, wts, lv, next_gain, last, pos0=0,
            pool_hist16=None,
            shift_prev=jnp.zeros((bp, 1, SHIFT_WIDTH), F32),
            s0_bd=jnp.zeros((bp, PAIRS, LANES, LANES), F32),
            bias=bias_p, caches=None)
        outs_p.append((kp[:, -WINDOW:].reshape(bp, WINDOW, ATT_KV_HEADS, HEAD_DIM),
                       vp[:, -WINDOW:].reshape(bp, WINDOW, ATT_KV_HEADS, HEAD_DIM), sp, shp, plp))
        xs, hs, k_s, v_s, s_s, sh_s, pl_s = _mixer_layer(
            xs, hs, bs, l, wts, lv, next_gain, last, pos0=PAST_LEN,
            pool_hist16=jnp.pad(state_pool[l], ((0, 0), (1, 0), (0, 0))),
            shift_prev=state_rwkv_shift[l][:, None, :],
            s0_bd=_state_to_blockdiag(state_rwkv[l]),
            bias=bias_s,
            caches=(cache_attn_k[l].reshape(bs, n_cache, KV_WIDTH), cache_attn_v[l].reshape(bs, n_cache, KV_WIDTH)))
        outs_s.append((k_s.reshape(bs, ts, ATT_KV_HEADS, HEAD_DIM), v_s.reshape(bs, ts, ATT_KV_HEADS, HEAD_DIM),
                       s_s, sh_s, pl_s))

    st = lambda outs, i: jnp.stack([o[i] for o in outs])
    return (xp.reshape(x_prompt.shape), xs.reshape(x_sample.shape), st(outs_p, 0), st(outs_p, 1), st(outs_s, 0), st(outs_s, 1),
            st(outs_p, 2), st(outs_s, 2), st(outs_p, 3), st(outs_s, 3), st(outs_p, 4), st(outs_s, 4))
```

```python
import functools
import math

import jax
import jax.numpy as jnp
import numpy as np
from jax import lax
from jax.experimental import pallas as pl
from jax.experimental.pallas import tpu as pltpu

F32 = jnp.float32
BF16 = jnp.bfloat16

D_MODEL = 2048
BRANCH_WIDTH = 1024
POOL_WINDOWS = (2, 4, 8, 16)
POOL_GROUP_DIM = BRANCH_WIDTH // len(POOL_WINDOWS)
POOL_HIST = max(POOL_WINDOWS) - 1
HEAD_DIM = 64
RWKV_HEADS = BRANCH_WIDTH // HEAD_DIM
LORA_RANK = 64
SHIFT_WIDTH = 3 * BRANCH_WIDTH + 2 * LORA_RANK
GN_EPS = 64e-5
ATT_HEADS = 16
ATT_KV_HEADS = 4
ATT_GROUP = ATT_HEADS // ATT_KV_HEADS
KV_WIDTH = ATT_KV_HEADS * HEAD_DIM
CHUNK = 64
WINDOW = 128
BAND_CHUNKS = WINDOW // CHUNK
N_BUCKETS = 32
MAX_DISTANCE = 128
NEG_INF = -1e30
RMS_EPS = 1e-6
PAST_LEN = 2048
N_BRANCH = 3
OFF_POOL = 0
OFF_SHIFT = OFF_POOL + BRANCH_WIDTH
OFF_Q = OFF_SHIFT + SHIFT_WIDTH
OFF_K = OFF_Q + BRANCH_WIDTH
OFF_V = OFF_K + KV_WIDTH
OFF_GATE = OFF_V + KV_WIDTH
OFF_MERGE = OFF_GATE + N_BRANCH * BRANCH_WIDTH

SUBLANES = 8
LANES = 128
PAIRS = BRANCH_WIDTH // LANES
ATTN_CHUNKS_PER_STEP = 8
RWKV_CHUNKS_PER_STEP = 8
RWKV_GROUP = 8
KEY_PAD = 256
ATT_HEAD_ORDER = tuple(8 * m + o for m in range(ATT_KV_HEADS // 2) for o in (0, 2, 5, 7, 1, 3, 4, 6))
POOL_FUSED_TM = 1024
ROW_TM = 512
MERGE_TM, MERGE_TN = 1024, 1024
PROJ_SEGMENTS = {
    "pool": (OFF_POOL, BRANCH_WIDTH, 1024, 1024, None, F32),
    "shift": (OFF_SHIFT, SHIFT_WIDTH, SHIFT_WIDTH, 512, None, F32),
    "q": (OFF_Q, BRANCH_WIDTH, 1024, 1024, "att_scale", BF16),
    "kv": (OFF_K, 2 * KV_WIDTH, 2 * KV_WIDTH, 1024, None, F32),
    "gate": (OFF_GATE, N_BRANCH * BRANCH_WIDTH, 3072, 1024, "silu", BF16),
    "merge": (OFF_MERGE, N_BRANCH * D_MODEL, 3072, 1024, "sigmoid", BF16),
}
VMEM_LIMIT = 56 * 1024 * 1024


def _params(*sem):
    return pltpu.CompilerParams(dimension_semantics=sem, vmem_limit_bytes=VMEM_LIMIT)


def _sigmoid(x):
    return 1.0 / (1.0 + jnp.exp(-x))


def _dot(a, b):
    return jnp.dot(a, b, preferred_element_type=F32)


def _dot_nt(a, b):
    return lax.dot_general(a, b, (((1,), (1,)), ((), ())), preferred_element_type=F32)


def _split3(x):
    hi = x.astype(BF16)
    r1 = x - hi.astype(F32)
    mid = r1.astype(BF16)
    lo = (r1 - mid.astype(F32)).astype(BF16)
    return hi, mid, lo


def _dot3_right_exact(x, b):
    hi, mid, lo = _split3(x)
    return _dot(hi, b) + _dot(mid, b) + _dot(lo, b)


def _dot3_left_exact(a, x):
    hi, mid, lo = _split3(x)
    return _dot(a, hi) + _dot(a, mid) + _dot(a, lo)


def _rmsnorm_kernel(x_ref, g_ref, o_ref):
    x = x_ref[...]
    ms = jnp.mean(x * x, axis=-1, keepdims=True)
    o_ref[...] = (x * lax.rsqrt(ms + RMS_EPS) * g_ref[...]).astype(o_ref.dtype)


def _rmsnorm(x2d, g, out_dtype, tm):
    m, d = x2d.shape
    tm = min(tm, m)
    return pl.pallas_call(
        _rmsnorm_kernel,
        grid=(m // tm,),
        in_specs=[pl.BlockSpec((tm, d), lambda i: (i, 0)), pl.BlockSpec((1, d), lambda i: (0, 0))],
        out_specs=pl.BlockSpec((tm, d), lambda i: (i, 0)),
        out_shape=jax.ShapeDtypeStruct((m, d), out_dtype),
        compiler_params=_params("parallel"),
        name="rmsnorm",
    )(x2d, g.reshape(1, d))


def _mm_kernel(a_ref, w_ref, o_ref, *, act):
    acc = _dot(a_ref[...], w_ref[0])
    if act == "silu":
        acc = acc * _sigmoid(acc)
    elif act == "sigmoid":
        acc = _sigmoid(acc)
    elif act == "att_scale":
        acc = acc * (HEAD_DIM ** -0.5)
    o_ref[...] = acc.astype(o_ref.dtype)


def _matmul(a, w, layer, col0, n, *, tn, tm, act, out_dtype, name):
    m, k = a.shape
    tm = min(tm, m)
    return pl.pallas_call(
        functools.partial(_mm_kernel, act=act),
        grid=(n // tn, m // tm),
        in_specs=[pl.BlockSpec((tm, k), lambda j, i: (i, 0)),
                  pl.BlockSpec((pl.Element(1), pl.Element(k), pl.Element(tn)),
                               lambda j, i: (layer, 0, pl.multiple_of(col0 + j * tn, LANES)))],
        out_specs=pl.BlockSpec((tm, tn), lambda j, i: (i, j)),
        out_shape=jax.ShapeDtypeStruct((m, n), out_dtype),
        compiler_params=_params("arbitrary", "arbitrary"),
        name=name,
    )(a, w)


def _pool_mix(u, prev, pos, zg_ref, w_ref, scale_ref, o_ref):
    for g, w in enumerate(POOL_WINDOWS):
        sl = slice(g * POOL_GROUP_DIM, (g + 1) * POOL_GROUP_DIM)
        e = jnp.concatenate([prev[:, sl], u[:, sl]], axis=0)
        step = 1
        while step < w:
            e = e + pltpu.roll(e, step, 0)
            step *= 2
        s = e[16:, :]
        cnt = jnp.minimum(w, pos + 1).astype(F32)
        pooled = s / cnt - u[:, sl]
        o = _dot(pooled.astype(BF16), w_ref[g]) * scale_ref[:, sl]
        o_ref[:, sl] = (o * zg_ref[:, sl].astype(F32)).astype(o_ref.dtype)


def _pool_kernel(u_ref, prev_ref, hist_ref, zg_ref, w_ref, scale_ref, o_ref, *, tt, pos0):
    i = pl.program_id(1)
    prev = jnp.where(i == 0, hist_ref[0], prev_ref[0])
    pos = pos0 + i * tt + lax.broadcasted_iota(jnp.int32, (tt, POOL_GROUP_DIM), 0)
    _pool_mix(u_ref[0], prev, pos, zg_ref.at[0], w_ref, scale_ref, o_ref.at[0])


def _proj_pool_kernel(a_ref, w_ref, zg_ref, pw_ref, scale_ref, u_ref, o_ref, carry_ref, *, tm, tiles_per_seq):
    step = lax.rem(pl.program_id(0), tiles_per_seq)
    u = _dot(a_ref[...], w_ref[0])
    u_ref[...] = u
    prev = jnp.where(step == 0, 0.0, carry_ref[...])
    carry_ref[...] = u[tm - 16:, :]
    pos = step * tm + lax.broadcasted_iota(jnp.int32, (tm, POOL_GROUP_DIM), 0)
    _pool_mix(u, prev, pos, zg_ref, pw_ref, scale_ref, o_ref)


def _proj_pool(h, w_in, layer, gate2d, pool_w, pool_scale, seq_len):
    m, k = h.shape
    c = BRANCH_WIDTH
    tm = min(POOL_FUSED_TM, seq_len)
    return pl.pallas_call(
        functools.partial(_proj_pool_kernel, tm=tm, tiles_per_seq=seq_len // tm),
        grid=(m // tm,),
        in_specs=[
            pl.BlockSpec((tm, k), lambda i: (i, 0)),
            pl.BlockSpec((pl.Element(1), pl.Element(k), pl.Element(c)), lambda i: (layer, 0, OFF_POOL)),
            pl.BlockSpec((tm, c), lambda i: (i, 0)),
            pl.BlockSpec((None,) + pool_w.shape[1:], lambda i: (layer, 0, 0, 0)),
            pl.BlockSpec((1, c), lambda i: (0, 0)),
        ],
        out_specs=[pl.BlockSpec((tm, c), lambda i: (i, 0)), pl.BlockSpec((tm, c), lambda i: (i, 0))],
        out_shape=[jax.ShapeDtypeStruct((m, c), F32), jax.ShapeDtypeStruct((m, c), BF16)],
        scratch_shapes=[pltpu.VMEM((16, c), F32)],
        compiler_params=_params("arbitrary"),
        name="proj_pool_fused",
    )(h, w_in, gate2d, pool_w, pool_scale)


def _pool_branch(u, hist16, gate, pool_w, layer, pool_scale, pos0):
    b, t, c = u.shape
    tt = min(t, 512)
    sub = tt // 16
    return pl.pallas_call(
        functools.partial(_pool_kernel, tt=tt, pos0=pos0),
        grid=(b, t // tt),
        in_specs=[
            pl.BlockSpec((1, tt, c), lambda bi, i: (bi, i, 0)),
            pl.BlockSpec((1, 16, c), lambda bi, i: (bi, jnp.maximum(i * sub - 1, 0), 0)),
            pl.BlockSpec((1, 16, c), lambda bi, i: (bi, 0, 0)),
            pl.BlockSpec((1, tt, c), lambda bi, i: (bi, i, 0)),
            pl.BlockSpec((None,) + pool_w.shape[1:], lambda bi, i: (layer, 0, 0, 0)),
            pl.BlockSpec((1, c), lambda bi, i: (0, 0)),
        ],
        out_specs=pl.BlockSpec((1, tt, c), lambda bi, i: (bi, i, 0)),
        out_shape=jax.ShapeDtypeStruct((b, t, c), BF16),
        compiler_params=_params("parallel", "parallel"),
        name="pool_branch",
    )(u, u, hist16, gate, pool_w, pool_scale)


def _rwkv_kernel(p_ref, sprev_ref, s0_ref, zg_ref, mu_ref, vec_ref, wl_ref, o_ref, s_ref, carry_ref,
                 *, n_valid, n_sub):
    @pl.when(pl.program_id(1) == 0)
    def _():
        s_ref[...] = s0_ref[...]
        carry_ref[...] = sprev_ref[0]

    for sub in range(n_sub):
        rows = pl.ds(sub * CHUNK, CHUNK)
        _rwkv_chunk(p_ref.at[:, rows, :], zg_ref.at[:, rows, :], mu_ref, vec_ref, wl_ref, o_ref.at[:, rows, :],
                    s_ref, carry_ref, n_valid=n_valid)


def _rwkv_chunk(p_ref, zg_ref, mu_ref, vec_ref, wl_ref, o_ref, s_ref, carry_ref, *, n_valid):
    C = CHUNK
    p = p_ref[0]
    first_row = lax.broadcasted_iota(jnp.int32, (SUBLANES, SHIFT_WIDTH), 0) == 0
    row = lax.broadcasted_iota(jnp.int32, (C, BRANCH_WIDTH), 0)
    shifted = pltpu.roll(p, 1, 0)
    prev = jnp.concatenate([jnp.where(first_row, carry_ref[...], shifted[:SUBLANES]), shifted[SUBLANES:]], axis=0)
    carry_ref[...] = p[C - 1:C, :]
    xs = p + (prev - p) * mu_ref[...]

    w0, a0, k_k, k_a, r_k, gn_w, gn_b = (vec_ref[i:i + 1, :] for i in range(7))
    lane = lax.broadcasted_iota(jnp.int32, (C, LANES), 1)
    head0 = lane < HEAD_DIM
    lin = xs[:, 3 * BRANCH_WIDTH:]
    lin = jnp.where(head0, jnp.tanh(lin), lin)
    lora = _dot(lin.astype(BF16), wl_ref[...])
    log_decay = -math.exp(-0.5) * _sigmoid(w0 + lora[:, :BRANCH_WIDTH])
    iclr = _sigmoid(a0 + lora[:, BRANCH_WIDTH:])
    r = xs[:, :BRANCH_WIDTH]
    k = xs[:, BRANCH_WIDTH:2 * BRANCH_WIDTH]
    v = xs[:, 2 * BRANCH_WIDTH:3 * BRANCH_WIDTH]
    kk_raw = k * k_k
    k2 = k * (1.0 + (iclr - 1.0) * k_a)
    if n_valid < C:
        ok = row < n_valid
        log_decay = jnp.where(ok, log_decay, 0.0)
        kk_raw = jnp.where(ok, kk_raw, 0.0)
        k2 = jnp.where(ok, k2, 0.0)
        r = jnp.where(ok, r, 0.0)
        v = jnp.where(ok, v, 0.0)

    ti = lax.broadcasted_iota(jnp.int32, (C, C), 0)
    si = lax.broadcasted_iota(jnp.int32, (C, C), 1)
    tri = (si <= ti).astype(BF16)
    cum = _dot3_left_exact(tri, log_decay)
    cum_x = cum - log_decay
    cum_c = cum[C - 1:C, :]

    ri =lax.broadcasted_iota(jnp.int32, (2 * C, 2 * C), 0)
    ci = lax.broadcasted_iota(jnp.int32, (2 * C, 2 * C), 1)

    def stack(x):
        return jnp.concatenate([jnp.where(head0, x, 0.0), jnp.where(head0, 0.0, x)], axis=0)

    def each(f, *cols):
        return [f(*xs) for xs in zip(*cols)]

    def bf(xs):
        return [x.astype(BF16) for x in xs]

    def head_sums(xs):
        lo = [jnp.sum(jnp.where(head0, x, 0.0), axis=-1, keepdims=True) for x in xs]
        hi = [jnp.sum(jnp.where(head0, 0.0, x), axis=-1, keepdims=True) for x in xs]
        return [jnp.where(head0, a, b) for a, b in zip(lo, hi)]

    def stack_b(x):
        xb = x.astype(BF16)
        zero = jnp.zeros((), BF16)
        return jnp.concatenate([jnp.where(head0, xb, zero), jnp.where(head0, zero, xb)], axis=0)

    fold_strict = (ci & (C - 1)) < (ri & (C - 1))
    fold_incl = (ci & (C - 1)) <= (ri & (C - 1))
    zeros_blk = jnp.zeros((C, LANES), BF16)

    low_lanes = lax.broadcasted_iota(jnp.int32, (C, 2 * C), 1) < C

    def unfold_p(x):
        moved = pltpu.roll(x[C:], C, 1)
        return jnp.concatenate([jnp.where(low_lanes, x[:C], 0.0), jnp.where(low_lanes, 0.0, moved)], axis=0)

    def halves(x):
        return x[:C], x[C:]

    for g0 in range(0, PAIRS, RWKV_GROUP):
        prs = list(range(g0, g0 + RWKV_GROUP))
        sls = [slice(pr * LANES, (pr + 1) * LANES) for pr in prs]
        r_p = [r[:, sl] for sl in sls]
        k_p = [k2[:, sl] for sl in sls]
        v_p = [v[:, sl] for sl in sls]
        kkr = [kk_raw[:, sl] for sl in sls]
        ic = [iclr[:, sl] for sl in sls]
        cm = [cum[:, sl] for sl in sls]
        cx = [cum_x[:, sl] for sl in sls]
        cc = [cum_c[:, sl] for sl in sls]
        n2 = head_sums(each(lambda x: x * x, kkr))
        kk = each(lambda x, n: x * lax.rsqrt(jnp.maximum(n, 1e-24)), kkr, n2)
        b_p = each(lambda x, i: x * i, kk, ic)
        e_neg = each(lambda c_: jnp.exp(-c_), cm)
        e_end = each(lambda c_, e_: jnp.exp(e_ - c_), cm, cc)
        w_c = each(jnp.exp, cc)
        rt = each(lambda x, c_: x * jnp.exp(c_), r_p, cm)
        at_b = each(lambda x, c_: stack_b(-x * jnp.exp(c_)), kk, cx)
        rt_b = each(stack_b, rt)
        v_s = each(stack, v_p)
        v_b = bf(v_s)
        bb_b = each(lambda x, e_: stack_b(x * e_), b_p, e_end)
        kb_b = each(lambda x, e_: stack_b(x * e_), k_p, e_end)
        bk_b = bf(each(lambda b_, k_, e_: jnp.concatenate([b_ * e_, k_ * e_], axis=0), b_p, k_p, e_neg))
        ar_b = each(lambda a_, r_: jnp.concatenate([a_, r_], axis=0), at_b, rt_b)
        prod = each(_dot_nt, ar_b, bk_b)
        fa = each(lambda x: jnp.where(fold_strict, x[:2 * C], 0.0), prod)
        fa_b = bf(fa)
        fr_b = bf(each(lambda x: jnp.where(fold_incl, x[2 * C:], 0.0), prod))
        a_ab = each(unfold_p, fa)
        tinv = each(lambda x: jnp.where(ri == ci, 1.0, x), a_ab)
        pw_b = bf(a_ab)
        pw_b = bf(each(_dot, pw_b, pw_b))
        n_sq = int(math.log2(C)) - 1
        for step in range(n_sq):
            if step < n_sq - 1:
                both = each(lambda t_, p_: _dot(p_, jnp.concatenate([t_.astype(BF16), p_], axis=1)), tinv, pw_b)
                tinv = each(lambda t_, x: t_ + x[:, :2 * C], tinv, both)
                pw_b = bf([x[:, 2 * C:] for x in both])
            else:
                tinv = each(lambda t_, p_: t_ + _dot(p_, t_.astype(BF16)), tinv, pw_b)
        t_b = bf(tinv)
        def akv_of(f_, v_):
            return jnp.concatenate([_dot(fh, jnp.concatenate([zeros_blk, vh], axis=0))
                                    for fh, vh in zip(halves(f_), halves(v_))], axis=0)
        akv = each(akv_of, fa_b, v_b)
        au = each(lambda t_, a_, k_: _dot(t_, jnp.concatenate([a_, k_.astype(BF16)], axis=1)), t_b, at_b, akv)
        au_b = bf(au)

        def r_side(f_, au_, v_):
            parts = [_dot(fh, jnp.concatenate([ah, jnp.concatenate([zeros_blk, vh], axis=1)], axis=0))
                     for fh, ah, vh in zip(halves(f_), halves(au_), halves(v_))]
            return parts[0] + parts[1]
        ry = each(r_side, fr_b, au_b, v_b)
        rh = each(lambda x, y_: x + y_[:, :LANES], rt, ry)
        y0 = [y_[:, LANES:] for y_ in ry]
        mg = each(lambda x, b_: _dot(x.T.astype(BF16), b_), au, bb_b)
        m_lr = [x[:LANES] for x in mg]
        g_bd = each(lambda x, v_, k_: x[LANES:] + _dot(v_.T.astype(BF16), k_), mg, v_s, kb_b)
        s_bd = [s_ref[0, pr] for pr in prs]
        s_b = bf(s_bd)
        y = each(lambda x, s_, y_: _dot_nt(x.astype(BF16), s_) + y_, rh, s_b, y0)
        s_new = each(lambda s_, w_, sb_, m_, g_: s_ * w_ + _dot(sb_, m_.astype(BF16)) + g_,
                     s_bd, w_c, s_b, m_lr, g_bd)
        for pr, s_ in zip(prs, s_new):
            s_ref[0, pr] = s_
        yc = each(lambda x, m_: x - m_ * (1.0 / HEAD_DIM), y, head_sums(y))
        var = head_sums(each(lambda x: x * x, yc))
        rk = head_sums(each(lambda r_, k_, sl: r_ * k_ * r_k[:, sl], r_p, k_p, sls))
        for sl, yc_, var_, rk_, v_ in zip(sls, yc, var, rk, v_p):
            yn = yc_ * lax.rsqrt(var_ * (1.0 / HEAD_DIM) + GN_EPS) * gn_w[:, sl] + gn_b[:, sl]
            o_ref[0, :, sl] = ((yn + rk_ * v_) * zg_ref[0, :, sl].astype(F32)).astype(o_ref.dtype)


def _rwkv_branch(p_shift, shift_prev, s0_bd, gate, mu, vecs, w_lora, n_valid):
    b, t, _ = p_shift.shape
    n_sub = math.gcd(t // CHUNK, RWKV_CHUNKS_PER_STEP)
    rows = n_sub * CHUNK
    return pl.pallas_call(
        functools.partial(_rwkv_kernel, n_valid=n_valid, n_sub=n_sub),
        grid=(b, t // rows),
        in_specs=[
            pl.BlockSpec((1, rows, SHIFT_WIDTH), lambda bi, c: (bi, c, 0)),
            pl.BlockSpec((1, 1, SHIFT_WIDTH), lambda bi, c: (bi, 0, 0)),
            pl.BlockSpec((1, PAIRS, LANES, LANES), lambda bi, c: (bi, 0, 0, 0)),
            pl.BlockSpec((1, rows, BRANCH_WIDTH), lambda bi, c: (bi, c, 1)),
            pl.BlockSpec((1, SHIFT_WIDTH), lambda bi, c: (0, 0)),
            pl.BlockSpec((8, BRANCH_WIDTH), lambda bi, c: (0, 0)),
            pl.BlockSpec((LANES, 2 * BRANCH_WIDTH), lambda bi, c: (0, 0)),
        ],
        out_specs=[
            pl.BlockSpec((1, rows, BRANCH_WIDTH), lambda bi, c: (bi, c, 0)),
            pl.BlockSpec((1, PAIRS, LANES, LANES), lambda bi, c: (bi, 0, 0, 0)),
        ],
        out_shape=[
            jax.ShapeDtypeStruct((b, t, BRANCH_WIDTH), BF16),
            jax.ShapeDtypeStruct((b, PAIRS, LANES, LANES), F32),
        ],
        scratch_shapes=[pltpu.VMEM((1, SHIFT_WIDTH), F32)],
        compiler_params=_params("parallel", "arbitrary"),
        name="rwkv_branch",
    )(p_shift, shift_prev, s0_bd, gate, mu, vecs, w_lora)


def _state_to_blockdiag(s):
    b = s.shape[0]
    sp = s.reshape(b, PAIRS, 2, HEAD_DIM, HEAD_DIM)
    z = jnp.zeros_like(sp[:, :, 0])
    top = jnp.concatenate([sp[:, :, 0], z], axis=-1)
    bot = jnp.concatenate([z, sp[:, :, 1]], axis=-1)
    return jnp.concatenate([top, bot], axis=-2)


def _state_from_blockdiag(s_bd):
    b = s_bd.shape[0]
    h0 = s_bd[:, :, :HEAD_DIM, :HEAD_DIM]
    h1 = s_bd[:, :, HEAD_DIM:, HEAD_DIM:]
    return jnp.stack([h0, h1], axis=2).reshape(b, RWKV_HEADS, HEAD_DIM, HEAD_DIM)


def _attn_kernel(*refs, n_kv_parts, tq, n_keys, masked, n_sub):
    q_ref = refs[0]
    k_refs = refs[1:1 + n_kv_parts]
    v_refs = refs[1 + n_kv_parts:1 + 2 * n_kv_parts]
    bias_ref, sink_ref, zg_ref, o_ref = refs[1 + 2 * n_kv_parts:]
    k_rows = jnp.concatenate([kr[0] for kr in k_refs], axis=0)
    v_rows = jnp.concatenate([vr[0] for vr in v_refs], axis=0)
    for sub in range(n_sub):
        rows = pl.ds(sub * tq, tq)
        _attn_chunk(q_ref.at[:, rows, :], k_rows[sub * tq:sub * tq + n_keys], v_rows[sub * tq:sub * tq + n_keys],
                    bias_ref, sink_ref, zg_ref.at[:, rows, :], o_ref.at[:, rows, :],
                    n=pl.program_id(1) * n_sub + sub, tq=tq, n_keys=n_keys, masked=masked)


def _attn_chunk(q_ref, k_seen, v_seen, bias_ref, sink_ref, zg_ref, o_ref, *, n, tq, n_keys, masked):
    pad = jnp.zeros((KEY_PAD - n_keys, KV_WIDTH), F32)
    k_all = jnp.concatenate([k_seen, pad], axis=0)
    v_all = jnp.concatenate([v_seen, pad], axis=0)
    lane =lax.broadcasted_iota(jnp.int32, (tq, LANES), 1)
    head0 = lane < HEAD_DIM
    col = lax.broadcasted_iota(jnp.int32, (1, KEY_PAD), 1)
    valid = col < n_keys
    if masked:
        valid = valid & (col // CHUNK + n - BAND_CHUNKS >= 0)
    q = q_ref[0]
    zero = jnp.zeros((), q.dtype)
    lo_half = lambda s: jnp.where(head0, s, zero)
    hi_half = lambda s: jnp.where(head0, zero, s)
    lhs, keys, vals = [], [], []
    for m in range(ATT_KV_HEADS // 2):
        slabs = [q[:, (4 * m + i) * LANES:(4 * m + i + 1) * LANES] for i in range(4)]
        k_sl = k_all[:, m * LANES:(m + 1) * LANES]
        v_sl = v_all[:, m * LANES:(m + 1) * LANES]
        lhs.append(jnp.concatenate([lo_half(slabs[0]), lo_half(slabs[1]), hi_half(slabs[2]), hi_half(slabs[3])], 0))
        lhs.append(jnp.concatenate([hi_half(slabs[0]), hi_half(slabs[1]), lo_half(slabs[2]), lo_half(slabs[3])], 0))
        keys += [k_sl.astype(BF16), pltpu.roll(k_sl, HEAD_DIM, 1).astype(BF16)]
        vals += [v_sl.astype(BF16), pltpu.roll(v_sl, HEAD_DIM, 1).astype(BF16)]
    n_grp = len(lhs)
    ones = jnp.ones((KEY_PAD, LANES), BF16)
    vals = [jnp.concatenate([x, ones], axis=1) for x in vals]
    logits = [jnp.where(valid, _dot_nt(lhs[i], keys[i]) + bias_ref[i], NEG_INF) for i in range(n_grp)]
    sink = [sink_ref[i] for i in range(n_grp)]
    mx = [jnp.maximum(jnp.max(logits[i], axis=-1, keepdims=True), sink[i]) for i in range(n_grp)]
    e = [jnp.exp(jnp.concatenate([logits[i][:, :LANES] - mx[i], logits[i][:, LANES:] - mx[i]], axis=1))
         for i in range(n_grp)]
    pv = [_dot(e[i].astype(BF16), vals[i]) for i in range(n_grp)]
    out = [pv[i][:, :LANES] / (pv[i][:, LANES:] + jnp.exp(sink[i] - mx[i])) for i in range(n_grp)]
    for m in range(ATT_KV_HEADS // 2):
        same, swap = out[2 * m], out[2 * m + 1]
        for i in range(4):
            a, b = same[i * tq:(i + 1) * tq], swap[i * tq:(i + 1) * tq]
            res = jnp.where(head0, a, b) if i < 2 else jnp.where(head0, b, a)
            sl = slice((4 * m + i) * LANES, (4 * m + i + 1) * LANES)
            o_ref[0, :, sl] = (res * zg_ref[0, :, sl].astype(F32)).astype(o_ref.dtype)


def _attn_prompt(q, kv, bias, sink_rows, gate):
    b, t, _ = q.shape
    nc = t // CHUNK
    n_sub = math.gcd(nc, ATTN_CHUNKS_PER_STEP)
    rows = n_sub * CHUNK
    k_specs, v_specs = ([pl.BlockSpec((1, CHUNK, KV_WIDTH),
                                      lambda bi, n, d=d, c=c: (bi, jnp.maximum(n * n_sub - d, 0), c))
                         for d in range(BAND_CHUNKS, 0, -1)]
                        + [pl.BlockSpec((1, rows, KV_WIDTH), lambda bi, n, c=c: (bi, n, c))] for c in range(2))
    return pl.pallas_call(
        functools.partial(_attn_kernel, n_kv_parts=BAND_CHUNKS + 1, tq=CHUNK,
                          n_keys=(BAND_CHUNKS + 1) * CHUNK, masked=True, n_sub=n_sub),
        grid=(b, nc // n_sub),
        in_specs=[pl.BlockSpec((1, rows, BRANCH_WIDTH), lambda bi, n: (bi, n, 0))]
        + k_specs + v_specs
        + [pl.BlockSpec(bias.shape, lambda bi, n: (0, 0, 0)),
           pl.BlockSpec(sink_rows.shape, lambda bi, n: (0, 0, 0)),
           pl.BlockSpec((1, rows, BRANCH_WIDTH), lambda bi, n: (bi, n, 2))],
        out_specs=pl.BlockSpec((1, rows, BRANCH_WIDTH), lambda bi, n: (bi, n, 0)),
        out_shape=jax.ShapeDtypeStruct((b, t, BRANCH_WIDTH), BF16),
        compiler_params=_params("parallel", "parallel"),
        name="attn_prompt",
    )(q, kv, kv, kv, kv, kv, kv, bias, sink_rows, gate)


def _attn_sample(q, kv, k_cache, v_cache, bias, sink_rows, gate):
    b, t, _ = q.shape
    n_cache = k_cache.shape[1]
    cache_spec = pl.BlockSpec((1, n_cache, KV_WIDTH), lambda bi, n: (bi, 0, 0))
    return pl.pallas_call(
        functools.partial(_attn_kernel, n_kv_parts=2, tq=t, n_keys=n_cache + t, masked=False, n_sub=1),
        grid=(b, 1),
        in_specs=[pl.BlockSpec((1, t, BRANCH_WIDTH), lambda bi, n: (bi, 0, 0)),
                  cache_spec, pl.BlockSpec((1, t, KV_WIDTH), lambda bi, n: (bi, 0, 0)),
                  cache_spec, pl.BlockSpec((1, t, KV_WIDTH), lambda bi, n: (bi, 0, 1)),
                  pl.BlockSpec(bias.shape, lambda bi, n: (0, 0, 0)),
                  pl.BlockSpec(sink_rows.shape, lambda bi, n: (0, 0, 0)),
                  pl.BlockSpec((1, t, BRANCH_WIDTH), lambda bi, n: (bi, 0, 2))],
        out_specs=pl.BlockSpec((1, t, BRANCH_WIDTH), lambda bi, n: (bi, 0, 0)),
        out_shape=jax.ShapeDtypeStruct((b, t, BRANCH_WIDTH), BF16),
        compiler_params=_params("parallel", "parallel"),
        name="attn_sample",
    )(q, k_cache, kv, v_cache, kv, bias, sink_rows, gate)


def _t5_bucket(rel):
    half = N_BUCKETS // 2
    n = -rel
    ret = jnp.where(n < 0, half, 0)
    n = jnp.abs(n)
    max_exact = half // 2
    large = max_exact + (jnp.log(jnp.maximum(n, 1).astype(jnp.float32) / max_exact)
                         / math.log(MAX_DISTANCE / max_exact) * (half - max_exact)).astype(jnp.int32)
    large = jnp.minimum(large, half - 1)
    return ret + jnp.where(n < max_exact, n, large)


def _bias_kernel(tab_ref, onehot_ref, o_ref):
    o_ref[...] = _dot3_right_exact(tab_ref[...], onehot_ref[...])


def _rel_bias(table, key_pos, n_q):
    n_keys = key_pos.shape[0]
    bucket = _t5_bucket(key_pos[None, :] - jnp.arange(n_q)[:, None])
    bucket = jnp.pad(bucket, ((0, 0), (0, KEY_PAD - n_keys)), constant_values=-1)
    onehot = (bucket.reshape(1, -1) == jnp.arange(N_BUCKETS)[:, None]).astype(BF16)
    out = pl.pallas_call(
        _bias_kernel,
        out_shape=jax.ShapeDtypeStruct((ATT_HEADS, n_q * KEY_PAD), F32),
        name="rel_bias",
    )(table.T[np.array(ATT_HEAD_ORDER)], onehot)
    return out.reshape(len(ATT_HEAD_ORDER) // 4, 4 * n_q, KEY_PAD)


def _merge_kernel(ga_ref, gb_ref, gc_ref, w_ref, m0_ref, m1_ref, m2_ref, o_ref):
    acc = m0_ref[...].astype(F32) * _dot(ga_ref[...], w_ref[0])
    acc = acc + m1_ref[...].astype(F32) * _dot(gb_ref[...], w_ref[1])
    acc = acc + m2_ref[...].astype(F32) * _dot(gc_ref[...], w_ref[2])
    o_ref[...] = acc.astype(o_ref.dtype)


def _merge(ga, gb, gc, w_branch, layer, mg):
    m, kdim = ga.shape
    tm = min(MERGE_TM, m)
    tn = MERGE_TN
    nj = D_MODEL // tn
    g_spec = pl.BlockSpec((tm, kdim), lambda j, i: (i, 0))
    m_specs = [pl.BlockSpec((tm, tn), lambda j, i, br=br: (i, br * nj + j)) for br in range(N_BRANCH)]
    return pl.pallas_call(
        _merge_kernel,
        grid=(nj, m // tm),
        in_specs=[g_spec, g_spec, g_spec,
                  pl.BlockSpec((None, N_BRANCH, kdim, tn), lambda j, i: (layer, 0, 0, j))] + m_specs,
        out_specs=pl.BlockSpec((tm, tn), lambda j, i: (i, j)),
        out_shape=jax.ShapeDtypeStruct((m, D_MODEL), BF16),
        compiler_params=_params("arbitrary", "arbitrary"),
        name="merge",
    )(ga, gb, gc, w_branch, mg, mg, mg)


def _out_kernel(a_ref, w_ref, x_ref, g_ref, *o_refs, final_norm):
    y = x_ref[...] + _dot(a_ref[...], w_ref[...])
    ms = jnp.mean(y * y, axis=-1, keepdims=True)
    normed = y * lax.rsqrt(ms + RMS_EPS) * g_ref[...]
    if final_norm:
        o_refs[0][...] = normed
    else:
        o_refs[0][...] = y
        o_refs[1][...] = normed.astype(o_refs[1].dtype)


def _out_proj(merged, w_out, layer, x2d, gain, final_norm):
    m, d = x2d.shape
    tm = min(ROW_TM, m)
    row_spec = pl.BlockSpec((tm, d), lambda i: (i, 0))
    out_specs, out_shape = [row_spec], [jax.ShapeDtypeStruct((m, d), F32)]
    if not final_norm:
        out_specs.append(row_spec)
        out_shape.append(jax.ShapeDtypeStruct((m, d), BF16))
    return pl.pallas_call(
        functools.partial(_out_kernel, final_norm=final_norm),
        grid=(m // tm,),
        in_specs=[row_spec, pl.BlockSpec((None, d, d), lambda i: (layer, 0, 0)), row_spec,
                  pl.BlockSpec((1, d), lambda i: (0, 0))],
        out_specs=out_specs,
        out_shape=out_shape,
        compiler_params=_params("parallel"),
        name="out_proj",
    )(merged, w_out, x2d, gain.reshape(1, d))


def _layer_vectors(l, pool_scale, rwkv_mu, rwkv_w0, rwkv_w_up, rwkv_a0, rwkv_a_up, rwkv_k_k, rwkv_k_a, rwkv_r_k,
                   rwkv_gn_w, rwkv_gn_b, attn_sink):
    w_lora = jnp.zeros((LANES, 2 * BRANCH_WIDTH), F32)
    w_lora = w_lora.at[:LORA_RANK, :BRANCH_WIDTH].set(rwkv_w_up[l]).at[LORA_RANK:, BRANCH_WIDTH:].set(rwkv_a_up[l])
    vecs = jnp.stack([rwkv_w0[l], rwkv_a0[l], rwkv_k_k[l], rwkv_k_a[l], rwkv_r_k[l], rwkv_gn_w[l], rwkv_gn_b[l],
                      jnp.zeros_like(rwkv_w0[l])])
    return dict(pool_scale=pool_scale[l].reshape(1, -1), mu=rwkv_mu[l].reshape(1, -1), vecs=vecs,
                w_lora=w_lora.astype(BF16), sink=attn_sink[l][np.array(ATT_HEAD_ORDER)])


def _sink_rows(sink_ordered, n_q):
    rows = jnp.repeat(sink_ordered.reshape(-1, 4), n_q, axis=1)
    return jnp.broadcast_to(rows[..., None], rows.shape + (LANES,))


def _mixer_layer(x2, h, b, l, wts, lv, next_gain, final_norm, *, pos0, pool_hist16, shift_prev, s0_bd, bias, caches):
    t = x2.shape[0] // b
    fuse_pool = pool_hist16 is None and t % min(POOL_FUSED_TM, t) == 0
    proj = {name: _matmul(h, wts["w_in"], l, col0, n, tn=tn, tm=tm, act=act, out_dtype=dt, name="proj_" + name)
            for name, (col0, n, tn, tm, act, dt) in PROJ_SEGMENTS.items() if not (fuse_pool and name == "pool")}
    p_shift, q, kv, gate = (proj[name].reshape(b, t, -1) for name in ("shift", "q", "kv", "gate"))
    mg = proj["merge"]

    if fuse_pool:
        u_pool, ga = _proj_pool(h, wts["w_in"], l, proj["gate"], wts["pool_w"], lv["pool_scale"], t)
        u_pool = u_pool.reshape(b, t, -1)
    else:
        u_pool = proj["pool"].reshape(b, t, -1)
        if pool_hist16 is None:
            pool_hist16 = jnp.zeros((b, 16, BRANCH_WIDTH), F32)
        ga = _pool_branch(u_pool, pool_hist16, gate, wts["pool_w"], l, lv["pool_scale"], pos0)

    t_pad = -(-t // CHUNK) * CHUNK
    if t_pad != t:
        p_in = jnp.pad(p_shift, ((0, 0), (0, t_pad - t), (0, 0)))
        gate_in = jnp.pad(gate, ((0, 0), (0, t_pad - t), (0, 0)))
    else:
        p_in, gate_in = p_shift, gate
    gb, s_bd = _rwkv_branch(p_in, shift_prev, s0_bd, gate_in, lv["mu"], lv["vecs"], lv["w_lora"],
                            n_valid=CHUNK - (t_pad - t))
    gb = gb[:, :t]

    if caches is None:
        gc = _attn_prompt(q, kv, bias, _sink_rows(lv["sink"], CHUNK), gate)
    else:
        gc = _attn_sample(q, kv, caches[0], caches[1], bias, _sink_rows(lv["sink"], t), gate)

    merged = _merge(ga.reshape(b * t, -1), gb.reshape(b * t, -1), gc.reshape(b * t, -1), wts["w_branch"], l, mg)
    outs = _out_proj(merged, wts["w_out"], l, x2, next_gain, final_norm)
    y, h_next = (outs[0], None) if final_norm else outs
    return (y, h_next, kv[..., :KV_WIDTH], kv[..., KV_WIDTH:], _state_from_blockdiag(s_bd), p_shift[:, -1],
            u_pool[:, t - POOL_HIST:])


def kernel(x_prompt, x_sample, cache_attn_k, cache_attn_v, state_rwkv, state_rwkv_shift, state_pool, norm_g, w_in,
           pool_w, pool_scale, rwkv_mu, rwkv_w0, rwkv_w_up, rwkv_a0, rwkv_a_up, rwkv_k_k, rwkv_k_a, rwkv_r_k,
           rwkv_gn_w, rwkv_gn_b, attn_sink, rel_pos_table, w_branch, w_out, final_norm_g):
    depth = norm_g.shape[0]
    bp, tp, _ = x_prompt.shape
    bs, ts, _ = x_sample.shape
    n_cache = cache_attn_k.shape[2]

    key_p = jnp.arange((BAND_CHUNKS + 1) * CHUNK) - BAND_CHUNKS * CHUNK
    bias_p = _rel_bias(rel_pos_table, key_p, CHUNK)
    key_s = jnp.concatenate([jnp.arange(n_cache) - n_cache, jnp.arange(ts)])
    bias_s = _rel_bias(rel_pos_table, key_s, ts)

    wts = dict(w_in=w_in.astype(BF16), pool_w=pool_w.astype(BF16), w_branch=w_branch.astype(BF16),
               w_out=w_out.astype(BF16))
    xp, xs = x_prompt.reshape(bp * tp, -1), x_sample.reshape(bs * ts, -1)
    hp = _rmsnorm(xp, norm_g[0], BF16, ROW_TM)
    hs = _rmsnorm(xs, norm_g[0], BF16, ROW_TM)
    outs_p, outs_s = [], []
    for l in range(depth):
        lv = _layer_vectors(l, pool_scale, rwkv_mu, rwkv_w0, rwkv_w_up, rwkv_a0, rwkv_a_up, rwkv_k_k, rwkv_k_a,
                            rwkv_r_k, rwkv_gn_w, rwkv_gn_b, attn_sink)
        last = l == depth - 1
        next_gain = final_norm_g if last else norm_g[l + 1]
        xp, hp, kp, vp, sp, shp, plp = _mixer_layer(
            xp, hp, bp, l, wts, lv, next_gain, last, pos0=0,
            pool_hist16=None,
            shift_prev=jnp.zeros((bp, 1, SHIFT_WIDTH), F32),
            s0_bd=jnp.zeros((bp, PAIRS, LANES, LANES), F32),
            bias=bias_p, caches=None)
        outs_p.append((kp[:, -WINDOW:].reshape(bp, WINDOW, ATT_KV_HEADS, HEAD_DIM),
                       vp[:, -WINDOW:].reshape(bp, WINDOW, ATT_KV_HEADS, HEAD_DIM), sp, shp, plp))
        xs, hs, k_s, v_s, s_s, sh_s, pl_s = _mixer_layer(
            xs, hs, bs, l, wts, lv, next_gain, last, pos0=PAST_LEN,
            pool_hist16=jnp.pad(state_pool[l], ((0, 0), (1, 0), (0, 0))),
            shift_prev=state_rwkv_shift[l][:, None, :],
            s0_bd=_state_to_blockdiag(state_rwkv[l]),
            bias=bias_s,
            caches=(cache_attn_k[l].reshape(bs, n_cache, KV_WIDTH), cache_attn_v[l].reshape(bs, n_cache, KV_WIDTH)))
        outs_s.append((k_s.reshape(bs, ts, ATT_KV_HEADS, HEAD_DIM), v_s.reshape(bs, ts, ATT_KV_HEADS, HEAD_DIM),
                       s_s, sh_s, pl_s))

    st = lambda outs, i: jnp.stack([o[i] for o in outs])
    return (xp.reshape(x_prompt.shape), xs.reshape(x_sample.shape), st(outs_p, 0), st(outs_p, 1), st(outs_s, 0), st(outs_s, 1),
            st(outs_p, 2), st(outs_s, 2), st(outs_p, 3), st(outs_s, 3), st(outs_p, 4), st(outs_s, 4))
```

```python
import functools
import math

import jax
import jax.numpy as jnp
import numpy as np
from jax import lax
from jax.experimental import pallas as pl
from jax.experimental.pallas import tpu as pltpu

F32 = jnp.float32
BF16 = jnp.bfloat16

D_MODEL = 2048
BRANCH_WIDTH = 1024
POOL_WINDOWS = (2, 4, 8, 16)
POOL_GROUP_DIM = BRANCH_WIDTH // len(POOL_WINDOWS)
POOL_HIST = max(POOL_WINDOWS) - 1
HEAD_DIM = 64
RWKV_HEADS = BRANCH_WIDTH // HEAD_DIM
LORA_RANK = 64
SHIFT_WIDTH = 3 * BRANCH_WIDTH + 2 * LORA_RANK
GN_EPS = 64e-5
ATT_HEADS = 16
ATT_KV_HEADS = 4
ATT_GROUP = ATT_HEADS // ATT_KV_HEADS
KV_WIDTH = ATT_KV_HEADS * HEAD_DIM
CHUNK = 64
WINDOW = 128
BAND_CHUNKS = WINDOW // CHUNK
N_BUCKETS = 32
MAX_DISTANCE = 128
NEG_INF = -1e30
RMS_EPS = 1e-6
PAST_LEN = 2048
N_BRANCH = 3
OFF_POOL = 0
OFF_SHIFT = OFF_POOL + BRANCH_WIDTH
OFF_Q = OFF_SHIFT + SHIFT_WIDTH
OFF_K = OFF_Q + BRANCH_WIDTH
OFF_V = OFF_K + KV_WIDTH
OFF_GATE = OFF_V + KV_WIDTH
OFF_MERGE = OFF_GATE + N_BRANCH * BRANCH_WIDTH

SUBLANES = 8
LANES = 128
PAIRS = BRANCH_WIDTH // LANES
ATTN_CHUNKS_PER_STEP = 8
RWKV_CHUNKS_PER_STEP = 8
RWKV_GROUP = 8
KEY_PAD = 256
ATT_HEAD_ORDER = tuple(8 * m + o for m in range(ATT_KV_HEADS // 2) for o in (0, 2, 5, 7, 1, 3, 4, 6))
POOL_FUSED_TM = 1024
ROW_TM = 512
MERGE_TM, MERGE_TN = 1024, 1024
PROJ_SEGMENTS = {
    "pool": (OFF_POOL, BRANCH_WIDTH, 1024, 1024, None, F32),
    "shift": (OFF_SHIFT, SHIFT_WIDTH, SHIFT_WIDTH, 512, None, F32),
    "q": (OFF_Q, BRANCH_WIDTH, 1024, 1024, "att_scale", BF16),
    "kv": (OFF_K, 2 * KV_WIDTH, 2 * KV_WIDTH, 1024, None, F32),
    "gate": (OFF_GATE, N_BRANCH * BRANCH_WIDTH, 3072, 1024, "silu", BF16),
    "merge": (OFF_MERGE, N_BRANCH * D_MODEL, 3072, 1024, "sigmoid", BF16),
}
VMEM_LIMIT = 56 * 1024 * 1024


def _params(*sem):
    return pltpu.CompilerParams(dimension_semantics=sem, vmem_limit_bytes=VMEM_LIMIT)


def _sigmoid(x):
    return 1.0 / (1.0 + jnp.exp(-x))


def _dot(a, b):
    return jnp.dot(a, b, preferred_element_type=F32)


def _dot_nt(a, b):
    return lax.dot_general(a, b, (((1,), (1,)), ((), ())), preferred_element_type=F32)


def _split3(x):
    hi = x.astype(BF16)
    r1 = x - hi.astype(F32)
    mid = r1.astype(BF16)
    lo = (r1 - mid.astype(F32)).astype(BF16)
    return hi, mid, lo


def _dot3_right_exact(x, b):
    hi, mid, lo = _split3(x)
    return _dot(hi, b) + _dot(mid, b) + _dot(lo, b)


def _dot3_left_exact(a, x):
    hi, mid, lo = _split3(x)
    return _dot(a, hi) + _dot(a, mid) + _dot(a, lo)


def _rmsnorm_kernel(x_ref, g_ref, o_ref):
    x = x_ref[...]
    ms = jnp.mean(x * x, axis=-1, keepdims=True)
    o_ref[...] = (x * lax.rsqrt(ms + RMS_EPS) * g_ref[...]).astype(o_ref.dtype)


def _rmsnorm(x2d, g, out_dtype, tm):
    m, d = x2d.shape
    tm = min(tm, m)
    return pl.pallas_call(
        _rmsnorm_kernel,
        grid=(m // tm,),
        in_specs=[pl.BlockSpec((tm, d), lambda i: (i, 0)), pl.BlockSpec((1, d), lambda i: (0, 0))],
        out_specs=pl.BlockSpec((tm, d), lambda i: (i, 0)),
        out_shape=jax.ShapeDtypeStruct((m, d), out_dtype),
        compiler_params=_params("parallel"),
        name="rmsnorm",
    )(x2d, g.reshape(1, d))


def _mm_kernel(a_ref, w_ref, o_ref, *, act):
    acc = _dot(a_ref[...], w_ref[0])
    if act == "silu":
        acc = acc * _sigmoid(acc)
    elif act == "sigmoid":
        acc = _sigmoid(acc)
    elif act == "att_scale":
        acc = acc * (HEAD_DIM ** -0.5)
    o_ref[...] = acc.astype(o_ref.dtype)


def _matmul(a, w, layer, col0, n, *, tn, tm, act, out_dtype, name):
    m, k = a.shape
    tm = min(tm, m)
    return pl.pallas_call(
        functools.partial(_mm_kernel, act=act),
        grid=(n // tn, m // tm),
        in_specs=[pl.BlockSpec((tm, k), lambda j, i: (i, 0)),
                  pl.BlockSpec((pl.Element(1), pl.Element(k), pl.Element(tn)),
                               lambda j, i: (layer, 0, pl.multiple_of(col0 + j * tn, LANES)))],
        out_specs=pl.BlockSpec((tm, tn), lambda j, i: (i, j)),
        out_shape=jax.ShapeDtypeStruct((m, n), out_dtype),
        compiler_params=_params("arbitrary", "arbitrary"),
        name=name,
    )(a, w)


def _pool_mix(u, prev, pos, zg_ref, w_ref, scale_ref, o_ref):
    for g, w in enumerate(POOL_WINDOWS):
        sl = slice(g * POOL_GROUP_DIM, (g + 1) * POOL_GROUP_DIM)
        e = jnp.concatenate([prev[:, sl], u[:, sl]], axis=0)
        step = 1
        while step < w:
            e = e + pltpu.roll(e, step, 0)
            step *= 2
        s = e[16:, :]
        cnt = jnp.minimum(w, pos + 1).astype(F32)
        pooled = s / cnt - u[:, sl]
        o = _dot(pooled.astype(BF16), w_ref[g]) * scale_ref[:, sl]
        o_ref[:, sl] = (o * zg_ref[:, sl].astype(F32)).astype(o_ref.dtype)


def _pool_kernel(u_ref, prev_ref, hist_ref, zg_ref, w_ref, scale_ref, o_ref, *, tt, pos0):
    i = pl.program_id(1)
    prev = jnp.where(i == 0, hist_ref[0], prev_ref[0])
    pos = pos0 + i * tt + lax.broadcasted_iota(jnp.int32, (tt, POOL_GROUP_DIM), 0)
    _pool_mix(u_ref[0], prev, pos, zg_ref.at[0], w_ref, scale_ref, o_ref.at[0])


def _proj_pool_kernel(a_ref, w_ref, wkv_ref, zg_ref, pw_ref, scale_ref, u_ref, o_ref, kv_ref, carry_ref,
                      *, tm, tiles_per_seq):
    step = lax.rem(pl.program_id(0), tiles_per_seq)
    a = a_ref[...]
    kv_ref[...] = _dot(a, wkv_ref[0])
    u = _dot(a, w_ref[0])
    u_ref[...] = u
    prev = jnp.where(step == 0, 0.0, carry_ref[...])
    carry_ref[...] = u[tm - 16:, :]
    pos = step * tm + lax.broadcasted_iota(jnp.int32, (tm, POOL_GROUP_DIM), 0)
    _pool_mix(u, prev, pos, zg_ref, pw_ref, scale_ref, o_ref)


def _proj_pool(h, w_in, layer, gate2d, pool_w, pool_scale, seq_len):
    m, k = h.shape
    c = BRANCH_WIDTH
    tm = min(POOL_FUSED_TM, seq_len)
    return pl.pallas_call(
        functools.partial(_proj_pool_kernel, tm=tm, tiles_per_seq=seq_len // tm),
        grid=(m // tm,),
        in_specs=[
            pl.BlockSpec((tm, k), lambda i: (i, 0)),
            pl.BlockSpec((pl.Element(1), pl.Element(k), pl.Element(c)), lambda i: (layer, 0, OFF_POOL)),
            pl.BlockSpec((pl.Element(1), pl.Element(k), pl.Element(2 * KV_WIDTH)), lambda i: (layer, 0, OFF_K)),
            pl.BlockSpec((tm, c), lambda i: (i, 0)),
            pl.BlockSpec((None,) + pool_w.shape[1:], lambda i: (layer, 0, 0, 0)),
            pl.BlockSpec((1, c), lambda i: (0, 0)),
        ],
        out_specs=[pl.BlockSpec((tm, c), lambda i: (i, 0)), pl.BlockSpec((tm, c), lambda i: (i, 0)),
                   pl.BlockSpec((tm, 2 * KV_WIDTH), lambda i: (i, 0))],
        out_shape=[jax.ShapeDtypeStruct((m, c), F32), jax.ShapeDtypeStruct((m, c), BF16),
                   jax.ShapeDtypeStruct((m, 2 * KV_WIDTH), F32)],
        scratch_shapes=[pltpu.VMEM((16, c), F32)],
        compiler_params=_params("arbitrary"),
        name="proj_pool_fused",
    )(h, w_in, w_in, gate2d, pool_w, pool_scale)


def _pool_branch(u, hist16, gate, pool_w, layer, pool_scale, pos0):
    b, t, c = u.shape
    tt = min(t, 512)
    sub = tt // 16
    return pl.pallas_call(
        functools.partial(_pool_kernel, tt=tt, pos0=pos0),
        grid=(b, t // tt),
        in_specs=[
            pl.BlockSpec((1, tt, c), lambda bi, i: (bi, i, 0)),
            pl.BlockSpec((1, 16, c), lambda bi, i: (bi, jnp.maximum(i * sub - 1, 0), 0)),
            pl.BlockSpec((1, 16, c), lambda bi, i: (bi, 0, 0)),
            pl.BlockSpec((1, tt, c), lambda bi, i: (bi, i, 0)),
            pl.BlockSpec((None,) + pool_w.shape[1:], lambda bi, i: (layer, 0, 0, 0)),
            pl.BlockSpec((1, c), lambda bi, i: (0, 0)),
        ],
        out_specs=pl.BlockSpec((1, tt, c), lambda bi, i: (bi, i, 0)),
        out_shape=jax.ShapeDtypeStruct((b, t, c), BF16),
        compiler_params=_params("parallel", "parallel"),
        name="pool_branch",
    )(u, u, hist16, gate, pool_w, pool_scale)


def _rwkv_kernel(p_ref, sprev_ref, s0_ref, zg_ref, mu_ref, vec_ref, wl_ref, o_ref, s_ref, carry_ref,
                 *, n_valid, n_sub):
    @pl.when(pl.program_id(1) == 0)
    def _():
        s_ref[...] = s0_ref[...]
        carry_ref[...] = sprev_ref[0]

    for sub in range(n_sub):
        rows = pl.ds(sub * CHUNK, CHUNK)
        _rwkv_chunk(p_ref.at[:, rows, :], zg_ref.at[:, rows, :], mu_ref, vec_ref, wl_ref, o_ref.at[:, rows, :],
                    s_ref, carry_ref, n_valid=n_valid)


def _rwkv_chunk(p_ref, zg_ref, mu_ref, vec_ref, wl_ref, o_ref, s_ref, carry_ref, *, n_valid):
    C = CHUNK
    p = p_ref[0]
    first_row = lax.broadcasted_iota(jnp.int32, (SUBLANES, SHIFT_WIDTH), 0) == 0
    row = lax.broadcasted_iota(jnp.int32, (C, BRANCH_WIDTH), 0)
    shifted = pltpu.roll(p, 1, 0)
    prev = jnp.concatenate([jnp.where(first_row, carry_ref[...], shifted[:SUBLANES]), shifted[SUBLANES:]], axis=0)
    carry_ref[...] = p[C - 1:C, :]
    xs = p + (prev - p) * mu_ref[...]

    w0, a0, k_k, k_a, r_k, gn_w, gn_b = (vec_ref[i:i + 1, :] for i in range(7))
    lane = lax.broadcasted_iota(jnp.int32, (C, LANES), 1)
    head0 = lane < HEAD_DIM
    lin = xs[:, 3 * BRANCH_WIDTH:]
    lin = jnp.where(head0, jnp.tanh(lin), lin)
    lora = _dot(lin.astype(BF16), wl_ref[...])
    log_decay = -math.exp(-0.5) * _sigmoid(w0 + lora[:, :BRANCH_WIDTH])
    iclr = _sigmoid(a0 + lora[:, BRANCH_WIDTH:])
    r = xs[:, :BRANCH_WIDTH]
    k = xs[:, BRANCH_WIDTH:2 * BRANCH_WIDTH]
    v = xs[:, 2 * BRANCH_WIDTH:3 * BRANCH_WIDTH]
    kk_raw = k * k_k
    k2 = k * (1.0 + (iclr - 1.0) * k_a)
    if n_valid < C:
        ok = row < n_valid
        log_decay = jnp.where(ok, log_decay, 0.0)
        kk_raw = jnp.where(ok, kk_raw, 0.0)
        k2 = jnp.where(ok, k2, 0.0)
        r = jnp.where(ok, r, 0.0)
        v = jnp.where(ok, v, 0.0)

    ti = lax.broadcasted_iota(jnp.int32, (C, C), 0)
    si = lax.broadcasted_iota(jnp.int32, (C, C), 1)
    tri = (si <= ti).astype(BF16)
    cum = _dot3_left_exact(tri, log_decay)
    cum_x = cum - log_decay
    cum_c = cum[C - 1:C, :]

    ri =lax.broadcasted_iota(jnp.int32, (2 * C, 2 * C), 0)
    ci = lax.broadcasted_iota(jnp.int32, (2 * C, 2 * C), 1)

    def stack(x):
        return jnp.concatenate([jnp.where(head0, x, 0.0), jnp.where(head0, 0.0, x)], axis=0)

    def each(f, *cols):
        return [f(*xs) for xs in zip(*cols)]

    def bf(xs):
        return [x.astype(BF16) for x in xs]

    def head_sums(xs):
        lo = [jnp.sum(jnp.where(head0, x, 0.0), axis=-1, keepdims=True) for x in xs]
        hi = [jnp.sum(jnp.where(head0, 0.0, x), axis=-1, keepdims=True) for x in xs]
        return [jnp.where(head0, a, b) for a, b in zip(lo, hi)]

    def stack_b(x):
        xb = x.astype(BF16)
        zero = jnp.zeros((), BF16)
        return jnp.concatenate([jnp.where(head0, xb, zero), jnp.where(head0, zero, xb)], axis=0)

    fold_strict = (ci & (C - 1)) < (ri & (C - 1))
    fold_incl = (ci & (C - 1)) <= (ri & (C - 1))
    zeros_blk = jnp.zeros((C, LANES), BF16)

    low_lanes = lax.broadcasted_iota(jnp.int32, (C, 2 * C), 1) < C

    def unfold_p(x):
        moved = pltpu.roll(x[C:], C, 1)
        return jnp.concatenate([jnp.where(low_lanes, x[:C], 0.0), jnp.where(low_lanes, 0.0, moved)], axis=0)

    def halves(x):
        return x[:C], x[C:]

    for g0 in range(0, PAIRS, RWKV_GROUP):
        prs = list(range(g0, g0 + RWKV_GROUP))
        sls = [slice(pr * LANES, (pr + 1) * LANES) for pr in prs]
        r_p = [r[:, sl] for sl in sls]
        k_p = [k2[:, sl] for sl in sls]
        v_p = [v[:, sl] for sl in sls]
        kkr = [kk_raw[:, sl] for sl in sls]
        ic = [iclr[:, sl] for sl in sls]
        cm = [cum[:, sl] for sl in sls]
        cx = [cum_x[:, sl] for sl in sls]
        cc = [cum_c[:, sl] for sl in sls]
        n2 = head_sums(each(lambda x: x * x, kkr))
        kk = each(lambda x, n: x * lax.rsqrt(jnp.maximum(n, 1e-24)), kkr, n2)
        b_p = each(lambda x, i: x * i, kk, ic)
        e_neg = each(lambda c_: jnp.exp(-c_), cm)
        e_end = each(lambda c_, e_: jnp.exp(e_ - c_), cm, cc)
        w_c = each(jnp.exp, cc)
        rt = each(lambda x, c_: x * jnp.exp(c_), r_p, cm)
        at_b = each(lambda x, c_: stack_b(-x * jnp.exp(c_)), kk, cx)
        rt_b = each(stack_b, rt)
        v_s = each(stack, v_p)
        v_b = bf(v_s)
        bb_b = each(lambda x, e_: stack_b(x * e_), b_p, e_end)
        kb_b = each(lambda x, e_: stack_b(x * e_), k_p, e_end)
        bk_b = bf(each(lambda b_, k_, e_: jnp.concatenate([b_ * e_, k_ * e_], axis=0), b_p, k_p, e_neg))
        ar_b = each(lambda a_, r_: jnp.concatenate([a_, r_], axis=0), at_b, rt_b)
        prod = each(_dot_nt, ar_b, bk_b)
        fa = each(lambda x: jnp.where(fold_strict, x[:2 * C], 0.0), prod)
        fa_b = bf(fa)
        fr_b = bf(each(lambda x: jnp.where(fold_incl, x[2 * C:], 0.0), prod))
        a_ab = each(unfold_p, fa)
        tinv = each(lambda x: jnp.where(ri == ci, 1.0, x), a_ab)
        pw_b = bf(a_ab)
        pw_b = bf(each(_dot, pw_b, pw_b))
        n_sq = int(math.log2(C)) - 1
        for step in range(n_sq):
            if step < n_sq - 1:
                both = each(lambda t_, p_: _dot(p_, jnp.concatenate([t_.astype(BF16), p_], axis=1)), tinv, pw_b)
                tinv = each(lambda t_, x: t_ + x[:, :2 * C], tinv, both)
                pw_b = bf([x[:, 2 * C:] for x in both])
            else:
                tinv = each(lambda t_, p_: t_ + _dot(p_, t_.astype(BF16)), tinv, pw_b)
        t_b = bf(tinv)
        def akv_of(f_, v_):
            return jnp.concatenate([_dot(fh, jnp.concatenate([zeros_blk, vh], axis=0))
                                    for fh, vh in zip(halves(f_), halves(v_))], axis=0)
        akv = each(akv_of, fa_b, v_b)
        au = each(lambda t_, a_, k_: _dot(t_, jnp.concatenate([a_, k_.astype(BF16)], axis=1)), t_b, at_b, akv)
        au_b = bf(au)

        def r_side(f_, au_, v_):
            parts = [_dot(fh, jnp.concatenate([ah, jnp.concatenate([zeros_blk, vh], axis=1)], axis=0))
                     for fh, ah, vh in zip(halves(f_), halves(au_), halves(v_))]
            return parts[0] + parts[1]
        ry = each(r_side, fr_b, au_b, v_b)
        rh = each(lambda x, y_: x + y_[:, :LANES], rt, ry)
        y0 = [y_[:, LANES:] for y_ in ry]
        mg = each(lambda x, b_: _dot(x.T.astype(BF16), b_), au, bb_b)
        m_lr = [x[:LANES] for x in mg]
        g_bd = each(lambda x, v_, k_: x[LANES:] + _dot(v_.T.astype(BF16), k_), mg, v_s, kb_b)
        s_bd = [s_ref[0, pr] for pr in prs]
        s_b = bf(s_bd)
        y = each(lambda x, s_, y_: _dot_nt(x.astype(BF16), s_) + y_, rh, s_b, y0)
        s_new = each(lambda s_, w_, sb_, m_, g_: s_ * w_ + _dot(sb_, m_.astype(BF16)) + g_,
                     s_bd, w_c, s_b, m_lr, g_bd)
        for pr, s_ in zip(prs, s_new):
            s_ref[0, pr] = s_
        yc = each(lambda x, m_: x - m_ * (1.0 / HEAD_DIM), y, head_sums(y))
        var = head_sums(each(lambda x: x * x, yc))
        rk = head_sums(each(lambda r_, k_, sl: r_ * k_ * r_k[:, sl], r_p, k_p, sls))
        for sl, yc_, var_, rk_, v_ in zip(sls, yc, var, rk, v_p):
            yn = yc_ * lax.rsqrt(var_ * (1.0 / HEAD_DIM) + GN_EPS) * gn_w[:, sl] + gn_b[:, sl]
            o_ref[0, :, sl] = ((yn + rk_ * v_) * zg_ref[0, :, sl].astype(F32)).astype(o_ref.dtype)


def _rwkv_branch(p_shift, shift_prev, s0_bd, gate, mu, vecs, w_lora, n_valid):
    b, t, _ = p_shift.shape
    n_sub = math.gcd(t // CHUNK, RWKV_CHUNKS_PER_STEP)
    rows = n_sub * CHUNK
    return pl.pallas_call(
        functools.partial(_rwkv_kernel, n_valid=n_valid, n_sub=n_sub),
        grid=(b, t // rows),
        in_specs=[
            pl.BlockSpec((1, rows, SHIFT_WIDTH), lambda bi, c: (bi, c, 0)),
            pl.BlockSpec((1, 1, SHIFT_WIDTH), lambda bi, c: (bi, 0, 0)),
            pl.BlockSpec((1, PAIRS, LANES, LANES), lambda bi, c: (bi, 0, 0, 0)),
            pl.BlockSpec((1, rows, BRANCH_WIDTH), lambda bi, c: (bi, c, 1)),
            pl.BlockSpec((1, SHIFT_WIDTH), lambda bi, c: (0, 0)),
            pl.BlockSpec((8, BRANCH_WIDTH), lambda bi, c: (0, 0)),
            pl.BlockSpec((LANES, 2 * BRANCH_WIDTH), lambda bi, c: (0, 0)),
        ],
        out_specs=[
            pl.BlockSpec((1, rows, BRANCH_WIDTH), lambda bi, c: (bi, c, 0)),
            pl.BlockSpec((1, PAIRS, LANES, LANES), lambda bi, c: (bi, 0, 0, 0)),
        ],
        out_shape=[
            jax.ShapeDtypeStruct((b, t, BRANCH_WIDTH), BF16),
            jax.ShapeDtypeStruct((b, PAIRS, LANES, LANES), F32),
        ],
        scratch_shapes=[pltpu.VMEM((1, SHIFT_WIDTH), F32)],
        compiler_params=_params("parallel", "arbitrary"),
        name="rwkv_branch",
    )(p_shift, shift_prev, s0_bd, gate, mu, vecs, w_lora)


def _state_to_blockdiag(s):
    b = s.shape[0]
    sp = s.reshape(b, PAIRS, 2, HEAD_DIM, HEAD_DIM)
    z = jnp.zeros_like(sp[:, :, 0])
    top = jnp.concatenate([sp[:, :, 0], z], axis=-1)
    bot = jnp.concatenate([z, sp[:, :, 1]], axis=-1)
    return jnp.concatenate([top, bot], axis=-2)


def _state_from_blockdiag(s_bd):
    b = s_bd.shape[0]
    h0 = s_bd[:, :, :HEAD_DIM, :HEAD_DIM]
    h1 = s_bd[:, :, HEAD_DIM:, HEAD_DIM:]
    return jnp.stack([h0, h1], axis=2).reshape(b, RWKV_HEADS, HEAD_DIM, HEAD_DIM)


def _attn_kernel(*refs, n_kv_parts, tq, n_keys, masked, n_sub):
    q_ref = refs[0]
    k_refs = refs[1:1 + n_kv_parts]
    v_refs = refs[1 + n_kv_parts:1 + 2 * n_kv_parts]
    bias_ref, sink_ref, zg_ref, o_ref = refs[1 + 2 * n_kv_parts:]
    k_rows = jnp.concatenate([kr[0] for kr in k_refs], axis=0)
    v_rows = jnp.concatenate([vr[0] for vr in v_refs], axis=0)
    for sub in range(n_sub):
        rows = pl.ds(sub * tq, tq)
        _attn_chunk(q_ref.at[:, rows, :], k_rows[sub * tq:sub * tq + n_keys], v_rows[sub * tq:sub * tq + n_keys],
                    bias_ref, sink_ref, zg_ref.at[:, rows, :], o_ref.at[:, rows, :],
                    n=pl.program_id(1) * n_sub + sub, tq=tq, n_keys=n_keys, masked=masked)


def _attn_chunk(q_ref, k_seen, v_seen, bias_ref, sink_ref, zg_ref, o_ref, *, n, tq, n_keys, masked):
    pad = jnp.zeros((KEY_PAD - n_keys, KV_WIDTH), F32)
    k_all = jnp.concatenate([k_seen, pad], axis=0)
    v_all = jnp.concatenate([v_seen, pad], axis=0)
    lane =lax.broadcasted_iota(jnp.int32, (tq, LANES), 1)
    head0 = lane < HEAD_DIM
    col = lax.broadcasted_iota(jnp.int32, (1, KEY_PAD), 1)
    valid = col < n_keys
    if masked:
        valid = valid & (col // CHUNK + n - BAND_CHUNKS >= 0)
    q = q_ref[0]
    zero = jnp.zeros((), q.dtype)
    lo_half = lambda s: jnp.where(head0, s, zero)
    hi_half = lambda s: jnp.where(head0, zero, s)
    lhs, keys, vals = [], [], []
    for m in range(ATT_KV_HEADS // 2):
        slabs = [q[:, (4 * m + i) * LANES:(4 * m + i + 1) * LANES] for i in range(4)]
        k_sl = k_all[:, m * LANES:(m + 1) * LANES]
        v_sl = v_all[:, m * LANES:(m + 1) * LANES]
        lhs.append(jnp.concatenate([lo_half(slabs[0]), lo_half(slabs[1]), hi_half(slabs[2]), hi_half(slabs[3])], 0))
        lhs.append(jnp.concatenate([hi_half(slabs[0]), hi_half(slabs[1]), lo_half(slabs[2]), lo_half(slabs[3])], 0))
        keys += [k_sl.astype(BF16), pltpu.roll(k_sl, HEAD_DIM, 1).astype(BF16)]
        vals += [v_sl.astype(BF16), pltpu.roll(v_sl, HEAD_DIM, 1).astype(BF16)]
    n_grp = len(lhs)
    ones = jnp.ones((KEY_PAD, LANES), BF16)
    vals = [jnp.concatenate([x, ones], axis=1) for x in vals]
    logits = [jnp.where(valid, _dot_nt(lhs[i], keys[i]) + bias_ref[i], NEG_INF) for i in range(n_grp)]
    sink = [sink_ref[i] for i in range(n_grp)]
    mx = [jnp.maximum(jnp.max(logits[i], axis=-1, keepdims=True), sink[i]) for i in range(n_grp)]
    e = [jnp.exp(jnp.concatenate([logits[i][:, :LANES] - mx[i], logits[i][:, LANES:] - mx[i]], axis=1))
         for i in range(n_grp)]
    pv = [_dot(e[i].astype(BF16), vals[i]) for i in range(n_grp)]
    out = [pv[i][:, :LANES] / (pv[i][:, LANES:] + jnp.exp(sink[i] - mx[i])) for i in range(n_grp)]
    for m in range(ATT_KV_HEADS // 2):
        same, swap = out[2 * m], out[2 * m + 1]
        for i in range(4):
            a, b = same[i * tq:(i + 1) * tq], swap[i * tq:(i + 1) * tq]
            res = jnp.where(head0, a, b) if i < 2 else jnp.where(head0, b, a)
            sl = slice((4 * m + i) * LANES, (4 * m + i + 1) * LANES)
            o_ref[0, :, sl] = (res * zg_ref[0, :, sl].astype(F32)).astype(o_ref.dtype)


def _attn_prompt(q, kv, bias, sink_rows, gate):
    b, t, _ = q.shape
    nc = t // CHUNK
    n_sub = math.gcd(nc, ATTN_CHUNKS_PER_STEP)
    rows = n_sub * CHUNK
    k_specs, v_specs = ([pl.BlockSpec((1, CHUNK, KV_WIDTH),
                                      lambda bi, n, d=d, c=c: (bi, jnp.maximum(n * n_sub - d, 0), c))
                         for d in range(BAND_CHUNKS, 0, -1)]
                        + [pl.BlockSpec((1, rows, KV_WIDTH), lambda bi, n, c=c: (bi, n, c))] for c in range(2))
    return pl.pallas_call(
        functools.partial(_attn_kernel, n_kv_parts=BAND_CHUNKS + 1, tq=CHUNK,
                          n_keys=(BAND_CHUNKS + 1) * CHUNK, masked=True, n_sub=n_sub),
        grid=(b, nc // n_sub),
        in_specs=[pl.BlockSpec((1, rows, BRANCH_WIDTH), lambda bi, n: (bi, n, 0))]
        + k_specs + v_specs
        + [pl.BlockSpec(bias.shape, lambda bi, n: (0, 0, 0)),
           pl.BlockSpec(sink_rows.shape, lambda bi, n: (0, 0, 0)),
           pl.BlockSpec((1, rows, BRANCH_WIDTH), lambda bi, n: (bi, n, 2))],
        out_specs=pl.BlockSpec((1, rows, BRANCH_WIDTH), lambda bi, n: (bi, n, 0)),
        out_shape=jax.ShapeDtypeStruct((b, t, BRANCH_WIDTH), BF16),
        compiler_params=_params("parallel", "parallel"),
        name="attn_prompt",
    )(q, kv, kv, kv, kv, kv, kv, bias, sink_rows, gate)


def _attn_sample(q, kv, k_cache, v_cache, bias, sink_rows, gate):
    b, t, _ = q.shape
    n_cache = k_cache.shape[1]
    cache_spec = pl.BlockSpec((1, n_cache, KV_WIDTH), lambda bi, n: (bi, 0, 0))
    return pl.pallas_call(
        functools.partial(_attn_kernel, n_kv_parts=2, tq=t, n_keys=n_cache + t, masked=False, n_sub=1),
        grid=(b, 1),
        in_specs=[pl.BlockSpec((1, t, BRANCH_WIDTH), lambda bi, n: (bi, 0, 0)),
                  cache_spec, pl.BlockSpec((1, t, KV_WIDTH), lambda bi, n: (bi, 0, 0)),
                  cache_spec, pl.BlockSpec((1, t, KV_WIDTH), lambda bi, n: (bi, 0, 1)),
                  pl.BlockSpec(bias.shape, lambda bi, n: (0, 0, 0)),
                  pl.BlockSpec(sink_rows.shape, lambda bi, n: (0, 0, 0)),
                  pl.BlockSpec((1, t, BRANCH_WIDTH), lambda bi, n: (bi, 0, 2))],
        out_specs=pl.BlockSpec((1, t, BRANCH_WIDTH), lambda bi, n: (bi, 0, 0)),
        out_shape=jax.ShapeDtypeStruct((b, t, BRANCH_WIDTH), BF16),
        compiler_params=_params("parallel", "parallel"),
        name="attn_sample",
    )(q, k_cache, kv, v_cache, kv, bias, sink_rows, gate)


def _t5_bucket(rel):
    half = N_BUCKETS // 2
    n = -rel
    ret = jnp.where(n < 0, half, 0)
    n = jnp.abs(n)
    max_exact = half // 2
    large = max_exact + (jnp.log(jnp.maximum(n, 1).astype(jnp.float32) / max_exact)
                         / math.log(MAX_DISTANCE / max_exact) * (half - max_exact)).astype(jnp.int32)
    large = jnp.minimum(large, half - 1)
    return ret + jnp.where(n < max_exact, n, large)


def _bias_kernel(tab_ref, onehot_ref, o_ref):
    o_ref[...] = _dot3_right_exact(tab_ref[...], onehot_ref[...])


def _rel_bias(table, key_pos, n_q):
    n_keys = key_pos.shape[0]
    bucket = _t5_bucket(key_pos[None, :] - jnp.arange(n_q)[:, None])
    bucket = jnp.pad(bucket, ((0, 0), (0, KEY_PAD - n_keys)), constant_values=-1)
    onehot = (bucket.reshape(1, -1) == jnp.arange(N_BUCKETS)[:, None]).astype(BF16)
    out = pl.pallas_call(
        _bias_kernel,
        out_shape=jax.ShapeDtypeStruct((ATT_HEADS, n_q * KEY_PAD), F32),
        name="rel_bias",
    )(table.T[np.array(ATT_HEAD_ORDER)], onehot)
    return out.reshape(len(ATT_HEAD_ORDER) // 4, 4 * n_q, KEY_PAD)


def _merge_kernel(ga_ref, gb_ref, gc_ref, w_ref, m0_ref, m1_ref, m2_ref, o_ref):
    acc = m0_ref[...].astype(F32) * _dot(ga_ref[...], w_ref[0])
    acc = acc + m1_ref[...].astype(F32) * _dot(gb_ref[...], w_ref[1])
    acc = acc + m2_ref[...].astype(F32) * _dot(gc_ref[...], w_ref[2])
    o_ref[...] = acc.astype(o_ref.dtype)


def _merge(ga, gb, gc, w_branch, layer, mg):
    m, kdim = ga.shape
    tm = min(MERGE_TM, m)
    tn = MERGE_TN
    nj = D_MODEL // tn
    g_spec = pl.BlockSpec((tm, kdim), lambda j, i: (i, 0))
    m_specs = [pl.BlockSpec((tm, tn), lambda j, i, br=br: (i, br * nj + j)) for br in range(N_BRANCH)]
    return pl.pallas_call(
        _merge_kernel,
        grid=(nj, m // tm),
        in_specs=[g_spec, g_spec, g_spec,
                  pl.BlockSpec((None, N_BRANCH, kdim, tn), lambda j, i: (layer, 0, 0, j))] + m_specs,
        out_specs=pl.BlockSpec((tm, tn), lambda j, i: (i, j)),
        out_shape=jax.ShapeDtypeStruct((m, D_MODEL), BF16),
        compiler_params=_params("arbitrary", "arbitrary"),
        name="merge",
    )(ga, gb, gc, w_branch, mg, mg, mg)


def _out_kernel(a_ref, w_ref, x_ref, g_ref, *o_refs, final_norm):
    y = x_ref[...] + _dot(a_ref[...], w_ref[...])
    ms = jnp.mean(y * y, axis=-1, keepdims=True)
    normed = y * lax.rsqrt(ms + RMS_EPS) * g_ref[...]
    if final_norm:
        o_refs[0][...] = normed
    else:
        o_refs[0][...] = y
        o_refs[1][...] = normed.astype(o_refs[1].dtype)


def _out_proj(merged, w_out, layer, x2d, gain, final_norm):
    m, d = x2d.shape
    tm = min(ROW_TM, m)
    row_spec = pl.BlockSpec((tm, d), lambda i: (i, 0))
    out_specs, out_shape = [row_spec], [jax.ShapeDtypeStruct((m, d), F32)]
    if not final_norm:
        out_specs.append(row_spec)
        out_shape.append(jax.ShapeDtypeStruct((m, d), BF16))
    return pl.pallas_call(
        functools.partial(_out_kernel, final_norm=final_norm),
        grid=(m // tm,),
        in_specs=[row_spec, pl.BlockSpec((None, d, d), lambda i: (layer, 0, 0)), row_spec,
                  pl.BlockSpec((1, d), lambda i: (0, 0))],
        out_specs=out_specs,
        out_shape=out_shape,
        compiler_params=_params("parallel"),
        name="out_proj",
    )(merged, w_out, x2d, gain.reshape(1, d))


def _layer_vectors(l, pool_scale, rwkv_mu, rwkv_w0, rwkv_w_up, rwkv_a0, rwkv_a_up, rwkv_k_k, rwkv_k_a, rwkv_r_k,
                   rwkv_gn_w, rwkv_gn_b, attn_sink):
    w_lora = jnp.zeros((LANES, 2 * BRANCH_WIDTH), F32)
    w_lora = w_lora.at[:LORA_RANK, :BRANCH_WIDTH].set(rwkv_w_up[l]).at[LORA_RANK:, BRANCH_WIDTH:].set(rwkv_a_up[l])
    vecs = jnp.stack([rwkv_w0[l], rwkv_a0[l], rwkv_k_k[l], rwkv_k_a[l], rwkv_r_k[l], rwkv_gn_w[l], rwkv_gn_b[l],
                      jnp.zeros_like(rwkv_w0[l])])
    return dict(pool_scale=pool_scale[l].reshape(1, -1), mu=rwkv_mu[l].reshape(1, -1), vecs=vecs,
                w_lora=w_lora.astype(BF16), sink=attn_sink[l][np.array(ATT_HEAD_ORDER)])


def _sink_rows(sink_ordered, n_q):
    rows = jnp.repeat(sink_ordered.reshape(-1, 4), n_q, axis=1)
    return jnp.broadcast_to(rows[..., None], rows.shape + (LANES,))


def _mixer_layer(x2, h, b, l, wts, lv, next_gain, final_norm, *, pos0, pool_hist16, shift_prev, s0_bd, bias, caches):
    t = x2.shape[0] // b
    fuse_pool = pool_hist16 is None and t % min(POOL_FUSED_TM, t) == 0
    proj = {name: _matmul(h, wts["w_in"], l, col0, n, tn=tn, tm=tm, act=act, out_dtype=dt, name="proj_" + name)
            for name, (col0, n, tn, tm, act, dt) in PROJ_SEGMENTS.items()
            if not (fuse_pool and name in ("pool", "kv"))}
    p_shift, q, gate = (proj[name].reshape(b, t, -1) for name in ("shift", "q", "gate"))
    mg = proj["merge"]

    if fuse_pool:
        u_pool, ga, kv = _proj_pool(h, wts["w_in"], l, proj["gate"], wts["pool_w"], lv["pool_scale"], t)
        u_pool, kv = u_pool.reshape(b, t, -1), kv.reshape(b, t, -1)
    else:
        u_pool, kv = proj["pool"].reshape(b, t, -1), proj["kv"].reshape(b, t, -1)
        if pool_hist16 is None:
            pool_hist16 = jnp.zeros((b, 16, BRANCH_WIDTH), F32)
        ga = _pool_branch(u_pool, pool_hist16, gate, wts["pool_w"], l, lv["pool_scale"], pos0)

    t_pad = -(-t // CHUNK) * CHUNK
    if t_pad != t:
        p_in = jnp.pad(p_shift, ((0, 0), (0, t_pad - t), (0, 0)))
        gate_in = jnp.pad(gate, ((0, 0), (0, t_pad - t), (0, 0)))
    else:
        p_in, gate_in = p_shift, gate
    gb, s_bd = _rwkv_branch(p_in, shift_prev, s0_bd, gate_in, lv["mu"], lv["vecs"], lv["w_lora"],
                            n_valid=CHUNK - (t_pad - t))
    gb = gb[:, :t]

    if caches is None:
        gc = _attn_prompt(q, kv, bias, _sink_rows(lv["sink"], CHUNK), gate)
    else:
        gc = _attn_sample(q, kv, caches[0], caches[1], bias, _sink_rows(lv["sink"], t), gate)

    merged = _merge(ga.reshape(b * t, -1), gb.reshape(b * t, -1), gc.reshape(b * t, -1), wts["w_branch"], l, mg)
    outs = _out_proj(merged, wts["w_out"], l, x2, next_gain, final_norm)
    y, h_next = (outs[0], None) if final_norm else outs
    return (y, h_next, kv[..., :KV_WIDTH], kv[..., KV_WIDTH:], _state_from_blockdiag(s_bd), p_shift[:, -1],
            u_pool[:, t - POOL_HIST:])


def kernel(x_prompt, x_sample, cache_attn_k, cache_attn_v, state_rwkv, state_rwkv_shift, state_pool, norm_g, w_in,
           pool_w, pool_scale, rwkv_mu, rwkv_w0, rwkv_w_up, rwkv_a0, rwkv_a_up, rwkv_k_k, rwkv_k_a, rwkv_r_k,
           rwkv_gn_w, rwkv_gn_b, attn_sink, rel_pos_table, w_branch, w_out, final_norm_g):
    depth = norm_g.shape[0]
    bp, tp, _ = x_prompt.shape
    bs, ts, _ = x_sample.shape
    n_cache = cache_attn_k.shape[2]

    key_p = jnp.arange((BAND_CHUNKS + 1) * CHUNK) - BAND_CHUNKS * CHUNK
    bias_p = _rel_bias(rel_pos_table, key_p, CHUNK)
    key_s = jnp.concatenate([jnp.arange(n_cache) - n_cache, jnp.arange(ts)])
    bias_s = _rel_bias(rel_pos_table, key_s, ts)

    wts = dict(w_in=w_in.astype(BF16), pool_w=pool_w.astype(BF16), w_branch=w_branch.astype(BF16),
               w_out=w_out.astype(BF16))
    xp, xs = x_prompt.reshape(bp * tp, -1), x_sample.reshape(bs * ts, -1)
    hp = _rmsnorm(xp, norm_g[0], BF16, ROW_TM)
    hs = _rmsnorm(xs, norm_g[0], BF16, ROW_TM)
    outs_p, outs_s = [], []
    for l in range(depth):
        lv = _layer_vectors(l, pool_scale, rwkv_mu, rwkv_w0, rwkv_w_up, rwkv_a0, rwkv_a_up, rwkv_k_k, rwkv_k_a,
                            rwkv_r_k, rwkv_gn_w, rwkv_gn_b, attn_sink)
        last = l == depth - 1
        next_gain = final_norm_g if last else norm_g[l + 1]
        xp, hp, kp, vp, sp, shp, plp = _mixer_layer(
            xp, hp, bp, l, wts, lv, next_gain, last, pos0=0,
            pool_hist16=None,
            shift_prev=jnp.zeros((bp, 1, SHIFT_WIDTH), F32),
            s0_bd=jnp.zeros((bp, PAIRS, LANES, LANES), F32),
            bias=bias_p, caches=None)
        outs_p.append((kp[:, -WINDOW:].reshape(bp, WINDOW, ATT_KV_HEADS, HEAD_DIM),
                       vp[:, -WINDOW:].reshape(bp, WINDOW, ATT_KV_HEADS, HEAD_DIM), sp, shp, plp))
        xs, hs, k_s, v_s, s_s, sh_s, pl_s = _mixer_layer(
            xs, hs, bs, l, wts, lv, next_gain, last, pos0=PAST_LEN,
            pool_hist16=jnp.pad(state_pool[l], ((0, 0), (1, 0), (0, 0))),
            shift_prev=state_rwkv_shift[l][:, None, :],
            s0_bd=_state_to_blockdiag(state_rwkv[l]),
            bias=bias_s,
            caches=(cache_attn_k[l].reshape(bs, n_cache, KV_WIDTH), cache_attn_v[l].reshape(bs, n_cache, KV_WIDTH)))
        outs_s.append((k_s.reshape(bs, ts, ATT_KV_HEADS, HEAD_DIM), v_s.reshape(bs, ts, ATT_KV_HEADS, HEAD_DIM),
                       s_s, sh_s, pl_s))

    st = lambda outs, i: jnp.stack([o[i] for o in outs])
    return (xp.reshape(x_prompt.shape), xs.reshape(x_sample.shape), st(outs_p, 0), st(outs_p, 1), st(outs_s, 0), st(outs_s, 1),
            st(outs_p, 2), st(outs_s, 2), st(outs_p, 3), st(outs_s, 3), st(outs_p, 4), st(outs_s, 4))
```

```python
import functools
import math

import jax
import jax.numpy as jnp
import numpy as np
from jax import lax
from jax.experimental import pallas as pl
from jax.experimental.pallas import tpu as pltpu

F32 = jnp.float32
BF16 = jnp.bfloat16

D_MODEL = 2048
BRANCH_WIDTH = 1024
POOL_WINDOWS = (2, 4, 8, 16)
POOL_GROUP_DIM = BRANCH_WIDTH // len(POOL_WINDOWS)
POOL_HIST = max(POOL_WINDOWS) - 1
HEAD_DIM = 64
RWKV_HEADS = BRANCH_WIDTH // HEAD_DIM
LORA_RANK = 64
SHIFT_WIDTH = 3 * BRANCH_WIDTH + 2 * LORA_RANK
GN_EPS = 64e-5
ATT_HEADS = 16
ATT_KV_HEADS = 4
ATT_GROUP = ATT_HEADS // ATT_KV_HEADS
KV_WIDTH = ATT_KV_HEADS * HEAD_DIM
CHUNK = 64
WINDOW = 128
BAND_CHUNKS = WINDOW // CHUNK
N_BUCKETS = 32
MAX_DISTANCE = 128
NEG_INF = -1e30
RMS_EPS = 1e-6
PAST_LEN = 2048
N_BRANCH = 3
OFF_POOL = 0
OFF_SHIFT = OFF_POOL + BRANCH_WIDTH
OFF_Q = OFF_SHIFT + SHIFT_WIDTH
OFF_K = OFF_Q + BRANCH_WIDTH
OFF_V = OFF_K + KV_WIDTH
OFF_GATE = OFF_V + KV_WIDTH
OFF_MERGE = OFF_GATE + N_BRANCH * BRANCH_WIDTH

SUBLANES = 8
LANES = 128
PAIRS = BRANCH_WIDTH // LANES
ATTN_CHUNKS_PER_STEP = 8
RWKV_CHUNKS_PER_STEP = 8
RWKV_GROUP = 8
KEY_PAD = 256
ATT_HEAD_ORDER = tuple(8 * m + o for m in range(ATT_KV_HEADS // 2) for o in (0, 2, 5, 7, 1, 3, 4, 6))
POOL_FUSED_TM = 1024
ROW_TM = 512
MERGE_TM, MERGE_TN = 1024, 1024
PROJ_SEGMENTS = {
    "pool": (OFF_POOL, BRANCH_WIDTH, 1024, 1024, None, F32),
    "shift": (OFF_SHIFT, SHIFT_WIDTH, SHIFT_WIDTH, 512, None, F32),
    "q": (OFF_Q, BRANCH_WIDTH, 1024, 1024, "att_scale", BF16),
    "kv": (OFF_K, 2 * KV_WIDTH, 2 * KV_WIDTH, 1024, None, F32),
    "gate": (OFF_GATE, N_BRANCH * BRANCH_WIDTH, 3072, 1024, "silu", BF16),
    "merge": (OFF_MERGE, N_BRANCH * D_MODEL, 3072, 1024, "sigmoid", BF16),
}
VMEM_LIMIT = 56 * 1024 * 1024


def _params(*sem):
    return pltpu.CompilerParams(dimension_semantics=sem, vmem_limit_bytes=VMEM_LIMIT)


def _sigmoid(x):
    return 1.0 / (1.0 + jnp.exp(-x))


def _dot(a, b):
    return jnp.dot(a, b, preferred_element_type=F32)


def _dot_nt(a, b):
    return lax.dot_general(a, b, (((1,), (1,)), ((), ())), preferred_element_type=F32)


def _split3(x):
    hi = x.astype(BF16)
    r1 = x - hi.astype(F32)
    mid = r1.astype(BF16)
    lo = (r1 - mid.astype(F32)).astype(BF16)
    return hi, mid, lo


def _dot3_right_exact(x, b):
    hi, mid, lo = _split3(x)
    return _dot(hi, b) + _dot(mid, b) + _dot(lo, b)


def _dot3_left_exact(a, x):
    hi, mid, lo = _split3(x)
    return _dot(a, hi) + _dot(a, mid) + _dot(a, lo)


def _rmsnorm_kernel(x_ref, g_ref, o_ref):
    x = x_ref[...]
    ms = jnp.mean(x * x, axis=-1, keepdims=True)
    o_ref[...] = (x * lax.rsqrt(ms + RMS_EPS) * g_ref[...]).astype(o_ref.dtype)


def _rmsnorm(x2d, g, out_dtype, tm):
    m, d = x2d.shape
    tm = min(tm, m)
    return pl.pallas_call(
        _rmsnorm_kernel,
        grid=(m // tm,),
        in_specs=[pl.BlockSpec((tm, d), lambda i: (i, 0)), pl.BlockSpec((1, d), lambda i: (0, 0))],
        out_specs=pl.BlockSpec((tm, d), lambda i: (i, 0)),
        out_shape=jax.ShapeDtypeStruct((m, d), out_dtype),
        compiler_params=_params("parallel"),
        name="rmsnorm",
    )(x2d, g.reshape(1, d))


def _mm_kernel(a_ref, w_ref, o_ref, *, act):
    acc = _dot(a_ref[...], w_ref[0])
    if act == "silu":
        acc = acc * _sigmoid(acc)
    elif act == "sigmoid":
        acc = _sigmoid(acc)
    elif act == "att_scale":
        acc = acc * (HEAD_DIM ** -0.5)
    o_ref[...] = acc.astype(o_ref.dtype)


def _matmul(a, w, layer, col0, n, *, tn, tm, act, out_dtype, name):
    m, k = a.shape
    tm = min(tm, m)
    return pl.pallas_call(
        functools.partial(_mm_kernel, act=act),
        grid=(n // tn, m // tm),
        in_specs=[pl.BlockSpec((tm, k), lambda j, i: (i, 0)),
                  pl.BlockSpec((pl.Element(1), pl.Element(k), pl.Element(tn)),
                               lambda j, i: (layer, 0, pl.multiple_of(col0 + j * tn, LANES)))],
        out_specs=pl.BlockSpec((tm, tn), lambda j, i: (i, j)),
        out_shape=jax.ShapeDtypeStruct((m, n), out_dtype),
        compiler_params=_params("arbitrary", "arbitrary"),
        name=name,
    )(a, w)


def _pool_mix(u, prev, pos, zg_ref, w_ref, scale_ref, o_ref):
    for g, w in enumerate(POOL_WINDOWS):
        sl = slice(g * POOL_GROUP_DIM, (g + 1) * POOL_GROUP_DIM)
        e = jnp.concatenate([prev[:, sl], u[:, sl]], axis=0)
        step = 1
        while step < w:
            e = e + pltpu.roll(e, step, 0)
            step *= 2
        s = e[16:, :]
        cnt = jnp.minimum(w, pos + 1).astype(F32)
        pooled = s / cnt - u[:, sl]
        o = _dot(pooled.astype(BF16), w_ref[g]) * scale_ref[:, sl]
        o_ref[:, sl] = (o * zg_ref[:, sl].astype(F32)).astype(o_ref.dtype)


def _pool_kernel(u_ref, prev_ref, hist_ref, zg_ref, w_ref, scale_ref, o_ref, *, tt, pos0):
    i = pl.program_id(1)
    prev = jnp.where(i == 0, hist_ref[0], prev_ref[0])
    pos = pos0 + i * tt + lax.broadcasted_iota(jnp.int32, (tt, POOL_GROUP_DIM), 0)
    _pool_mix(u_ref[0], prev, pos, zg_ref.at[0], w_ref, scale_ref, o_ref.at[0])


def _proj_pool_kernel(a_ref, w_ref, wkv_ref, zg_ref, pw_ref, scale_ref, u_ref, o_ref, kv_ref, carry_ref,
                      *, tm, tiles_per_seq):
    step = lax.rem(pl.program_id(0), tiles_per_seq)
    a = a_ref[...]
    kv_ref[...] = _dot(a, wkv_ref[0])
    u = _dot(a, w_ref[0])
    u_ref[...] = u
    prev = jnp.where(step == 0, 0.0, carry_ref[...])
    carry_ref[...] = u[tm - 16:, :]
    pos = step * tm + lax.broadcasted_iota(jnp.int32, (tm, POOL_GROUP_DIM), 0)
    _pool_mix(u, prev, pos, zg_ref, pw_ref, scale_ref, o_ref)


def _proj_pool(h, w_in, layer, gate2d, pool_w, pool_scale, seq_len):
    m, k = h.shape
    c = BRANCH_WIDTH
    tm = min(POOL_FUSED_TM, seq_len)
    return pl.pallas_call(
        functools.partial(_proj_pool_kernel, tm=tm, tiles_per_seq=seq_len // tm),
        grid=(m // tm,),
        in_specs=[
            pl.BlockSpec((tm, k), lambda i: (i, 0)),
            pl.BlockSpec((pl.Element(1), pl.Element(k), pl.Element(c)), lambda i: (layer, 0, OFF_POOL)),
            pl.BlockSpec((pl.Element(1), pl.Element(k), pl.Element(2 * KV_WIDTH)), lambda i: (layer, 0, OFF_K)),
            pl.BlockSpec((tm, c), lambda i: (i, 0)),
            pl.BlockSpec((None,) + pool_w.shape[1:], lambda i: (layer, 0, 0, 0)),
            pl.BlockSpec((1, c), lambda i: (0, 0)),
        ],
        out_specs=[pl.BlockSpec((tm, c), lambda i: (i, 0)), pl.BlockSpec((tm, c), lambda i: (i, 0)),
                   pl.BlockSpec((tm, 2 * KV_WIDTH), lambda i: (i, 0))],
        out_shape=[jax.ShapeDtypeStruct((m, c), F32), jax.ShapeDtypeStruct((m, c), BF16),
                   jax.ShapeDtypeStruct((m, 2 * KV_WIDTH), F32)],
        scratch_shapes=[pltpu.VMEM((16, c), F32)],
        compiler_params=_params("arbitrary"),
        name="proj_pool_fused",
    )(h, w_in, w_in, gate2d, pool_w, pool_scale)


def _pool_branch(u, hist16, gate, pool_w, layer, pool_scale, pos0):
    b, t, c = u.shape
    tt = min(t, 512)
    sub = tt // 16
    return pl.pallas_call(
        functools.partial(_pool_kernel, tt=tt, pos0=pos0),
        grid=(b, t // tt),
        in_specs=[
            pl.BlockSpec((1, tt, c), lambda bi, i: (bi, i, 0)),
            pl.BlockSpec((1, 16, c), lambda bi, i: (bi, jnp.maximum(i * sub - 1, 0), 0)),
            pl.BlockSpec((1, 16, c), lambda bi, i: (bi, 0, 0)),
            pl.BlockSpec((1, tt, c), lambda bi, i: (bi, i, 0)),
            pl.BlockSpec((None,) + pool_w.shape[1:], lambda bi, i: (layer, 0, 0, 0)),
            pl.BlockSpec((1, c), lambda bi, i: (0, 0)),
        ],
        out_specs=pl.BlockSpec((1, tt, c), lambda bi, i: (bi, i, 0)),
        out_shape=jax.ShapeDtypeStruct((b, t, c), BF16),
        compiler_params=_params("parallel", "parallel"),
        name="pool_branch",
    )(u, u, hist16, gate, pool_w, pool_scale)


def _rwkv_kernel(p_ref, sprev_ref, s0_ref, zg_ref, mu_ref, vec_ref, wl_ref, o_ref, s_ref, carry_ref,
                 *, n_valid, n_sub):
    @pl.when(pl.program_id(1) == 0)
    def _():
        s_ref[...] = s0_ref[...]
        carry_ref[...] = sprev_ref[0]

    for sub in range(n_sub):
        rows = pl.ds(sub * CHUNK, CHUNK)
        _rwkv_chunk(p_ref.at[:, rows, :], zg_ref.at[:, rows, :], mu_ref, vec_ref, wl_ref, o_ref.at[:, rows, :],
                    s_ref, carry_ref, n_valid=n_valid)


def _rwkv_chunk(p_ref, zg_ref, mu_ref, vec_ref, wl_ref, o_ref, s_ref, carry_ref, *, n_valid):
    C = CHUNK
    p = p_ref[0]
    first_row = lax.broadcasted_iota(jnp.int32, (SUBLANES, SHIFT_WIDTH), 0) == 0
    row = lax.broadcasted_iota(jnp.int32, (C, BRANCH_WIDTH), 0)
    shifted = pltpu.roll(p, 1, 0)
    prev = jnp.concatenate([jnp.where(first_row, carry_ref[...], shifted[:SUBLANES]), shifted[SUBLANES:]], axis=0)
    carry_ref[...] = p[C - 1:C, :]
    xs = p + (prev - p) * mu_ref[...]

    w0, a0, k_k, k_a, r_k, gn_w, gn_b = (vec_ref[i:i + 1, :] for i in range(7))
    lane = lax.broadcasted_iota(jnp.int32, (C, LANES), 1)
    head0 = lane < HEAD_DIM
    lin = xs[:, 3 * BRANCH_WIDTH:]
    lin = jnp.where(head0, jnp.tanh(lin), lin)
    lora = _dot(lin.astype(BF16), wl_ref[...])
    log_decay = -math.exp(-0.5) * _sigmoid(w0 + lora[:, :BRANCH_WIDTH])
    iclr = _sigmoid(a0 + lora[:, BRANCH_WIDTH:])
    r = xs[:, :BRANCH_WIDTH]
    k = xs[:, BRANCH_WIDTH:2 * BRANCH_WIDTH]
    v = xs[:, 2 * BRANCH_WIDTH:3 * BRANCH_WIDTH]
    kk_raw = k * k_k
    k2 = k * (1.0 + (iclr - 1.0) * k_a)
    if n_valid < C:
        ok = row < n_valid
        log_decay = jnp.where(ok, log_decay, 0.0)
        kk_raw = jnp.where(ok, kk_raw, 0.0)
        k2 = jnp.where(ok, k2, 0.0)
        r = jnp.where(ok, r, 0.0)
        v = jnp.where(ok, v, 0.0)

    ti = lax.broadcasted_iota(jnp.int32, (C, C), 0)
    si = lax.broadcasted_iota(jnp.int32, (C, C), 1)
    tri = (si <= ti).astype(BF16)
    cum = _dot3_left_exact(tri, log_decay)
    cum_x = cum - log_decay
    cum_c = cum[C - 1:C, :]

    ri = lax.broadcasted_iota(jnp.int32, (2 * C, 2 * C), 0)
    ci = lax.broadcasted_iota(jnp.int32, (2 * C, 2 * C), 1)

    def stack(x):
        return jnp.concatenate([jnp.where(head0, x, 0.0), jnp.where(head0, 0.0, x)], axis=0)

    def each(f, *cols):
        return [f(*xs) for xs in zip(*cols)]

    def bf(xs):
        return [x.astype(BF16) for x in xs]

    def head_sums(xs):
        lo = [jnp.sum(jnp.where(head0, x, 0.0), axis=-1, keepdims=True) for x in xs]
        hi = [jnp.sum(jnp.where(head0, 0.0, x), axis=-1, keepdims=True) for x in xs]
        return [jnp.where(head0, a, b) for a, b in zip(lo, hi)]

    def stack_b(x):
        xb = x.astype(BF16)
        zero = jnp.zeros((), BF16)
        return jnp.concatenate([jnp.where(head0, xb, zero), jnp.where(head0, zero, xb)], axis=0)

    fold_strict = (ci & (C - 1)) < (ri & (C - 1))
    fold_incl = (ci & (C - 1)) <= (ri & (C - 1))
    zeros_blk = jnp.zeros((C, LANES), BF16)

    low_lanes = lax.broadcasted_iota(jnp.int32, (C, 2 * C), 1) < C

    def unfold_p(x):
        moved = pltpu.roll(x[C:], C, 1)
        return jnp.concatenate([jnp.where(low_lanes, x[:C], 0.0), jnp.where(low_lanes, 0.0, moved)], axis=0)

    def halves(x):
        return x[:C], x[C:]

    for g0 in range(0, PAIRS, RWKV_GROUP):
        prs = list(range(g0, g0 + RWKV_GROUP))
        sls = [slice(pr * LANES, (pr + 1) * LANES) for pr in prs]
        r_p = [r[:, sl] for sl in sls]
        k_p = [k2[:, sl] for sl in sls]
        v_p = [v[:, sl] for sl in sls]
        kkr = [kk_raw[:, sl] for sl in sls]
        ic = [iclr[:, sl] for sl in sls]
        cm = [cum[:, sl] for sl in sls]
        cx = [cum_x[:, sl] for sl in sls]
        cc = [cum_c[:, sl] for sl in sls]
        n2 = head_sums(each(lambda x: x * x, kkr))
        kk = each(lambda x, n: x * lax.rsqrt(jnp.maximum(n, 1e-24)), kkr, n2)
        b_p = each(lambda x, i: x * i, kk, ic)
        e_neg = each(lambda c_: jnp.exp(-c_), cm)
        e_end = each(lambda c_, e_: jnp.exp(e_ - c_), cm, cc)
        w_c = each(jnp.exp, cc)
        rt = each(lambda x, c_: x * jnp.exp(c_), r_p, cm)
        at_b = each(lambda x, c_: stack_b(-x * jnp.exp(c_)), kk, cx)
        rt_b = each(stack_b, rt)
        v_s = each(stack, v_p)
        v_b = bf(v_s)
        bb_b = each(lambda x, e_: stack_b(x * e_), b_p, e_end)
        kb_b = each(lambda x, e_: stack_b(x * e_), k_p, e_end)
        bk_b = bf(each(lambda b_, k_, e_: jnp.concatenate([b_ * e_, k_ * e_], axis=0), b_p, k_p, e_neg))
        ar_b = each(lambda a_, r_: jnp.concatenate([a_, r_], axis=0), at_b, rt_b)
        prod = each(_dot_nt, ar_b, bk_b)
        fa = each(lambda x: jnp.where(fold_strict, x[:2 * C], 0.0), prod)
        fa_b = bf(fa)
        fr_b = bf(each(lambda x: jnp.where(fold_incl, x[2 * C:], 0.0), prod))
        a_ab = each(unfold_p, fa)
        tinv = each(lambda x: jnp.where(ri == ci, 1.0, x), a_ab)
        pw_b = bf(a_ab)
        pw_b = bf(each(_dot, pw_b, pw_b))
        n_sq = int(math.log2(C)) - 1
        for step in range(n_sq):
            if step < n_sq - 1:
                both = each(lambda t_, p_: _dot(p_, jnp.concatenate([t_.astype(BF16), p_], axis=1)), tinv, pw_b)
                tinv = each(lambda t_, x: t_ + x[:, :2 * C], tinv, both)
                pw_b = bf([x[:, 2 * C:] for x in both])
            else:
                tinv = each(lambda t_, p_: t_ + _dot(p_, t_.astype(BF16)), tinv, pw_b)
        t_b = bf(tinv)
        def akv_of(f_, v_):
            return jnp.concatenate([_dot(fh, jnp.concatenate([zeros_blk, vh], axis=0))
                                    for fh, vh in zip(halves(f_), halves(v_))], axis=0)
        akv = each(akv_of, fa_b, v_b)
        au = each(lambda t_, a_, k_: _dot(t_, jnp.concatenate([a_, k_.astype(BF16)], axis=1)), t_b, at_b, akv)
        au_b = bf(au)

        def r_side(f_, au_, v_):
            parts = [_dot(fh, jnp.concatenate([ah, jnp.concatenate([zeros_blk, vh], axis=1)], axis=0))
                     for fh, ah, vh in zip(halves(f_), halves(au_), halves(v_))]
            return parts[0] + parts[1]
        ry = each(r_side, fr_b, au_b, v_b)
        rh = each(lambda x, y_: x + y_[:, :LANES], rt, ry)
        y0 = [y_[:, LANES:] for y_ in ry]
        mg = each(lambda x, b_: _dot(x.T.astype(BF16), b_), au, bb_b)
        m_lr = [x[:LANES] for x in mg]
        g_bd = each(lambda x, v_, k_: x[LANES:] + _dot(v_.T.astype(BF16), k_), mg, v_s, kb_b)
        s_bd = [s_ref[0, pr] for pr in prs]
        s_b = bf(s_bd)
        y = each(lambda x, s_, y_: _dot_nt(x.astype(BF16), s_) + y_, rh, s_b, y0)
        s_new = each(lambda s_, w_, sb_, m_, g_: s_ * w_ + _dot(sb_, m_.astype(BF16)) + g_,
                     s_bd, w_c, s_b, m_lr, g_bd)
        for pr, s_ in zip(prs, s_new):
            s_ref[0, pr] = s_
        yc = each(lambda x, m_: x - m_ * (1.0 / HEAD_DIM), y, head_sums(y))
        var = head_sums(each(lambda x: x * x, yc))
        rk = head_sums(each(lambda r_, k_, sl: r_ * k_ * r_k[:, sl], r_p, k_p, sls))
        for sl, yc_, var_, rk_, v_ in zip(sls, yc, var, rk, v_p):
            yn = yc_ * lax.rsqrt(var_ * (1.0 / HEAD_DIM) + GN_EPS) * gn_w[:, sl] + gn_b[:, sl]
            o_ref[0, :, sl] = ((yn + rk_ * v_) * zg_ref[0, :, sl].astype(F32)).astype(o_ref.dtype)


def _rwkv_branch(p_shift, shift_prev, s0_bd, gate, mu, vecs, w_lora, n_valid):
    b, t, _ = p_shift.shape
    n_sub = math.gcd(t // CHUNK, RWKV_CHUNKS_PER_STEP)
    rows = n_sub * CHUNK
    return pl.pallas_call(
        functools.partial(_rwkv_kernel, n_valid=n_valid, n_sub=n_sub),
        grid=(b, t // rows),
        in_specs=[
            pl.BlockSpec((1, rows, SHIFT_WIDTH), lambda bi, c: (bi, c, 0)),
            pl.BlockSpec((1, 1, SHIFT_WIDTH), lambda bi, c: (bi, 0, 0)),
            pl.BlockSpec((1, PAIRS, LANES, LANES), lambda bi, c: (bi, 0, 0, 0)),
            pl.BlockSpec((1, rows, BRANCH_WIDTH), lambda bi, c: (bi, c, 1)),
            pl.BlockSpec((1, SHIFT_WIDTH), lambda bi, c: (0, 0)),
            pl.BlockSpec((8, BRANCH_WIDTH), lambda bi, c: (0, 0)),
            pl.BlockSpec((LANES, 2 * BRANCH_WIDTH), lambda bi, c: (0, 0)),
        ],
        out_specs=[
            pl.BlockSpec((1, rows, BRANCH_WIDTH), lambda bi, c: (bi, c, 0)),
            pl.BlockSpec((1, PAIRS, LANES, LANES), lambda bi, c: (bi, 0, 0, 0)),
        ],
        out_shape=[
            jax.ShapeDtypeStruct((b, t, BRANCH_WIDTH), BF16),
            jax.ShapeDtypeStruct((b, PAIRS, LANES, LANES), F32),
        ],
        scratch_shapes=[pltpu.VMEM((1, SHIFT_WIDTH), F32)],
        compiler_params=_params("parallel", "arbitrary"),
        name="rwkv_branch",
    )(p_shift, shift_prev, s0_bd, gate, mu, vecs, w_lora)


def _state_to_blockdiag(s):
    b = s.shape[0]
    sp = s.reshape(b, PAIRS, 2, HEAD_DIM, HEAD_DIM)
    z = jnp.zeros_like(sp[:, :, 0])
    top = jnp.concatenate([sp[:, :, 0], z], axis=-1)
    bot = jnp.concatenate([z, sp[:, :, 1]], axis=-1)
    return jnp.concatenate([top, bot], axis=-2)


def _state_from_blockdiag(s_bd):
    b = s_bd.shape[0]
    h0 = s_bd[:, :, :HEAD_DIM, :HEAD_DIM]
    h1 = s_bd[:, :, HEAD_DIM:, HEAD_DIM:]
    return jnp.stack([h0, h1], axis=2).reshape(b, RWKV_HEADS, HEAD_DIM, HEAD_DIM)


def _attn_kernel(*refs, n_kv_parts, tq, n_keys, masked, n_sub):
    q_ref = refs[0]
    k_refs = refs[1:1 + n_kv_parts]
    v_refs = refs[1 + n_kv_parts:1 + 2 * n_kv_parts]
    bias_ref, sink_ref, zg_ref, o_ref = refs[1 + 2 * n_kv_parts:]
    k_rows = jnp.concatenate([kr[0] for kr in k_refs], axis=0)
    v_rows = jnp.concatenate([vr[0] for vr in v_refs], axis=0)
    for sub in range(n_sub):
        rows = pl.ds(sub * tq, tq)
        _attn_chunk(q_ref.at[:, rows, :], k_rows[sub * tq:sub * tq + n_keys], v_rows[sub * tq:sub * tq + n_keys],
                    bias_ref, sink_ref, zg_ref.at[:, rows, :], o_ref.at[:, rows, :],
                    n=pl.program_id(1) * n_sub + sub, tq=tq, n_keys=n_keys, masked=masked)


def _attn_chunk(q_ref, k_seen, v_seen, bias_ref, sink_ref, zg_ref, o_ref, *, n, tq, n_keys, masked):
    pad = jnp.zeros((KEY_PAD - n_keys, KV_WIDTH), F32)
    k_all = jnp.concatenate([k_seen, pad], axis=0)
    v_all = jnp.concatenate([v_seen, pad], axis=0)
    lane =lax.broadcasted_iota(jnp.int32, (tq, LANES), 1)
    head0 = lane < HEAD_DIM
    col = lax.broadcasted_iota(jnp.int32, (1, KEY_PAD), 1)
    valid = col < n_keys
    if masked:
        valid = valid & (col // CHUNK + n - BAND_CHUNKS >= 0)
    q = q_ref[0]
    zero = jnp.zeros((), q.dtype)
    lo_half = lambda s: jnp.where(head0, s, zero)
    hi_half = lambda s: jnp.where(head0, zero, s)
    lhs, keys, vals = [], [], []
    for m in range(ATT_KV_HEADS // 2):
        slabs = [q[:, (4 * m + i) * LANES:(4 * m + i + 1) * LANES] for i in range(4)]
        k_sl = k_all[:, m * LANES:(m + 1) * LANES]
        v_sl = v_all[:, m * LANES:(m + 1) * LANES]
        lhs.append(jnp.concatenate([lo_half(slabs[0]), lo_half(slabs[1]), hi_half(slabs[2]), hi_half(slabs[3])], 0))
        lhs.append(jnp.concatenate([hi_half(slabs[0]), hi_half(slabs[1]), lo_half(slabs[2]), lo_half(slabs[3])], 0))
        keys += [k_sl.astype(BF16), pltpu.roll(k_sl, HEAD_DIM, 1).astype(BF16)]
        vals += [v_sl.astype(BF16), pltpu.roll(v_sl, HEAD_DIM, 1).astype(BF16)]
    n_grp = len(lhs)
    ones = jnp.ones((KEY_PAD, LANES), BF16)
    vals = [jnp.concatenate([x, ones], axis=1) for x in vals]
    logits = [jnp.where(valid, _dot_nt(lhs[i], keys[i]) + bias_ref[i], NEG_INF) for i in range(n_grp)]
    sink = [sink_ref[i] for i in range(n_grp)]
    mx = [jnp.maximum(jnp.max(logits[i], axis=-1, keepdims=True), sink[i]) for i in range(n_grp)]
    e = [jnp.exp(jnp.concatenate([logits[i][:, :LANES] - mx[i], logits[i][:, LANES:] - mx[i]], axis=1))
         for i in range(n_grp)]
    pv = [_dot(e[i].astype(BF16), vals[i]) for i in range(n_grp)]
    out = [pv[i][:, :LANES] / (pv[i][:, LANES:] + jnp.exp(sink[i] - mx[i])) for i in range(n_grp)]
    for m in range(ATT_KV_HEADS // 2):
        same, swap = out[2 * m], out[2 * m + 1]
        for i in range(4):
            a, b = same[i * tq:(i + 1) * tq], swap[i * tq:(i + 1) * tq]
            res = jnp.where(head0, a, b) if i < 2 else jnp.where(head0, b, a)
            sl = slice((4 * m + i) * LANES, (4 * m + i + 1) * LANES)
            o_ref[0, :, sl] = (res * zg_ref[0, :, sl].astype(F32)).astype(o_ref.dtype)


def _attn_prompt(q, kv, bias, sink_rows, gate):
    b, t, _ = q.shape
    nc = t // CHUNK
    n_sub = math.gcd(nc, ATTN_CHUNKS_PER_STEP)
    rows = n_sub * CHUNK
    k_specs, v_specs = ([pl.BlockSpec((1, CHUNK, KV_WIDTH),
                                      lambda bi, n, d=d, c=c: (bi, jnp.maximum(n * n_sub - d, 0), c))
                         for d in range(BAND_CHUNKS, 0, -1)]
                        + [pl.BlockSpec((1, rows, KV_WIDTH), lambda bi, n, c=c: (bi, n, c))] for c in range(2))
    return pl.pallas_call(
        functools.partial(_attn_kernel, n_kv_parts=BAND_CHUNKS + 1, tq=CHUNK,
                          n_keys=(BAND_CHUNKS + 1) * CHUNK, masked=True, n_sub=n_sub),
        grid=(b, nc // n_sub),
        in_specs=[pl.BlockSpec((1, rows, BRANCH_WIDTH), lambda bi, n: (bi, n, 0))]
        + k_specs + v_specs
        + [pl.BlockSpec(bias.shape, lambda bi, n: (0, 0, 0)),
           pl.BlockSpec(sink_rows.shape, lambda bi, n: (0, 0, 0)),
           pl.BlockSpec((1, rows, BRANCH_WIDTH), lambda bi, n: (bi, n, 2))],
        out_specs=pl.BlockSpec((1, rows, BRANCH_WIDTH), lambda bi, n: (bi, n, 0)),
        out_shape=jax.ShapeDtypeStruct((b, t, BRANCH_WIDTH), BF16),
        compiler_params=_params("parallel", "parallel"),
        name="attn_prompt",
    )(q, kv, kv, kv, kv, kv, kv, bias, sink_rows, gate)


def _attn_sample(q, kv, k_cache, v_cache, bias, sink_rows, gate):
    b, t, _ = q.shape
    n_cache = k_cache.shape[1]
    cache_spec = pl.BlockSpec((1, n_cache, KV_WIDTH), lambda bi, n: (bi, 0, 0))
    return pl.pallas_call(
        functools.partial(_attn_kernel, n_kv_parts=2, tq=t, n_keys=n_cache + t, masked=False, n_sub=1),
        grid=(b, 1),
        in_specs=[pl.BlockSpec((1, t, BRANCH_WIDTH), lambda bi, n: (bi, 0, 0)),
                  cache_spec, pl.BlockSpec((1, t, KV_WIDTH), lambda bi, n: (bi, 0, 0)),
                  cache_spec, pl.BlockSpec((1, t, KV_WIDTH), lambda bi, n: (bi, 0, 1)),
                  pl.BlockSpec(bias.shape, lambda bi, n: (0, 0, 0)),
                  pl.BlockSpec(sink_rows.shape, lambda bi, n: (0, 0, 0)),
                  pl.BlockSpec((1, t, BRANCH_WIDTH), lambda bi, n: (bi, 0, 2))],
        out_specs=pl.BlockSpec((1, t, BRANCH_WIDTH), lambda bi, n: (bi, 0, 0)),
        out_shape=jax.ShapeDtypeStruct((b, t, BRANCH_WIDTH), BF16),
        compiler_params=_params("parallel", "parallel"),
        name="attn_sample",
    )(q, k_cache, kv, v_cache, kv, bias, sink_rows, gate)


def _t5_bucket(rel):
    half = N_BUCKETS // 2
    n = -rel
    ret = jnp.where(n < 0, half, 0)
    n = jnp.abs(n)
    max_exact = half // 2
    large = max_exact + (jnp.log(jnp.maximum(n, 1).astype(jnp.float32) / max_exact)
                         / math.log(MAX_DISTANCE / max_exact) * (half - max_exact)).astype(jnp.int32)
    large = jnp.minimum(large, half - 1)
    return ret + jnp.where(n < max_exact, n, large)


def _bias_kernel(tab_ref, onehot_ref, o_ref):
    o_ref[...] = _dot3_right_exact(tab_ref[...], onehot_ref[...])


def _rel_bias(table, key_pos, n_q):
    n_keys = key_pos.shape[0]
    bucket = _t5_bucket(key_pos[None, :] - jnp.arange(n_q)[:, None])
    bucket = jnp.pad(bucket, ((0, 0), (0, KEY_PAD - n_keys)), constant_values=-1)
    onehot = (bucket.reshape(1, -1) == jnp.arange(N_BUCKETS)[:, None]).astype(BF16)
    out = pl.pallas_call(
        _bias_kernel,
        out_shape=jax.ShapeDtypeStruct((ATT_HEADS, n_q * KEY_PAD), F32),
        name="rel_bias",
    )(table.T[np.array(ATT_HEAD_ORDER)], onehot)
    return out.reshape(len(ATT_HEAD_ORDER) // 4, 4 * n_q, KEY_PAD)


def _merge_kernel(ga_ref, gb_ref, gc_ref, w_ref, m0_ref, m1_ref, m2_ref, o_ref):
    acc = m0_ref[...].astype(F32) * _dot(ga_ref[...], w_ref[0])
    acc = acc + m1_ref[...].astype(F32) * _dot(gb_ref[...], w_ref[1])
    acc = acc + m2_ref[...].astype(F32) * _dot(gc_ref[...], w_ref[2])
    o_ref[...] = acc.astype(o_ref.dtype)


def _merge(ga, gb, gc, w_branch, layer, mg):
    m, kdim = ga.shape
    tm = min(MERGE_TM, m)
    tn = MERGE_TN
    nj = D_MODEL // tn
    g_spec = pl.BlockSpec((tm, kdim), lambda j, i: (i, 0))
    m_specs = [pl.BlockSpec((tm, tn), lambda j, i, br=br: (i, br * nj + j)) for br in range(N_BRANCH)]
    return pl.pallas_call(
        _merge_kernel,
        grid=(nj, m // tm),
        in_specs=[g_spec, g_spec, g_spec,
                  pl.BlockSpec((None, N_BRANCH, kdim, tn), lambda j, i: (layer, 0, 0, j))] + m_specs,
        out_specs=pl.BlockSpec((tm, tn), lambda j, i: (i, j)),
        out_shape=jax.ShapeDtypeStruct((m, D_MODEL), BF16),
        compiler_params=_params("arbitrary", "arbitrary"),
        name="merge",
    )(ga, gb, gc, w_branch, mg, mg, mg)


def _out_kernel(a_ref, w_ref, x_ref, g_ref, *o_refs, final_norm):
    y = x_ref[...] + _dot(a_ref[...], w_ref[...])
    ms = jnp.mean(y * y, axis=-1, keepdims=True)
    normed = y * lax.rsqrt(ms + RMS_EPS) * g_ref[...]
    if final_norm:
        o_refs[0][...] = normed
    else:
        o_refs[0][...] = y
        o_refs[1][...] = normed.astype(o_refs[1].dtype)


def _out_proj(merged, w_out, layer, x2d, gain, final_norm):
    m, d = x2d.shape
    tm = min(ROW_TM, m)
    row_spec = pl.BlockSpec((tm, d), lambda i: (i, 0))
    out_specs, out_shape = [row_spec], [jax.ShapeDtypeStruct((m, d), F32)]
    if not final_norm:
        out_specs.append(row_spec)
        out_shape.append(jax.ShapeDtypeStruct((m, d), BF16))
    return pl.pallas_call(
        functools.partial(_out_kernel, final_norm=final_norm),
        grid=(m // tm,),
        in_specs=[row_spec, pl.BlockSpec((None, d, d), lambda i: (layer, 0, 0)), row_spec,
                  pl.BlockSpec((1, d), lambda i: (0, 0))],
        out_specs=out_specs,
        out_shape=out_shape,
        compiler_params=_params("parallel"),
        name="out_proj",
    )(merged, w_out, x2d, gain.reshape(1, d))


def _layer_vectors(l, pool_scale, rwkv_mu, rwkv_w0, rwkv_w_up, rwkv_a0, rwkv_a_up, rwkv_k_k, rwkv_k_a, rwkv_r_k,
                   rwkv_gn_w, rwkv_gn_b, attn_sink):
    w_lora = jnp.zeros((LANES, 2 * BRANCH_WIDTH), F32)
    w_lora = w_lora.at[:LORA_RANK, :BRANCH_WIDTH].set(rwkv_w_up[l]).at[LORA_RANK:, BRANCH_WIDTH:].set(rwkv_a_up[l])
    vecs = jnp.stack([rwkv_w0[l], rwkv_a0[l], rwkv_k_k[l], rwkv_k_a[l], rwkv_r_k[l], rwkv_gn_w[l], rwkv_gn_b[l],
                      jnp.zeros_like(rwkv_w0[l])])
    return dict(pool_scale=pool_scale[l].reshape(1, -1), mu=rwkv_mu[l].reshape(1, -1), vecs=vecs,
                w_lora=w_lora.astype(BF16), sink=attn_sink[l][np.array(ATT_HEAD_ORDER)])


def _sink_rows(sink_ordered, n_q):
    rows = jnp.repeat(sink_ordered.reshape(-1, 4), n_q, axis=1)
    return jnp.broadcast_to(rows[..., None], rows.shape + (LANES,))


def _mixer_layer(x2, h, b, l, wts, lv, next_gain, final_norm, *, pos0, pool_hist16, shift_prev, s0_bd, bias, caches):
    t = x2.shape[0] // b
    fuse_pool = pool_hist16 is None and t % min(POOL_FUSED_TM, t) == 0
    proj = {name: _matmul(h, wts["w_in"], l, col0, n, tn=tn, tm=tm, act=act, out_dtype=dt, name="proj_" + name)
            for name, (col0, n, tn, tm, act, dt) in PROJ_SEGMENTS.items()
            if not (fuse_pool and name in ("pool", "kv"))}
    p_shift, q, gate = (proj[name].reshape(b, t, -1) for name in ("shift", "q", "gate"))
    mg = proj["merge"]

    if fuse_pool:
        u_pool, ga, kv = _proj_pool(h, wts["w_in"], l, proj["gate"], wts["pool_w"], lv["pool_scale"], t)
        u_pool, kv = u_pool.reshape(b, t, -1), kv.reshape(b, t, -1)
    else:
        u_pool, kv = proj["pool"].reshape(b, t, -1), proj["kv"].reshape(b, t, -1)
        if pool_hist16 is None:
            pool_hist16 = jnp.zeros((b, 16, BRANCH_WIDTH), F32)
        ga = _pool_branch(u_pool, pool_hist16, gate, wts["pool_w"], l, lv["pool_scale"], pos0)

    t_pad = -(-t // CHUNK) * CHUNK
    if t_pad != t:
        p_in = jnp.pad(p_shift, ((0, 0), (0, t_pad - t), (0, 0)))
        gate_in = jnp.pad(gate, ((0, 0), (0, t_pad - t), (0, 0)))
    else:
        p_in, gate_in = p_shift, gate
    gb, s_bd = _rwkv_branch(p_in, shift_prev, s0_bd, gate_in, lv["mu"], lv["vecs"], lv["w_lora"],
                            n_valid=CHUNK - (t_pad - t))
    gb = gb[:, :t]

    if caches is None:
        gc = _attn_prompt(q, kv, bias, _sink_rows(lv["sink"], CHUNK), gate)
    else:
        gc = _attn_sample(q, kv, caches[0], caches[1], bias, _sink_rows(lv["sink"], t), gate)

    merged = _merge(ga.reshape(b * t, -1), gb.reshape(b * t, -1), gc.reshape(b * t, -1), wts["w_branch"], l, mg)
    outs = _out_proj(merged, wts["w_out"], l, x2, next_gain, final_norm)
    y, h_next = (outs[0], None) if final_norm else outs
    return (y, h_next, kv[..., :KV_WIDTH], kv[..., KV_WIDTH:], _state_from_blockdiag(s_bd), p_shift[:, -1],
            u_pool[:, t - POOL_HIST:])


def kernel(x_prompt, x_sample, cache_attn_k, cache_attn_v, state_rwkv, state_rwkv_shift, state_pool, norm_g, w_in,
           pool_w, pool_scale, rwkv_mu, rwkv_w0, rwkv_w_up, rwkv_a0, rwkv_a_up, rwkv_k_k, rwkv_k_a, rwkv_r_k,
           rwkv_gn_w, rwkv_gn_b, attn_sink, rel_pos_table, w_branch, w_out, final_norm_g):
    depth = norm_g.shape[0]
    bp, tp, _ = x_prompt.shape
    bs, ts, _ = x_sample.shape
    n_cache = cache_attn_k.shape[2]

    key_p = jnp.arange((BAND_CHUNKS + 1) * CHUNK) - BAND_CHUNKS * CHUNK
    bias_p = _rel_bias(rel_pos_table, key_p, CHUNK)
    key_s = jnp.concatenate([jnp.arange(n_cache) - n_cache, jnp.arange(ts)])
    bias_s = _rel_bias(rel_pos_table, key_s, ts)

    wts = dict(w_in=w_in.astype(BF16), pool_w=pool_w.astype(BF16), w_branch=w_branch.astype(BF16),
               w_out=w_out.astype(BF16))
    xp, xs = x_prompt.reshape(bp * tp, -1), x_sample.reshape(bs * ts, -1)
    hp = _rmsnorm(xp, norm_g[0], BF16, ROW_TM)
    hs = _rmsnorm(xs, norm_g[0], BF16, ROW_TM)
    outs_p, outs_s = [], []
    for l in range(depth):
        lv = _layer_vectors(l, pool_scale, rwkv_mu, rwkv_w0, rwkv_w_up, rwkv_a0, rwkv_a_up, rwkv_k_k, rwkv_k_a,
                            rwkv_r_k, rwkv_gn_w, rwkv_gn_b, attn_sink)
        last = l == depth - 1
        next_gain = final_norm_g if last else norm_g[l + 1]
        xp, hp, kp, vp, sp, shp, plp = _mixer_layer(
            xp, hp, bp, l, wts, lv, next_gain, last, pos0=0,
            pool_hist16=None,
            shift_prev=jnp.zeros((bp, 1, SHIFT_WIDTH), F32),
            s0_bd=jnp.zeros((bp, PAIRS, LANES, LANES), F32),
            bias=bias_p, caches=None)
        outs_p.append((kp[:, -WINDOW:].reshape(bp, WINDOW, ATT_KV_HEADS, HEAD_DIM),
                       vp[:, -WINDOW:].reshape(bp, WINDOW, ATT_KV_HEADS, HEAD_DIM), sp, shp, plp))
        xs, hs, k_s, v_s, s_s, sh_s, pl_s = _mixer_layer(
            xs, hs, bs, l, wts, lv, next_gain, last, pos0=PAST_LEN,
            pool_hist16=jnp.pad(state_pool[l], ((0, 0), (1, 0), (0, 0))),
            shift_prev=state_rwkv_shift[l][:, None, :],
            s0_bd=_state_to_blockdiag(state_rwkv[l]),
            bias=bias_s,
            caches=(cache_attn_k[l].reshape(bs, n_cache, KV_WIDTH), cache_attn_v[l].reshape(bs, n_cache, KV_WIDTH)))
        outs_s.append((k_s.reshape(bs, ts, ATT_KV_HEADS, HEAD_DIM), v_s.reshape(bs, ts, ATT_KV_HEADS, HEAD_DIM),
                       s_s, sh_s, pl_s))

    st = lambda outs, i: jnp.stack([o[i] for o in outs])
    return (xp.reshape(x_prompt.shape), xs.reshape(x_sample.shape), st(outs_p, 0), st(outs_p, 1), st(outs_s, 0), st(outs_s, 1),
            st(outs_p, 2), st(outs_s, 2), st(outs_p, 3), st(outs_s, 3), st(outs_p, 4), st(outs_s, 4))
```
